```python
import jax
import jax.numpy as jnp
from jax import lax
import numpy as np


D_MODEL = 1024
BATCH = 8
SEQ = 8192
DEPTH = 2

N_EVEN = (DEPTH + 1) // 2
N_ODD = DEPTH // 2

GLA_HEADS = 4
GLA_DK = D_MODEL // 2 // GLA_HEADS
GLA_DV = D_MODEL // GLA_HEADS
GLA_KEY_WIDTH = GLA_HEADS * GLA_DK
GLA_VALUE_WIDTH = GLA_HEADS * GLA_DV
GLA_GATE_RANK = 16
GLA_GATE_TAU = 16.0
GLA_CHUNK = 64
GLA_SPLITS = tuple(int(s) for s in np.cumsum([GLA_KEY_WIDTH, GLA_KEY_WIDTH, GLA_VALUE_WIDTH, GLA_VALUE_WIDTH, GLA_GATE_RANK]))
GLA_IN_WIDTH = 2 * GLA_KEY_WIDTH + 2 * GLA_VALUE_WIDTH + 2 * GLA_GATE_RANK

N_Q_HEADS = 16
N_KV_HEADS = 4
HEAD_DIM = 64
GROUP = N_Q_HEADS // N_KV_HEADS
WINDOW = 128
ATT_BLOCK = 128
SWA_QKV_WIDTH = (N_Q_HEADS + 2 * N_KV_HEADS) * HEAD_DIM

D_FF = 2816
N_EXPERTS = 8
TOP_K = 2

NORM_EPS = 1e-5

kernel_name = 'hybrid_gla_swa_moe_encoder'


def rms_norm(x, gain):
    xf = x.astype(jnp.float32)
    y = xf * lax.rsqrt(jnp.mean(xf * xf, axis=-1, keepdims=True) + NORM_EPS)
    return (y * gain.astype(jnp.float32)).astype(x.dtype)


def alibi_slopes(n_heads):
    return jnp.asarray(np.power(2.0, -8.0 * np.arange(1, n_heads + 1) / n_heads).astype(np.float32))


def gla_chunked(q, k, v, log_a, inclusive):
    B, H, S, dk = q.shape
    dv = v.shape[-1]
    C = GLA_CHUNK
    n = S // C
    q = q.reshape(B, H, n, C, dk)
    k = k.reshape(B, H, n, C, dk)
    v = v.reshape(B, H, n, C, dv)
    b = jnp.cumsum(log_a.reshape(B, H, n, C, dk), axis=3)
    b_ref = b[:, :, :, C // 2:C // 2 + 1]
    qe = q * jnp.exp(b - b_ref)
    ke = k * jnp.exp(b_ref - b)
    scores = jnp.einsum('bhnik,bhnjk->bhnij', qe, ke)
    mask = jnp.tril(jnp.ones((C, C), dtype=bool), 0 if inclusive else -1)
    scores = jnp.where(mask, scores, 0.0)
    o_intra = jnp.einsum('bhnij,bhnjv->bhniv', scores, v)
    b_last = b[:, :, :, -1:]
    chunk_update = jnp.einsum('bhnjk,bhnjv->bhnkv', k * jnp.exp(b_last - b), v)
    chunk_decay = jnp.exp(b_last[:, :, :, 0])

    def step(state, inp):
        decay, upd = inp
        return state * decay[..., None] + upd, state

    init = jnp.zeros((B, H, dk, dv), jnp.float32)
    _, prev = lax.scan(step, init, (jnp.moveaxis(chunk_decay, 2, 0), jnp.moveaxis(chunk_update, 2, 0)))
    prev = jnp.moveaxis(prev, 0, 2)
    o_inter = jnp.einsum('bhnik,bhnkv->bhniv', q * jnp.exp(b), prev)
    return (o_intra + o_inter).reshape(B, H, S, dv)


def gla_mixer(h, w_in, wg_f, bg_f, wg_b, bg_b, head_gain, w_out):
    B, S, _ = h.shape
    proj = (h @ w_in).astype(jnp.float32)
    q, k, v, r, lr_f, lr_b = jnp.split(proj, GLA_SPLITS, axis=-1)

    def heads(t, d):
        return t.reshape(B, S, GLA_HEADS, d).transpose(0, 2, 1, 3)

    q = heads(q, GLA_DK) * (GLA_DK ** -0.5)
    k = heads(k, GLA_DK)
    v = heads(v, GLA_DV)
    log_a_f = heads(jax.nn.log_sigmoid(lr_f @ wg_f.astype(jnp.float32) + bg_f.astype(jnp.float32)) / GLA_GATE_TAU, GLA_DK)
    log_a_b = heads(jax.nn.log_sigmoid(lr_b @ wg_b.astype(jnp.float32) + bg_b.astype(jnp.float32)) / GLA_GATE_TAU, GLA_DK)
    o_f = gla_chunked(q, k, v, log_a_f, True)
    flip = lambda t: jnp.flip(t, axis=2)
    o_b = flip(gla_chunked(flip(q), flip(k), flip(v), flip(log_a_b), False))
    o = (o_f + o_b).transpose(0, 2, 1, 3)
    o = o * lax.rsqrt(jnp.mean(o * o, axis=-1, keepdims=True) + NORM_EPS)
    o = o * head_gain.astype(jnp.float32).reshape(GLA_HEADS, GLA_DV)
    o = o.reshape(B, S, GLA_VALUE_WIDTH) * jax.nn.silu(r)
    return o.astype(h.dtype) @ w_out


def swa_mixer(h, w_qkv, b_qkv, sinks, w_out, b_out):
    B, S, _ = h.shape
    T = ATT_BLOCK
    n = S // T
    qkv = (h @ w_qkv + b_qkv).astype(jnp.float32)
    q, k, v = jnp.split(qkv, (N_Q_HEADS * HEAD_DIM, (N_Q_HEADS + N_KV_HEADS) * HEAD_DIM), axis=-1)
    q = q.reshape(B, n, T, N_KV_HEADS, GROUP, HEAD_DIM) * (HEAD_DIM ** -0.5)
    k = k.reshape(B, n, T, N_KV_HEADS, HEAD_DIM)
    v = v.reshape(B, n, T, N_KV_HEADS, HEAD_DIM)
    pad = ((0, 0), (1, 1), (0, 0), (0, 0), (0, 0))
    kp = jnp.pad(k, pad)
    vp = jnp.pad(v, pad)
    kw = jnp.concatenate([kp[:, :-2], kp[:, 1:-1], kp[:, 2:]], axis=2)
    vw = jnp.concatenate([vp[:, :-2], vp[:, 1:-1], vp[:, 2:]], axis=2)
    scores = jnp.einsum('bnqhgd,bnkhd->bnhgqk', q, kw)
    q_pos = jnp.arange(n)[:, None] * T + jnp.arange(T)[None, :]
    k_pos = (jnp.arange(n)[:, None] - 1) * T + jnp.arange(3 * T)[None, :]
    dist = jnp.abs(q_pos[:, :, None] - k_pos[:, None, :])
    valid = (dist <= WINDOW) & (k_pos >= 0)[:, None, :] & (k_pos < S)[:, None, :]
    slopes = alibi_slopes(N_Q_HEADS).reshape(N_KV_HEADS, GROUP)
    bias = jnp.where(valid[:, None, None], -slopes[None, :, :, None, None] * dist[:, None, None].astype(jnp.float32), -jnp.inf)
    scores = scores + bias[None]
    sink = sinks.astype(jnp.float32).reshape(N_KV_HEADS, GROUP)[None, None, :, :, None, None]
    m = jnp.maximum(jnp.max(scores, axis=-1, keepdims=True), sink)
    p = jnp.exp(scores - m)
    denom = jnp.sum(p, axis=-1, keepdims=True) + jnp.exp(sink - m)
    out = jnp.einsum('bnhgqk,bnkhd->bnqhgd', p / denom, vw)
    out = out.reshape(B, S, N_Q_HEADS * HEAD_DIM).astype(h.dtype)
    return out @ w_out + b_out


def swiglu(h, w_gate, w_up, w_down):
    return (jax.nn.silu(h @ w_gate) * (h @ w_up)) @ w_down


def moe_swiglu(h, router, w_gate, w_up, w_down):
    B, S, D = h.shape
    t = h.reshape(B * S, D)
    logits = (t @ router).astype(jnp.float32)
    top_val, top_idx = lax.top_k(logits, TOP_K)
    top_w = jax.nn.softmax(top_val, axis=-1)
    combine = jnp.sum(jax.nn.one_hot(top_idx, N_EXPERTS, dtype=jnp.float32) * top_w[..., None], axis=1)
    out = jnp.zeros((B * S, D), jnp.float32)
    for e in range(N_EXPERTS):
        out = out + combine[:, e:e + 1] * swiglu(t, w_gate[e], w_up[e], w_down[e]).astype(jnp.float32)
    return out.astype(h.dtype).reshape(B, S, D)


def _normal(key, shape, scale):
    return jax.random.normal(key, shape, jnp.float32) * scale


def setup_inputs(seed: int = 0) -> dict:
    key = jax.random.key(seed)
    ks = jax.random.split(key, 24)
    D, F, E = D_MODEL, D_FF, N_EXPERTS
    return {
        'x': _normal(ks[0], (BATCH, SEQ, D), 1.0),
        'mix_norm': 1.0 + _normal(ks[1], (DEPTH, D), 0.02),
        'ffn_norm': 1.0 + _normal(ks[2], (DEPTH, D), 0.02),
        'gla_in_proj': _normal(ks[3], (N_EVEN, D, GLA_IN_WIDTH), D ** -0.5),
        'gla_gate_w_fwd': _normal(ks[4], (N_EVEN, GLA_GATE_RANK, GLA_KEY_WIDTH), GLA_GATE_RANK ** -0.5),
        'gla_gate_b_fwd': _normal(ks[5], (N_EVEN, GLA_KEY_WIDTH), 0.1),
        'gla_gate_w_bwd': _normal(ks[6], (N_EVEN, GLA_GATE_RANK, GLA_KEY_WIDTH), GLA_GATE_RANK ** -0.5),
        'gla_gate_b_bwd': _normal(ks[7], (N_EVEN, GLA_KEY_WIDTH), 0.1),
        'gla_head_norm': 1.0 + _normal(ks[8], (N_EVEN, GLA_VALUE_WIDTH), 0.02),
        'gla_out_proj': _normal(ks[9], (N_EVEN, GLA_VALUE_WIDTH, D), GLA_VALUE_WIDTH ** -0.5),
        'swa_qkv_proj': _normal(ks[10], (N_ODD, D, SWA_QKV_WIDTH), D ** -0.5),
        'swa_qkv_bias': _normal(ks[11], (N_ODD, SWA_QKV_WIDTH), 0.02),
        'swa_sinks': _normal(ks[12], (N_ODD, N_Q_HEADS), 0.5),
        'swa_out_proj': _normal(ks[13], (N_ODD, N_Q_HEADS * HEAD_DIM, D), (N_Q_HEADS * HEAD_DIM) ** -0.5),
        'swa_out_bias': _normal(ks[14], (N_ODD, D), 0.02),
        'dense_w_gate': _normal(ks[15], (N_EVEN, D, F), D ** -0.5),
        'dense_w_up': _normal(ks[16], (N_EVEN, D, F), D ** -0.5),
        'dense_w_down': _normal(ks[17], (N_EVEN, F, D), F ** -0.5),
        'moe_router': _normal(ks[18], (N_ODD, D, E), D ** -0.5),
        'moe_w_gate': _normal(ks[19], (N_ODD, E, D, F), D ** -0.5),
        'moe_w_up': _normal(ks[20], (N_ODD, E, D, F), D ** -0.5),
        'moe_w_down': _normal(ks[21], (N_ODD, E, F, D), F ** -0.5),
        'final_norm': 1.0 + _normal(ks[22], (D,), 0.02),
    }


def reference(x, mix_norm, ffn_norm, gla_in_proj, gla_gate_w_fwd, gla_gate_b_fwd, gla_gate_w_bwd, gla_gate_b_bwd,
              gla_head_norm, gla_out_proj, swa_qkv_proj, swa_qkv_bias, swa_sinks, swa_out_proj, swa_out_bias,
              dense_w_gate, dense_w_up, dense_w_down, moe_router, moe_w_gate, moe_w_up, moe_w_down, final_norm):
    h = x
    for i in range(DEPTH):
        j = i // 2
        hn = rms_norm(h, mix_norm[i])
        if i % 2 == 0:
            h = h + gla_mixer(hn, gla_in_proj[j], gla_gate_w_fwd[j], gla_gate_b_fwd[j], gla_gate_w_bwd[j],
                              gla_gate_b_bwd[j], gla_head_norm[j], gla_out_proj[j])
        else:
            h = h + swa_mixer(hn, swa_qkv_proj[j], swa_qkv_bias[j], swa_sinks[j], swa_out_proj[j], swa_out_bias[j])
        hn = rms_norm(h, ffn_norm[i])
        if i % 2 == 0:
            h = h + swiglu(hn, dense_w_gate[j], dense_w_up[j], dense_w_down[j])
        else:
            h = h + moe_swiglu(hn, moe_router[j], moe_w_gate[j], moe_w_up[j], moe_w_down[j])
    return rms_norm(h, final_norm)
```

```python
import functools

import jax
import jax.numpy as jnp
from jax import lax
from jax.experimental import pallas as pl
from jax.experimental.pallas import tpu as pltpu

F32 = jnp.float32
BF16 = jnp.bfloat16

NORM_EPS = 1e-5

GLA_HEADS = 4
GLA_GATE_RANK = 16
GLA_GATE_TAU = 16.0
GLA_CHUNK = 64
N_Q_HEADS = 16
N_KV_HEADS = 4
HEAD_DIM = 64
GROUP = N_Q_HEADS // N_KV_HEADS
WINDOW = 128
ATT_BLOCK = 128
TOP_K = 2

LANES = 128
VMEM_LIMIT_BYTES = 56 * 2**20

ROW_TILE = 512
GLA_BLOCK = 512
FFN_TILE = 512
FFN_F_TILE = 256
MOE_TILE = 512


def _params(*sem):
    return pltpu.CompilerParams(dimension_semantics=sem, vmem_limit_bytes=VMEM_LIMIT_BYTES)


def _rms(x, gain):
    y = x * lax.rsqrt(jnp.mean(x * x, axis=-1, keepdims=True) + NORM_EPS)
    return y * gain


def _silu(x):
    return x * (1.0 / (1.0 + jnp.exp(-x)))


def _norm_proj_kernel(x_ref, g_ref, w_ref, b_ref, *o_refs, splits):
    y = _rms(x_ref[...], g_ref[...]).astype(BF16)
    for (start, width), o_ref in zip(splits, o_refs):
        acc = jnp.dot(y, w_ref[:, start:start + width], preferred_element_type=F32)
        acc = acc + b_ref[:, start:start + width]
        o_ref[...] = acc.astype(o_ref.dtype)


def _norm_proj(x, gain, w, bias, splits, dtypes, name):
    n, d = x.shape
    nout = w.shape[1]
    tm = ROW_TILE
    assert n % tm == 0
    return pl.pallas_call(
        functools.partial(_norm_proj_kernel, splits=splits),
        grid=(n // tm,),
        in_specs=[
            pl.BlockSpec((tm, d), lambda i: (i, 0)),
            pl.BlockSpec((1, d), lambda i: (0, 0)),
            pl.BlockSpec((d, nout), lambda i: (0, 0)),
            pl.BlockSpec((1, nout), lambda i: (0, 0)),
        ],
        out_specs=[pl.BlockSpec((tm, wd), lambda i: (i, 0)) for (_, wd) in splits],
        out_shape=[jax.ShapeDtypeStruct((n, wd), dt) for (_, wd), dt in zip(splits, dtypes)],
        compiler_params=_params("arbitrary"),
        name=name,
    )(x, gain.reshape(1, d), w, bias.reshape(1, nout))


def _gla_direction(q_ref, k_ref, v_ref, lr_ref, wg_ref, bg_ref, state_ref, reverse):
    rows, dk = q_ref.shape
    c = GLA_CHUNK
    nc = rows // c
    q = q_ref[...].astype(F32) * (dk ** -0.5)
    k = k_ref[...].astype(F32)
    v = v_ref[...]
    z = jnp.dot(lr_ref[...].astype(BF16), wg_ref[...], preferred_element_type=F32) + bg_ref[...]
    la = (jnp.minimum(z, 0.0) - jnp.log(1.0 + jnp.exp(-jnp.abs(z)))) * (1.0 / GLA_GATE_TAU)
    la_hi = la.astype(BF16)
    la_lo = (la - la_hi.astype(F32)).astype(BF16)

    ri = lax.broadcasted_iota(jnp.int32, (c, c), 0)
    ci = lax.broadcasted_iota(jnp.int32, (c, c), 1)
    if reverse:
        cum_mask = ci >= ri
        att_mask = ci > ri
        ref_row, last_row = c // 2 - 1, 0
    else:
        cum_mask = ci <= ri
        att_mask = ci <= ri
        ref_row, last_row = c // 2, c - 1
    cum = jnp.where(cum_mask, 1.0, 0.0).astype(BF16)
    ones = jnp.ones((c, LANES), BF16)
    tn = (((0,), (0,)), ((), ()))
    nt = (((1,), (1,)), ((), ()))

    o_intra, upd, decay, qb = [], [], [], []
    for j in range(nc):
        sl = slice(j * c, (j + 1) * c)
        hi, lo = la_hi[sl], la_lo[sl]
        b = (jnp.dot(cum, hi, preferred_element_type=F32)
             + jnp.dot(cum, lo, preferred_element_type=F32))
        b_ref = b[ref_row:ref_row + 1]
        b_last = b[last_row:last_row + 1]
        qe = (q[sl] * jnp.exp(b - b_ref)).astype(BF16)
        ke = (k[sl] * jnp.exp(b_ref - b)).astype(BF16)
        s = lax.dot_general(qe, ke, nt, preferred_element_type=F32)
        s = jnp.where(att_mask, s, 0.0).astype(BF16)
        o_intra.append(jnp.dot(s, v[sl], preferred_element_type=F32))
        kd = (k[sl] * jnp.exp(b_last - b)).astype(BF16)
        upd.append(lax.dot_general(kd, v[sl], tn, preferred_element_type=F32))
        bl = (lax.dot_general(hi, ones, tn, preferred_element_type=F32)
              + lax.dot_general(lo, ones, tn, preferred_element_type=F32))
        decay.append(jnp.exp(bl))
        qb.append((q[sl] * jnp.exp(b)).astype(BF16))

    dv = v.shape[1]
    state = state_ref[...]
    outs = [None] * nc
    for j in (range(nc - 1, -1, -1) if reverse else range(nc)):
        outs[j] = o_intra[j] + jnp.dot(qb[j], state.astype(BF16), preferred_element_type=F32)
        d = jnp.concatenate([decay[j]] * (dv // LANES), axis=1)
        state = state * d + upd[j]
    state_ref[...] = state
    return jnp.concatenate(outs, axis=0)


def _gla_kernel(q_ref, k_ref, v_ref, lr_ref, r_ref, wgf_ref, bgf_ref, wgb_ref, bgb_ref, gain_ref,
                o_ref, state_ref, oacc_ref, *, nblk):
    i = pl.program_id(2)
    rows = q_ref.shape[0]

    @pl.when((i == 0) | (i == nblk))
    def _():
        state_ref[...] = jnp.zeros_like(state_ref)

    @pl.when(i < nblk)
    def _():
        o = _gla_direction(q_ref, k_ref, v_ref, lr_ref, wgf_ref, bgf_ref, state_ref, False)
        oacc_ref[pl.ds(pl.multiple_of(i * rows, rows), rows), :] = o

    @pl.when(i >= nblk)
    def _():
        j = 2 * nblk - 1 - i
        o = _gla_direction(q_ref, k_ref, v_ref, lr_ref, wgb_ref, bgb_ref, state_ref, True)
        o = o + oacc_ref[pl.ds(pl.multiple_of(j * rows, rows), rows), :]
        o = _rms(o, gain_ref[...])
        o_ref[...] = (o * _silu(r_ref[...].astype(F32))).astype(o_ref.dtype)


def _gla(q, k, v, lr, r, wgf, bgf, wgb, bgb, gain, batch, seq):
    n = q.shape[0]
    heads = GLA_HEADS
    dk = q.shape[1] // heads
    dv = v.shape[1] // heads
    rows = GLA_BLOCK
    nblk = seq // rows
    assert seq % rows == 0 and rows % GLA_CHUNK == 0

    def blk(i):
        return jnp.where(i < nblk, i, 2 * nblk - 1 - i)

    def late_blk(i):
        return jnp.where(i < nblk, nblk - 1, 2 * nblk - 1 - i)

    row_map = lambda b, h, i: (b * nblk + blk(i), h)
    return pl.pallas_call(
        functools.partial(_gla_kernel, nblk=nblk),
        grid=(batch, heads, 2 * nblk),
        in_specs=[
            pl.BlockSpec((rows, dk), row_map),
            pl.BlockSpec((rows, dk), row_map),
            pl.BlockSpec((rows, dv), row_map),
            pl.BlockSpec((rows, lr.shape[1]), lambda b, h, i: (b * nblk + blk(i), 0)),
            pl.BlockSpec((rows, dv), lambda b, h, i: (b * nblk + late_blk(i), h)),
            pl.BlockSpec((wgf.shape[0], dk), lambda b, h, i: (0, h)),
            pl.BlockSpec((1, dk), lambda b, h, i: (0, h)),
            pl.BlockSpec((wgb.shape[0], dk), lambda b, h, i: (0, h)),
            pl.BlockSpec((1, dk), lambda b, h, i: (0, h)),
            pl.BlockSpec((1, dv), lambda b, h, i: (0, h)),
        ],
        out_specs=pl.BlockSpec((rows, dv), lambda b, h, i: (b * nblk + late_blk(i), h)),
        out_shape=jax.ShapeDtypeStruct((n, heads * dv), BF16),
        scratch_shapes=[pltpu.VMEM((dk, dv), F32), pltpu.VMEM((seq, dv), F32)],
        compiler_params=_params("arbitrary", "arbitrary", "arbitrary"),
        name="gla",
    )(q, k, v, lr, r, wgf, bgf, wgb, bgb, gain)


def _proj_residual_kernel(a_ref, w_ref, b_ref, h_ref, g_ref, h_out_ref, hn_out_ref):
    h = h_ref[...] + (jnp.dot(a_ref[...], w_ref[...], preferred_element_type=F32) + b_ref[...])
    h_out_ref[...] = h
    hn_out_ref[...] = _rms(h, g_ref[...]).astype(hn_out_ref.dtype)


def _proj_residual_router_kernel(a_ref, w_ref, b_ref, h_ref, g_ref, rhi_ref, rlo_ref,
                                 h_out_ref, hn_out_ref, meta_ref, count_ref, *, n_experts):
    i = pl.program_id(0)
    h = h_ref[...] + (jnp.dot(a_ref[...], w_ref[...], preferred_element_type=F32) + b_ref[...])
    h_out_ref[...] = h
    hn = _rms(h, g_ref[...])
    hn_out_ref[...] = hn.astype(hn_out_ref.dtype)

    hn_hi = hn.astype(BF16)
    hn_lo = (hn - hn_hi.astype(F32)).astype(BF16)
    logits = (jnp.dot(hn_hi, rhi_ref[...], preferred_element_type=F32)
              + jnp.dot(hn_lo, rhi_ref[...], preferred_element_type=F32)
              + jnp.dot(hn_hi, rlo_ref[...], preferred_element_type=F32))
    tm = logits.shape[0]
    lane = lax.broadcasted_iota(jnp.int32, logits.shape, 1).astype(F32)
    neg = jnp.float32(-jnp.inf)
    logits = jnp.where(lane < n_experts, logits, neg)
    m1 = jnp.max(logits, axis=-1, keepdims=True)
    i1 = jnp.min(jnp.where(logits == m1, lane, float(LANES)), axis=-1, keepdims=True)
    rest = jnp.where(lane == i1, neg, logits)
    m2 = jnp.max(rest, axis=-1, keepdims=True)
    i2 = jnp.min(jnp.where(rest == m2, lane, float(LANES)), axis=-1, keepdims=True)
    e2 = jnp.exp(m2 - m1)
    w1 = 1.0 / (1.0 + e2)
    w2 = e2 / (1.0 + e2)

    @pl.when(i == 0)
    def _():
        count_ref[...] = jnp.zeros_like(count_ref)

    sel = (lane == i1) | (lane == i2)
    onehot = jnp.where(sel, 1.0, 0.0)
    ri = lax.broadcasted_iota(jnp.int32, (tm, tm), 0)
    ci = lax.broadcasted_iota(jnp.int32, (tm, tm), 1)
    strict_lower = jnp.where(ci < ri, 1.0, 0.0).astype(BF16)
    rank = jnp.dot(strict_lower, onehot.astype(BF16), preferred_element_type=F32) + count_ref[...]
    r1 = jnp.sum(jnp.where(lane == i1, rank, 0.0), axis=-1, keepdims=True)
    r2 = jnp.sum(jnp.where(lane == i2, rank, 0.0), axis=-1, keepdims=True)
    count_ref[...] = count_ref[...] + jnp.sum(onehot, axis=0, keepdims=True)

    meta = jnp.where(lane == 0, i1, 0.0)
    meta = jnp.where(lane == 1, i2, meta)
    meta = jnp.where(lane == 2, w1, meta)
    meta = jnp.where(lane == 3, w2, meta)
    meta = jnp.where(lane == 4, r1, meta)
    meta = jnp.where(lane == 5, r2, meta)
    meta_ref[...] = meta


def _proj_residual(a, w, bias, h, gain, hn_dtype, name, router=None):
    n, kdim = a.shape
    d = w.shape[1]
    tm = ROW_TILE
    assert n % tm == 0
    in_specs = [
        pl.BlockSpec((tm, kdim), lambda i: (i, 0)),
        pl.BlockSpec((kdim, d), lambda i: (0, 0)),
        pl.BlockSpec((1, d), lambda i: (0, 0)),
        pl.BlockSpec((tm, d), lambda i: (i, 0)),
        pl.BlockSpec((1, d), lambda i: (0, 0)),
    ]
    out_specs = [pl.BlockSpec((tm, d), lambda i: (i, 0)), pl.BlockSpec((tm, d), lambda i: (i, 0))]
    out_shape = [jax.ShapeDtypeStruct((n, d), F32), jax.ShapeDtypeStruct((n, d), hn_dtype)]
    args = [a, w, bias.reshape(1, d), h, gain.reshape(1, d)]
    if router is None:
        body = _proj_residual_kernel
    else:
        n_experts = router.shape[1]
        rpad = jnp.zeros((d, LANES), F32).at[:, :n_experts].set(router)
        rhi = rpad.astype(BF16)
        rlo = (rpad - rhi.astype(F32)).astype(BF16)
        args += [rhi, rlo]
        in_specs += [pl.BlockSpec((d, LANES), lambda i: (0, 0))] * 2
        out_specs += [pl.BlockSpec((tm, LANES), lambda i: (i, 0)),
                      pl.BlockSpec((1, LANES), lambda i: (0, 0))]
        out_shape += [jax.ShapeDtypeStruct((n, LANES), F32), jax.ShapeDtypeStruct((1, LANES), F32)]
        body = functools.partial(_proj_residual_router_kernel, n_experts=n_experts)
    return pl.pallas_call(
        body,
        grid=(n // tm,),
        in_specs=in_specs,
        out_specs=out_specs,
        out_shape=out_shape,
        compiler_params=_params("arbitrary"),
        name=name,
    )(*args)


def _swiglu_kernel(te_ref, nu_ref, x_ref, wg_ref, wu_ref, wd_ref, *rest, residual):
    if residual:
        h_ref, o_ref, acc_ref = rest
    else:
        o_ref, acc_ref = rest
    i = pl.program_id(0)
    j = pl.program_id(1)
    used = i < nu_ref[0]

    @pl.when(used & (j == 0))
    def _():
        acc_ref[...] = jnp.zeros_like(acc_ref)

    @pl.when(used)
    def _():
        x = x_ref[...].astype(BF16)
        g = jnp.dot(x, wg_ref[...], preferred_element_type=F32)
        u = jnp.dot(x, wu_ref[...], preferred_element_type=F32)
        a = (_silu(g) * u).astype(BF16)
        acc_ref[...] += jnp.dot(a, wd_ref[...], preferred_element_type=F32)

    @pl.when(j == pl.num_programs(1) - 1)
    def _():
        @pl.when(used)
        def _():
            if residual:
                o_ref[...] = h_ref[...] + acc_ref[...]
            else:
                o_ref[...] = acc_ref[...].astype(o_ref.dtype)

        @pl.when(jnp.logical_not(used))
        def _():
            o_ref[...] = jnp.zeros_like(o_ref)


def _swiglu(x, wg, wu, wd, tile_expert, n_used, tm, out_dtype, name, residual=None):
    rows, d = x.shape
    f = wg.shape[2]
    tf = FFN_F_TILE
    assert rows % tm == 0 and f % tf == 0
    nf = f // tf

    def fcol(i, j, te, nu):
        return jnp.where(i < nu[0], j, nf - 1)

    in_specs = [
        pl.BlockSpec((tm, d), lambda i, j, te, nu: (i, 0)),
        pl.BlockSpec((None, d, tf), lambda i, j, te, nu: (te[i], 0, fcol(i, j, te, nu))),
        pl.BlockSpec((None, d, tf), lambda i, j, te, nu: (te[i], 0, fcol(i, j, te, nu))),
        pl.BlockSpec((None, tf, d), lambda i, j, te, nu: (te[i], fcol(i, j, te, nu), 0)),
    ]
    args = [x, wg, wu, wd]
    if residual is not None:
        in_specs.append(pl.BlockSpec((tm, d), lambda i, j, te, nu: (i, 0)))
        args.append(residual)
    return pl.pallas_call(
        functools.partial(_swiglu_kernel, residual=residual is not None),
        grid_spec=pltpu.PrefetchScalarGridSpec(
            num_scalar_prefetch=2,
            grid=(rows // tm, nf),
            in_specs=in_specs,
            out_specs=pl.BlockSpec((tm, d), lambda i, j, te, nu: (i, 0)),
            scratch_shapes=[pltpu.VMEM((tm, d), F32)],
        ),
        out_shape=jax.ShapeDtypeStruct((rows, d), out_dtype),
        compiler_params=_params("arbitrary", "arbitrary"),
        name=name,
    )(tile_expert, n_used, *args)


def _alibi_slope(head):
    return float(2.0 ** (-8.0 * (head + 1) / N_Q_HEADS))


def _swa_kernel(sink_ref, q_ref, kp_ref, kc_ref, kn_ref, vp_ref, vc_ref, vn_ref, o_ref, *, nblk):
    n = pl.program_id(1)
    t = q_ref.shape[0]
    half = HEAD_DIM
    qi = lax.broadcasted_iota(jnp.int32, (t, 3 * t), 0)
    kj = lax.broadcasted_iota(jnp.int32, (t, 3 * t), 1)
    dist = jnp.abs(qi + t - kj)
    valid = dist <= WINDOW
    valid = valid & ((kj >= t) | (n > 0)) & ((kj < 2 * t) | (n < nblk - 1))
    neg_dist = jnp.where(valid, -dist.astype(F32), -jnp.inf)
    lane = lax.broadcasted_iota(jnp.int32, (t, LANES), 1)
    low = lane < half

    for kvh in range(N_KV_HEADS):
        ks = slice(kvh * LANES, (kvh + 1) * LANES)
        kw = jnp.concatenate([kp_ref[:, ks], kc_ref[:, ks], kn_ref[:, ks]], axis=0)
        vw = jnp.concatenate([vp_ref[:, ks], vc_ref[:, ks], vn_ref[:, ks]], axis=0)
        for pair in range(GROUP // 2):
            col = (kvh * GROUP + 2 * pair) * half
            q2 = q_ref[:, col:col + LANES]
            outs = []
            for sub in range(2):
                head = kvh * GROUP + 2 * pair + sub
                qm = jnp.where(low if sub == 0 else jnp.logical_not(low), q2, jnp.zeros_like(q2))
                s = lax.dot_general(qm, kw, (((1,), (1,)), ((), ())), preferred_element_type=F32)
                s = s + _alibi_slope(head) * neg_dist
                sink = sink_ref[head]
                m = jnp.maximum(jnp.max(s, axis=-1, keepdims=True), sink)
                p = jnp.exp(s - m)
                denom = jnp.sum(p, axis=-1, keepdims=True) + jnp.exp(sink - m)
                p = (p * (1.0 / denom)).astype(BF16)
                outs.append(jnp.dot(p, vw, preferred_element_type=F32))
            o_ref[:, col:col + LANES] = jnp.where(low, outs[0], outs[1]).astype(o_ref.dtype)


def _swa(q, kdup, vdup, sinks, batch, seq):
    n, qw = q.shape
    t = ATT_BLOCK
    nblk = seq // t
    kvw = kdup.shape[1]
    prev = lambda b, i, s: (b * nblk + jnp.maximum(i - 1, 0), 0)
    cur = lambda b, i, s: (b * nblk + i, 0)
    nxt = lambda b, i, s: (b * nblk + jnp.minimum(i + 1, nblk - 1), 0)
    return pl.pallas_call(
        functools.partial(_swa_kernel, nblk=nblk),
        grid_spec=pltpu.PrefetchScalarGridSpec(
            num_scalar_prefetch=1,
            grid=(batch, nblk),
            in_specs=[
                pl.BlockSpec((t, qw), cur),
                pl.BlockSpec((t, kvw), prev), pl.BlockSpec((t, kvw), cur), pl.BlockSpec((t, kvw), nxt),
                pl.BlockSpec((t, kvw), prev), pl.BlockSpec((t, kvw), cur), pl.BlockSpec((t, kvw), nxt),
            ],
            out_specs=pl.BlockSpec((t, qw), cur),
        ),
        out_shape=jax.ShapeDtypeStruct((n, qw), BF16),
        compiler_params=_params("arbitrary", "arbitrary"),
        name="swa",
    )(sinks, q, kdup, kdup, kdup, vdup, vdup, vdup)


def _scatter_kernel(pos_ref, x_ref, xs_in_ref, xs_ref, sem):
    del xs_in_ref
    tm = x_ref.shape[0]

    def row_copy(t, s):
        dst = pos_ref[TOP_K * t + s]
        return pltpu.make_async_copy(x_ref.at[pl.ds(t, 1)], xs_ref.at[pl.ds(dst, 1)], sem)

    def start(t, carry):
        for s in range(TOP_K):
            row_copy(t, s).start()
        return carry

    def wait(t, carry):
        for s in range(TOP_K):
            row_copy(t, s).wait()
        return carry

    lax.fori_loop(0, tm, start, 0, unroll=8)
    lax.fori_loop(0, tm, wait, 0, unroll=8)


def _scatter_rows(x, pos_flat, total_rows):
    n, d = x.shape
    tm = ROW_TILE
    xs0 = jnp.zeros((total_rows, d), x.dtype)
    return pl.pallas_call(
        _scatter_kernel,
        grid=(n // tm,),
        in_specs=[
            pl.BlockSpec((TOP_K * tm,), lambda i: (i,), memory_space=pltpu.SMEM),
            pl.BlockSpec((tm, d), lambda i: (i, 0)),
            pl.BlockSpec(memory_space=pl.ANY),
        ],
        out_specs=pl.BlockSpec(memory_space=pl.ANY),
        out_shape=jax.ShapeDtypeStruct((total_rows, d), x.dtype),
        scratch_shapes=[pltpu.SemaphoreType.DMA],
        input_output_aliases={2: 0},
        compiler_params=_params("arbitrary"),
        name="moe_scatter",
    )(pos_flat, x, xs0)


def _combine_kernel(pos_ref, h_ref, meta_ref, g_ref, ys_ref, o_ref, y_ref, sem):
    tm = h_ref.shape[0]

    def row_copy(t, s):
        src = pos_ref[TOP_K * t + s]
        return pltpu.make_async_copy(ys_ref.at[pl.ds(src, 1)], y_ref.at[s, pl.ds(t, 1)], sem)

    def start(t, carry):
        for s in range(TOP_K):
            row_copy(t, s).start()
        return carry

    def wait(t, carry):
        for s in range(TOP_K):
            row_copy(t, s).wait()
        return carry

    lax.fori_loop(0, tm, start, 0, unroll=8)
    lax.fori_loop(0, tm, wait, 0, unroll=8)
    meta = meta_ref[...]
    w1 = meta[:, 2:3]
    w2 = meta[:, 3:4]
    moe = w1 * y_ref[0].astype(F32) + w2 * y_ref[1].astype(F32)
    o_ref[...] = _rms(h_ref[...] + moe, g_ref[...])


def _combine(h, meta, gain, ys, pos_flat):
    n, d = h.shape
    tm = ROW_TILE
    return pl.pallas_call(
        _combine_kernel,
        grid=(n // tm,),
        in_specs=[
            pl.BlockSpec((TOP_K * tm,), lambda i: (i,), memory_space=pltpu.SMEM),
            pl.BlockSpec((tm, d), lambda i: (i, 0)),
            pl.BlockSpec((tm, LANES), lambda i: (i, 0)),
            pl.BlockSpec((1, d), lambda i: (0, 0)),
            pl.BlockSpec(memory_space=pl.ANY),
        ],
        out_specs=pl.BlockSpec((tm, d), lambda i: (i, 0)),
        out_shape=jax.ShapeDtypeStruct((n, d), F32),
        scratch_shapes=[pltpu.VMEM((TOP_K, tm, d), ys.dtype), pltpu.SemaphoreType.DMA],
        compiler_params=_params("arbitrary"),
        name="moe_combine",
    )(pos_flat, h, meta, gain.reshape(1, d), ys)


def kernel(x, mix_norm, ffn_norm, gla_in_proj, gla_gate_w_fwd, gla_gate_b_fwd, gla_gate_w_bwd, gla_gate_b_bwd, gla_head_norm, gla_out_proj, swa_qkv_proj, swa_qkv_bias, swa_sinks, swa_out_proj, swa_out_bias, dense_w_gate, dense_w_up, dense_w_down, moe_router, moe_w_gate, moe_w_up, moe_w_down, final_norm):
    batch, seq, d = x.shape
    n = batch * seq
    h0 = x.reshape(n, d)
    zeros_d = jnp.zeros((d,), F32)

    key_w = gla_gate_w_fwd.shape[2]
    val_w = gla_head_norm.shape[1]
    rank = GLA_GATE_RANK
    in_w = gla_in_proj.shape[2]
    splits = ((0, key_w), (key_w, key_w), (2 * key_w, val_w), (2 * key_w + val_w, val_w),
              (2 * key_w + 2 * val_w, 2 * rank))
    q, k, v, r, lr = _norm_proj(h0, mix_norm[0], gla_in_proj[0].astype(BF16), jnp.zeros((in_w,), F32),
                                splits, (BF16, BF16, BF16, BF16, F32), "gla_in_proj")
    zero_gate = jnp.zeros((rank, key_w), F32)
    wgf = jnp.concatenate([gla_gate_w_fwd[0], zero_gate], axis=0).astype(BF16)
    wgb = jnp.concatenate([zero_gate, gla_gate_w_bwd[0]], axis=0).astype(BF16)
    og = _gla(q, k, v, lr, r, wgf, gla_gate_b_fwd[0].reshape(1, key_w), wgb,
              gla_gate_b_bwd[0].reshape(1, key_w), gla_head_norm[0].reshape(1, val_w), batch, seq)
    h1, hn1 = _proj_residual(og, gla_out_proj[0].astype(BF16), zeros_d, h0, ffn_norm[0], BF16,
                             "gla_out_proj")

    n_ffn_tiles = n // FFN_TILE
    h2 = _swiglu(hn1, dense_w_gate.astype(BF16), dense_w_up.astype(BF16), dense_w_down.astype(BF16),
                 jnp.zeros((n_ffn_tiles,), jnp.int32), jnp.full((1,), n_ffn_tiles, jnp.int32),
                 FFN_TILE, F32, "dense_swiglu", residual=h1)

    qw = N_Q_HEADS * HEAD_DIM
    kvw = N_KV_HEADS * HEAD_DIM
    scale = HEAD_DIM ** -0.5
    wqkv, bqkv = swa_qkv_proj[0], swa_qkv_bias[0]

    def dup(m):
        lead = m.shape[:-1]
        m = m.reshape(lead + (N_KV_HEADS, 1, HEAD_DIM))
        return jnp.broadcast_to(m, lead + (N_KV_HEADS, 2, HEAD_DIM)).reshape(lead + (2 * kvw,))

    w_aug = jnp.concatenate([wqkv[:, :qw] * scale, dup(wqkv[:, qw:qw + kvw]), dup(wqkv[:, qw + kvw:])], axis=1)
    b_aug = jnp.concatenate([bqkv[:qw] * scale, dup(bqkv[qw:qw + kvw]), dup(bqkv[qw + kvw:])], axis=0)
    splits = ((0, qw), (qw, 2 * kvw), (qw + 2 * kvw, 2 * kvw))
    aq, ak, av = _norm_proj(h2, mix_norm[1], w_aug.astype(BF16), b_aug, splits, (BF16, BF16, BF16),
                            "swa_qkv_proj")
    oa = _swa(aq, ak, av, swa_sinks[0], batch, seq)

    n_experts = moe_router.shape[2]
    h3, hn3, meta, counts = _proj_residual(oa, swa_out_proj[0].astype(BF16), swa_out_bias[0], h2,
                                           ffn_norm[1], F32, "swa_out_proj", router=moe_router[0])
    tm = MOE_TILE
    counts = counts[0, :n_experts].astype(jnp.int32)
    tiles_per_expert = (counts + tm - 1) // tm
    tile_end = jnp.cumsum(tiles_per_expert)
    offsets = (tile_end - tiles_per_expert) * tm
    n_tiles = (TOP_K * n) // tm + n_experts
    tile_expert = jnp.minimum(jnp.searchsorted(tile_end, jnp.arange(n_tiles), side="right"),
                              n_experts - 1).astype(jnp.int32)
    n_used = tile_end[-1:].astype(jnp.int32)
    eidx = meta[:, 0:TOP_K].astype(jnp.int32)
    pos = offsets[eidx] + meta[:, 4:4 + TOP_K].astype(jnp.int32)
    pos_flat = pos.reshape(-1)

    xs = _scatter_rows(hn3, pos_flat, n_tiles * tm)
    ys = _swiglu(xs, moe_w_gate[0].astype(BF16), moe_w_up[0].astype(BF16), moe_w_down[0].astype(BF16),
                 tile_expert, n_used, tm, F32, "moe_swiglu")
    out = _combine(h3, meta, final_norm, ys, pos_flat)
    return out.reshape(batch, seq, d)
```

```python
import functools

import jax
import jax.numpy as jnp
from jax import lax
from jax.experimental import pallas as pl
from jax.experimental.pallas import tpu as pltpu

F32 = jnp.float32
BF16 = jnp.bfloat16

NORM_EPS = 1e-5

GLA_HEADS = 4
GLA_GATE_RANK = 16
GLA_GATE_TAU = 16.0
GLA_CHUNK = 64
N_Q_HEADS = 16
N_KV_HEADS = 4
HEAD_DIM = 64
GROUP = N_Q_HEADS // N_KV_HEADS
WINDOW = 128
ATT_BLOCK = 128
TOP_K = 2

LANES = 128
VMEM_LIMIT_BYTES = 56 * 2**20

ROW_TILE = 512
GLA_BLOCK = 512
GLA_GROUP = 256
FFN_TILE = 512
FFN_F_TILE = 256
MOE_TILE = 512
SWA_LOOKAHEAD = 4


def _params(*sem):
    return pltpu.CompilerParams(dimension_semantics=sem, vmem_limit_bytes=VMEM_LIMIT_BYTES)


def _rms(x, gain):
    y = x * lax.rsqrt(jnp.mean(x * x, axis=-1, keepdims=True) + NORM_EPS)
    return y * gain


def _silu(x):
    return x * (1.0 / (1.0 + jnp.exp(-x)))


def _norm_proj_kernel(x_ref, g_ref, w_ref, b_ref, *o_refs, splits):
    y = _rms(x_ref[...], g_ref[...]).astype(BF16)
    for (start, width), o_ref in zip(splits, o_refs):
        acc = jnp.dot(y, w_ref[:, start:start + width], preferred_element_type=F32)
        acc = acc + b_ref[:, start:start + width]
        o_ref[...] = acc.astype(o_ref.dtype)


def _norm_proj(x, gain, w, bias, splits, dtypes, name):
    n, d = x.shape
    nout = w.shape[1]
    tm = ROW_TILE
    assert n % tm == 0
    return pl.pallas_call(
        functools.partial(_norm_proj_kernel, splits=splits),
        grid=(n // tm,),
        in_specs=[
            pl.BlockSpec((tm, d), lambda i: (i, 0)),
            pl.BlockSpec((1, d), lambda i: (0, 0)),
            pl.BlockSpec((d, nout), lambda i: (0, 0)),
            pl.BlockSpec((1, nout), lambda i: (0, 0)),
        ],
        out_specs=[pl.BlockSpec((tm, wd), lambda i: (i, 0)) for (_, wd) in splits],
        out_shape=[jax.ShapeDtypeStruct((n, wd), dt) for (_, wd), dt in zip(splits, dtypes)],
        compiler_params=_params("arbitrary"),
        name=name,
    )(x, gain.reshape(1, d), w, bias.reshape(1, nout))


def _gla_direction(q_ref, k_ref, v_ref, lr_ref, wg_ref, bg_ref, state_ref, reverse):
    rows, dk = q_ref.shape
    c = GLA_CHUNK
    nc = rows // c
    q = q_ref[...].astype(F32) * (dk ** -0.5)
    k = k_ref[...].astype(F32)
    v = v_ref[...]
    z = jnp.dot(lr_ref[...].astype(BF16), wg_ref[...], preferred_element_type=F32) + bg_ref[...]
    la = (jnp.minimum(z, 0.0) - jnp.log(1.0 + jnp.exp(-jnp.abs(z)))) * (1.0 / GLA_GATE_TAU)
    la_hi = la.astype(BF16)
    la_lo = (la - la_hi.astype(F32)).astype(BF16)

    tn = (((0,), (0,)), ((), ()))
    nt = (((1,), (1,)), ((), ()))
    grp = GLA_GROUP
    ng = rows // grp
    ri = lax.broadcasted_iota(jnp.int32, (grp, grp), 0)
    ci = lax.broadcasted_iota(jnp.int32, (grp, grp), 1)
    same_chunk = (ri // c) == (ci // c)
    if reverse:
        cum_mask = same_chunk & (ci >= ri)
        att_mask = same_chunk & (ci > ri)
        ref_row, last_row = c // 2 - 1, 0
    else:
        cum_mask = same_chunk & (ci <= ri)
        att_mask = cum_mask
        ref_row, last_row = c // 2, c - 1
    cum = jnp.where(cum_mask, 1.0, 0.0).astype(BF16)

    la_hl = jnp.concatenate([la_hi, la_lo], axis=1)
    b = jnp.concatenate(
        [jnp.dot(cum, la_hl[g * grp:(g + 1) * grp], preferred_element_type=F32) for g in range(ng)],
        axis=0)
    b = b[:, :dk] + b[:, dk:]
    b3 = b.reshape(nc, c, dk)
    b_ref = b3[:, ref_row:ref_row + 1, :]
    b_last = b3[:, last_row:last_row + 1, :]
    q3 = q.reshape(nc, c, dk)
    k3 = k.reshape(nc, c, dk)
    qe = (q3 * jnp.exp(b3 - b_ref)).astype(BF16).reshape(rows, dk)
    ke = (k3 * jnp.exp(b_ref - b3)).astype(BF16).reshape(rows, dk)
    kd = (k3 * jnp.exp(b_last - b3)).astype(BF16).reshape(rows, dk)
    qb = (q3 * jnp.exp(b3)).astype(BF16).reshape(rows, dk)

    o_intra = []
    for g in range(ng):
        sl = slice(g * grp, (g + 1) * grp)
        s = lax.dot_general(qe[sl], ke[sl], nt, preferred_element_type=F32)
        s = jnp.where(att_mask, s, 0.0).astype(BF16)
        o_intra.append(jnp.dot(s, v[sl], preferred_element_type=F32))
    o_intra = jnp.concatenate(o_intra, axis=0)

    upd = [lax.dot_general(kd[j * c:(j + 1) * c], v[j * c:(j + 1) * c], tn, preferred_element_type=F32)
           for j in range(nc)]
    decay_rows = jnp.exp(b_last.reshape(nc, dk))
    decay_rows = jnp.concatenate([decay_rows, jnp.zeros((dk - nc, dk), F32)], axis=0)
    decay_cols = decay_rows.T

    state = state_ref[...]
    o_inter = [None] * nc
    for j in (range(nc - 1, -1, -1) if reverse else range(nc)):
        o_inter[j] = jnp.dot(qb[j * c:(j + 1) * c], state.astype(BF16), preferred_element_type=F32)
        state = state * decay_cols[:, j:j + 1] + upd[j]
    state_ref[...] = state
    return o_intra + jnp.concatenate(o_inter, axis=0)


def _gla_kernel(q_ref, k_ref, v_ref, lr_ref, r_ref, wgf_ref, bgf_ref, wgb_ref, bgb_ref, gain_ref,
                o_ref, state_ref, oacc_ref, *, nblk):
    i = pl.program_id(2)
    rows = q_ref.shape[0]

    @pl.when((i == 0) | (i == nblk))
    def _():
        state_ref[...] = jnp.zeros_like(state_ref)

    @pl.when(i < nblk)
    def _():
        o = _gla_direction(q_ref, k_ref, v_ref, lr_ref, wgf_ref, bgf_ref, state_ref, False)
        oacc_ref[pl.ds(pl.multiple_of(i * rows, rows), rows), :] = o

    @pl.when(i >= nblk)
    def _():
        j = 2 * nblk - 1 - i
        o = _gla_direction(q_ref, k_ref, v_ref, lr_ref, wgb_ref, bgb_ref, state_ref, True)
        o = o + oacc_ref[pl.ds(pl.multiple_of(j * rows, rows), rows), :]
        o = _rms(o, gain_ref[...])
        o_ref[...] = (o * _silu(r_ref[...].astype(F32))).astype(o_ref.dtype)


def _gla(q, k, v, lr, r, wgf, bgf, wgb, bgb, gain, batch, seq):
    n = q.shape[0]
    heads = GLA_HEADS
    dk = q.shape[1] // heads
    dv = v.shape[1] // heads
    rows = GLA_BLOCK
    nblk = seq // rows
    assert seq % rows == 0 and rows % GLA_CHUNK == 0

    def blk(i):
        return jnp.where(i < nblk, i, 2 * nblk - 1 - i)

    def late_blk(i):
        return jnp.where(i < nblk, nblk - 1, 2 * nblk - 1 - i)

    row_map = lambda b, h, i: (b * nblk + blk(i), h)
    return pl.pallas_call(
        functools.partial(_gla_kernel, nblk=nblk),
        grid=(batch, heads, 2 * nblk),
        in_specs=[
            pl.BlockSpec((rows, dk), row_map),
            pl.BlockSpec((rows, dk), row_map),
            pl.BlockSpec((rows, dv), row_map),
            pl.BlockSpec((rows, lr.shape[1]), lambda b, h, i: (b * nblk + blk(i), 0)),
            pl.BlockSpec((rows, dv), lambda b, h, i: (b * nblk + late_blk(i), h)),
            pl.BlockSpec((wgf.shape[0], dk), lambda b, h, i: (0, h)),
            pl.BlockSpec((1, dk), lambda b, h, i: (0, h)),
            pl.BlockSpec((wgb.shape[0], dk), lambda b, h, i: (0, h)),
            pl.BlockSpec((1, dk), lambda b, h, i: (0, h)),
            pl.BlockSpec((1, dv), lambda b, h, i: (0, h)),
        ],
        out_specs=pl.BlockSpec((rows, dv), lambda b, h, i: (b * nblk + late_blk(i), h)),
        out_shape=jax.ShapeDtypeStruct((n, heads * dv), BF16),
        scratch_shapes=[pltpu.VMEM((dk, dv), F32), pltpu.VMEM((seq, dv), F32)],
        compiler_params=_params("arbitrary", "arbitrary", "arbitrary"),
        name="gla",
    )(q, k, v, lr, r, wgf, bgf, wgb, bgb, gain)


def _proj_residual_kernel(a_ref, w_ref, b_ref, h_ref, g_ref, h_out_ref, hn_out_ref):
    h = h_ref[...] + (jnp.dot(a_ref[...], w_ref[...], preferred_element_type=F32) + b_ref[...])
    h_out_ref[...] = h
    hn_out_ref[...] = _rms(h, g_ref[...]).astype(hn_out_ref.dtype)


def _proj_residual_router_kernel(a_ref, w_ref, b_ref, h_ref, g_ref, rhi_ref, rlo_ref,
                                 h_out_ref, hn_out_ref, meta_ref, count_ref, *, n_experts):
    i = pl.program_id(0)
    h = h_ref[...] + (jnp.dot(a_ref[...], w_ref[...], preferred_element_type=F32) + b_ref[...])
    h_out_ref[...] = h
    hn = _rms(h, g_ref[...])
    hn_out_ref[...] = hn.astype(hn_out_ref.dtype)

    hn_hi = hn.astype(BF16)
    hn_lo = (hn - hn_hi.astype(F32)).astype(BF16)
    logits = (jnp.dot(hn_hi, rhi_ref[...], preferred_element_type=F32)
              + jnp.dot(hn_lo, rhi_ref[...], preferred_element_type=F32)
              + jnp.dot(hn_hi, rlo_ref[...], preferred_element_type=F32))
    tm = logits.shape[0]
    lane = lax.broadcasted_iota(jnp.int32, logits.shape, 1).astype(F32)
    neg = jnp.float32(-jnp.inf)
    logits = jnp.where(lane < n_experts, logits, neg)
    m1 = jnp.max(logits, axis=-1, keepdims=True)
    i1 = jnp.min(jnp.where(logits == m1, lane, float(LANES)), axis=-1, keepdims=True)
    rest = jnp.where(lane == i1, neg, logits)
    m2 = jnp.max(rest, axis=-1, keepdims=True)
    i2 = jnp.min(jnp.where(rest == m2, lane, float(LANES)), axis=-1, keepdims=True)
    e2 = jnp.exp(m2 - m1)
    w1 = 1.0 / (1.0 + e2)
    w2 = e2 / (1.0 + e2)

    @pl.when(i == 0)
    def _():
        count_ref[...] = jnp.zeros_like(count_ref)

    sel = (lane == i1) | (lane == i2)
    onehot = jnp.where(sel, 1.0, 0.0)
    ri = lax.broadcasted_iota(jnp.int32, (tm, tm), 0)
    ci = lax.broadcasted_iota(jnp.int32, (tm, tm), 1)
    strict_lower = jnp.where(ci < ri, 1.0, 0.0).astype(BF16)
    rank = jnp.dot(strict_lower, onehot.astype(BF16), preferred_element_type=F32) + count_ref[...]
    r1 = jnp.sum(jnp.where(lane == i1, rank, 0.0), axis=-1, keepdims=True)
    r2 = jnp.sum(jnp.where(lane == i2, rank, 0.0), axis=-1, keepdims=True)
    count_ref[...] = count_ref[...] + jnp.sum(onehot, axis=0, keepdims=True)

    meta = jnp.where(lane == 0, i1, 0.0)
    meta = jnp.where(lane == 1, i2, meta)
    meta = jnp.where(lane == 2, w1, meta)
    meta = jnp.where(lane == 3, w2, meta)
    meta = jnp.where(lane == 4, r1, meta)
    meta = jnp.where(lane == 5, r2, meta)
    meta_ref[...] = meta


def _proj_residual(a, w, bias, h, gain, hn_dtype, name, router=None):
    n, kdim = a.shape
    d = w.shape[1]
    tm = ROW_TILE
    assert n % tm == 0
    in_specs = [
        pl.BlockSpec((tm, kdim), lambda i: (i, 0)),
        pl.BlockSpec((kdim, d), lambda i: (0, 0)),
        pl.BlockSpec((1, d), lambda i: (0, 0)),
        pl.BlockSpec((tm, d), lambda i: (i, 0)),
        pl.BlockSpec((1, d), lambda i: (0, 0)),
    ]
    out_specs = [pl.BlockSpec((tm, d), lambda i: (i, 0)), pl.BlockSpec((tm, d), lambda i: (i, 0))]
    out_shape = [jax.ShapeDtypeStruct((n, d), F32), jax.ShapeDtypeStruct((n, d), hn_dtype)]
    args = [a, w, bias.reshape(1, d), h, gain.reshape(1, d)]
    if router is None:
        body = _proj_residual_kernel
    else:
        n_experts = router.shape[1]
        rpad = jnp.zeros((d, LANES), F32).at[:, :n_experts].set(router)
        rhi = rpad.astype(BF16)
        rlo = (rpad - rhi.astype(F32)).astype(BF16)
        args += [rhi, rlo]
        in_specs += [pl.BlockSpec((d, LANES), lambda i: (0, 0))] * 2
        out_specs += [pl.BlockSpec((tm, LANES), lambda i: (i, 0)),
                      pl.BlockSpec((1, LANES), lambda i: (0, 0))]
        out_shape += [jax.ShapeDtypeStruct((n, LANES), F32), jax.ShapeDtypeStruct((1, LANES), F32)]
        body = functools.partial(_proj_residual_router_kernel, n_experts=n_experts)
    return pl.pallas_call(
        body,
        grid=(n // tm,),
        in_specs=in_specs,
        out_specs=out_specs,
        out_shape=out_shape,
        compiler_params=_params("arbitrary"),
        name=name,
    )(*args)


def _swiglu_kernel(te_ref, nu_ref, x_ref, wg_ref, wu_ref, wd_ref, *rest, residual):
    if residual:
        h_ref, o_ref, acc_ref = rest
    else:
        o_ref, acc_ref = rest
    i = pl.program_id(0)
    j = pl.program_id(1)
    used = i < nu_ref[0]

    @pl.when(used & (j == 0))
    def _():
        acc_ref[...] = jnp.zeros_like(acc_ref)

    @pl.when(used)
    def _():
        x = x_ref[...].astype(BF16)
        g = jnp.dot(x, wg_ref[...], preferred_element_type=F32)
        u = jnp.dot(x, wu_ref[...], preferred_element_type=F32)
        a = (_silu(g) * u).astype(BF16)
        acc_ref[...] += jnp.dot(a, wd_ref[...], preferred_element_type=F32)

    @pl.when(j == pl.num_programs(1) - 1)
    def _():
        @pl.when(used)
        def _():
            if residual:
                o_ref[...] = h_ref[...] + acc_ref[...]
            else:
                o_ref[...] = acc_ref[...].astype(o_ref.dtype)

        @pl.when(jnp.logical_not(used))
        def _():
            o_ref[...] = jnp.zeros_like(o_ref)


def _swiglu(x, wg, wu, wd, tile_expert, n_used, tm, out_dtype, name, residual=None):
    rows, d = x.shape
    f = wg.shape[2]
    tf = FFN_F_TILE
    assert rows % tm == 0 and f % tf == 0
    nf = f // tf

    def fcol(i, j, te, nu):
        return jnp.where(i < nu[0], j, nf - 1)

    in_specs = [
        pl.BlockSpec((tm, d), lambda i, j, te, nu: (i, 0)),
        pl.BlockSpec((None, d, tf), lambda i, j, te, nu: (te[i], 0, fcol(i, j, te, nu))),
        pl.BlockSpec((None, d, tf), lambda i, j, te, nu: (te[i], 0, fcol(i, j, te, nu))),
        pl.BlockSpec((None, tf, d), lambda i, j, te, nu: (te[i], fcol(i, j, te, nu), 0)),
    ]
    args = [x, wg, wu, wd]
    if residual is not None:
        in_specs.append(pl.BlockSpec((tm, d), lambda i, j, te, nu: (i, 0)))
        args.append(residual)
    return pl.pallas_call(
        functools.partial(_swiglu_kernel, residual=residual is not None),
        grid_spec=pltpu.PrefetchScalarGridSpec(
            num_scalar_prefetch=2,
            grid=(rows // tm, nf),
            in_specs=in_specs,
            out_specs=pl.BlockSpec((tm, d), lambda i, j, te, nu: (i, 0)),
            scratch_shapes=[pltpu.VMEM((tm, d), F32)],
        ),
        out_shape=jax.ShapeDtypeStruct((rows, d), out_dtype),
        compiler_params=_params("arbitrary", "arbitrary"),
        name=name,
    )(tile_expert, n_used, *args)


def _alibi_slope(head):
    return float(2.0 ** (-8.0 * (head + 1) / N_Q_HEADS))


def _swa_bias(t):
    qi = jnp.arange(t)[:, None]
    kj = jnp.arange(3 * t)[None, :]
    dist = jnp.abs(qi + t - kj)
    band = dist <= WINDOW
    valid = jnp.stack([band & (kj >= t), band, band & (kj < 2 * t)])
    slopes = jnp.asarray([_alibi_slope(h) for h in range(N_Q_HEADS)], F32)
    bias = -slopes[None, :, None, None] * dist.astype(F32)[None, None]
    bias = jnp.where(valid[:, None], bias, -jnp.inf)
    return bias.reshape(3, N_KV_HEADS, GROUP * t, 3 * t)


def _swa_kernel(sink_ref, q_ref, kp_ref, kc_ref, kn_ref, vp_ref, vc_ref, vn_ref, bias_ref, o_ref):
    t = q_ref.shape[0]
    lane = lax.broadcasted_iota(jnp.int32, (t, LANES), 1)
    low = lane < HEAD_DIM
    row_head = lax.broadcasted_iota(jnp.int32, (GROUP * t, 1), 0) // t
    ones = jnp.ones((3 * t, LANES), BF16)

    def window(p_ref, c_ref, n_ref, kvh):
        ks = slice(kvh * LANES, (kvh + 1) * LANES)
        return jnp.concatenate([p_ref[:, ks], c_ref[:, ks], n_ref[:, ks]], axis=0)

    blocks = []
    for kvh in range(N_KV_HEADS):
        qs = []
        for g in range(GROUP):
            head = kvh * GROUP + g
            col = (head // 2) * LANES
            q2 = q_ref[:, col:col + LANES]
            qs.append(jnp.where(low if head % 2 == 0 else jnp.logical_not(low), q2,
                                jnp.zeros_like(q2)))
        qg = jnp.concatenate(qs, axis=0)
        kw = window(kp_ref, kc_ref, kn_ref, kvh)
        s = lax.dot_general(qg, kw, (((1,), (1,)), ((), ())), preferred_element_type=F32)
        s = s + bias_ref[kvh]
        sink = jnp.full((GROUP * t, 1), sink_ref[kvh * GROUP], F32)
        for g in range(1, GROUP):
            sink = jnp.where(row_head == g, sink_ref[kvh * GROUP + g], sink)
        m = jnp.maximum(jnp.max(s, axis=-1, keepdims=True), sink)
        p = jnp.exp(s - m).astype(BF16)
        vw = jnp.concatenate([window(vp_ref, vc_ref, vn_ref, kvh), ones], axis=1)
        res = jnp.dot(p, vw, preferred_element_type=F32)
        out = res[:, :LANES] * (1.0 / (res[:, LANES:] + jnp.exp(sink - m)))
        for pair in range(GROUP // 2):
            a = out[(2 * pair) * t:(2 * pair + 1) * t]
            b = out[(2 * pair + 1) * t:(2 * pair + 2) * t]
            blocks.append(jnp.where(low, a, b).astype(o_ref.dtype))
    o_ref[...] = jnp.concatenate(blocks, axis=1)


def _swa(q, kdup, vdup, sinks, batch, seq):
    n, qw = q.shape
    t = ATT_BLOCK
    nblk = seq // t
    assert seq % t == 0 and nblk >= 2
    kvw = kdup.shape[1]
    bias = _swa_bias(t)
    prev = lambda b, i, s: (b * nblk + jnp.maximum(i - 1, 0), 0)
    cur = lambda b, i, s: (b * nblk + i, 0)
    nxt = lambda b, i, s: (b * nblk + jnp.minimum(i + 1, nblk - 1), 0)
    variant = lambda b, i, s: (jnp.where(i == 0, 0, jnp.where(i == nblk - 1, 2, 1)), 0, 0, 0)
    return pl.pallas_call(
        _swa_kernel,
        grid_spec=pltpu.PrefetchScalarGridSpec(
            num_scalar_prefetch=1,
            grid=(batch, nblk),
            in_specs=[
                pl.BlockSpec((t, qw), cur),
                pl.BlockSpec((t, kvw), prev), pl.BlockSpec((t, kvw), cur), pl.BlockSpec((t, kvw), nxt),
                pl.BlockSpec((t, kvw), prev), pl.BlockSpec((t, kvw), cur), pl.BlockSpec((t, kvw), nxt),
                pl.BlockSpec((None,) + bias.shape[1:], variant),
            ],
            out_specs=pl.BlockSpec((t, qw), cur),
        ),
        out_shape=jax.ShapeDtypeStruct((n, qw), BF16),
        compiler_params=_params("arbitrary", "arbitrary"),
        name="swa",
    )(sinks, q, kdup, kdup, kdup, vdup, vdup, vdup, bias)


def _scatter_kernel(pos_ref, x_ref, xs_in_ref, xs_ref, sem):
    del xs_in_ref
    tm = x_ref.shape[0]

    def row_copy(t, s):
        dst = pos_ref[TOP_K * t + s]
        return pltpu.make_async_copy(x_ref.at[pl.ds(t, 1)], xs_ref.at[pl.ds(dst, 1)], sem)

    def start(t, carry):
        for s in range(TOP_K):
            row_copy(t, s).start()
        return carry

    def wait(t, carry):
        for s in range(TOP_K):
            row_copy(t, s).wait()
        return carry

    lax.fori_loop(0, tm, start, 0, unroll=8)
    lax.fori_loop(0, tm, wait, 0, unroll=8)


def _scatter_rows(x, pos_flat, total_rows):
    n, d = x.shape
    tm = ROW_TILE
    xs0 = jnp.zeros((total_rows, d), x.dtype)
    return pl.pallas_call(
        _scatter_kernel,
        grid=(n // tm,),
        in_specs=[
            pl.BlockSpec((TOP_K * tm,), lambda i: (i,), memory_space=pltpu.SMEM),
            pl.BlockSpec((tm, d), lambda i: (i, 0)),
            pl.BlockSpec(memory_space=pl.ANY),
        ],
        out_specs=pl.BlockSpec(memory_space=pl.ANY),
        out_shape=jax.ShapeDtypeStruct((total_rows, d), x.dtype),
        scratch_shapes=[pltpu.SemaphoreType.DMA],
        input_output_aliases={2: 0},
        compiler_params=_params("arbitrary"),
        name="moe_scatter",
    )(pos_flat, x, xs0)


def _combine_kernel(pos_ref, h_ref, meta_ref, g_ref, ys_ref, o_ref, y_ref, sem):
    tm = h_ref.shape[0]

    def row_copy(t, s):
        src = pos_ref[TOP_K * t + s]
        return pltpu.make_async_copy(ys_ref.at[pl.ds(src, 1)], y_ref.at[s, pl.ds(t, 1)], sem)

    def start(t, carry):
        for s in range(TOP_K):
            row_copy(t, s).start()
        return carry

    def wait(t, carry):
        for s in range(TOP_K):
            row_copy(t, s).wait()
        return carry

    lax.fori_loop(0, tm, start, 0, unroll=8)
    lax.fori_loop(0, tm, wait, 0, unroll=8)
    meta = meta_ref[...]
    w1 = meta[:, 2:3]
    w2 = meta[:, 3:4]
    moe = w1 * y_ref[0].astype(F32) + w2 * y_ref[1].astype(F32)
    o_ref[...] = _rms(h_ref[...] + moe, g_ref[...])


def _combine(h, meta, gain, ys, pos_flat):
    n, d = h.shape
    tm = ROW_TILE
    return pl.pallas_call(
        _combine_kernel,
        grid=(n // tm,),
        in_specs=[
            pl.BlockSpec((TOP_K * tm,), lambda i: (i,), memory_space=pltpu.SMEM),
            pl.BlockSpec((tm, d), lambda i: (i, 0)),
            pl.BlockSpec((tm, LANES), lambda i: (i, 0)),
            pl.BlockSpec((1, d), lambda i: (0, 0)),
            pl.BlockSpec(memory_space=pl.ANY),
        ],
        out_specs=pl.BlockSpec((tm, d), lambda i: (i, 0)),
        out_shape=jax.ShapeDtypeStruct((n, d), F32),
        scratch_shapes=[pltpu.VMEM((TOP_K, tm, d), ys.dtype), pltpu.SemaphoreType.DMA],
        compiler_params=_params("arbitrary"),
        name="moe_combine",
    )(pos_flat, h, meta, gain.reshape(1, d), ys)


def kernel(x, mix_norm, ffn_norm, gla_in_proj, gla_gate_w_fwd, gla_gate_b_fwd, gla_gate_w_bwd, gla_gate_b_bwd, gla_head_norm, gla_out_proj, swa_qkv_proj, swa_qkv_bias, swa_sinks, swa_out_proj, swa_out_bias, dense_w_gate, dense_w_up, dense_w_down, moe_router, moe_w_gate, moe_w_up, moe_w_down, final_norm):
    batch, seq, d = x.shape
    n = batch * seq
    h0 = x.reshape(n, d)
    zeros_d = jnp.zeros((d,), F32)

    key_w = gla_gate_w_fwd.shape[2]
    val_w = gla_head_norm.shape[1]
    rank = GLA_GATE_RANK
    in_w = gla_in_proj.shape[2]
    splits = ((0, key_w), (key_w, key_w), (2 * key_w, val_w), (2 * key_w + val_w, val_w),
              (2 * key_w + 2 * val_w, 2 * rank))
    q, k, v, r, lr = _norm_proj(h0, mix_norm[0], gla_in_proj[0].astype(BF16), jnp.zeros((in_w,), F32),
                                splits, (BF16, BF16, BF16, BF16, F32), "gla_in_proj")
    zero_gate = jnp.zeros((rank, key_w), F32)
    wgf = jnp.concatenate([gla_gate_w_fwd[0], zero_gate], axis=0).astype(BF16)
    wgb = jnp.concatenate([zero_gate, gla_gate_w_bwd[0]], axis=0).astype(BF16)
    og = _gla(q, k, v, lr, r, wgf, gla_gate_b_fwd[0].reshape(1, key_w), wgb,
              gla_gate_b_bwd[0].reshape(1, key_w), gla_head_norm[0].reshape(1, val_w), batch, seq)
    h1, hn1 = _proj_residual(og, gla_out_proj[0].astype(BF16), zeros_d, h0, ffn_norm[0], BF16,
                             "gla_out_proj")

    n_ffn_tiles = n // FFN_TILE
    h2 = _swiglu(hn1, dense_w_gate.astype(BF16), dense_w_up.astype(BF16), dense_w_down.astype(BF16),
                 jnp.zeros((n_ffn_tiles,), jnp.int32), jnp.full((1,), n_ffn_tiles, jnp.int32),
                 FFN_TILE, F32, "dense_swiglu", residual=h1)

    qw = N_Q_HEADS * HEAD_DIM
    kvw = N_KV_HEADS * HEAD_DIM
    scale = HEAD_DIM ** -0.5
    wqkv, bqkv = swa_qkv_proj[0], swa_qkv_bias[0]

    def dup(m):
        lead = m.shape[:-1]
        m = m.reshape(lead + (N_KV_HEADS, 1, HEAD_DIM))
        return jnp.broadcast_to(m, lead + (N_KV_HEADS, 2, HEAD_DIM)).reshape(lead + (2 * kvw,))

    w_aug = jnp.concatenate([wqkv[:, :qw] * scale, dup(wqkv[:, qw:qw + kvw]), dup(wqkv[:, qw + kvw:])], axis=1)
    b_aug = jnp.concatenate([bqkv[:qw] * scale, dup(bqkv[qw:qw + kvw]), dup(bqkv[qw + kvw:])], axis=0)
    splits = ((0, qw), (qw, 2 * kvw), (qw + 2 * kvw, 2 * kvw))
    aq, ak, av = _norm_proj(h2, mix_norm[1], w_aug.astype(BF16), b_aug, splits, (BF16, BF16, BF16),
                            "swa_qkv_proj")
    oa = _swa(aq, ak, av, swa_sinks[0], batch, seq)

    n_experts = moe_router.shape[2]
    h3, hn3, meta, counts = _proj_residual(oa, swa_out_proj[0].astype(BF16), swa_out_bias[0], h2,
                                           ffn_norm[1], F32, "swa_out_proj", router=moe_router[0])
    tm = MOE_TILE
    counts = counts[0, :n_experts].astype(jnp.int32)
    tiles_per_expert = (counts + tm - 1) // tm
    tile_end = jnp.cumsum(tiles_per_expert)
    offsets = (tile_end - tiles_per_expert) * tm
    n_tiles = (TOP_K * n) // tm + n_experts
    tile_ids = jnp.arange(n_tiles, dtype=jnp.int32)
    tile_expert = jnp.sum((tile_ids[:, None] >= tile_end[None, :]).astype(jnp.int32), axis=1)
    tile_expert = jnp.minimum(tile_expert, n_experts - 1)
    n_used = tile_end[-1:].astype(jnp.int32)
    eidx = meta[:, 0:TOP_K].astype(jnp.int32)
    pos = offsets[eidx] + meta[:, 4:4 + TOP_K].astype(jnp.int32)
    pos_flat = pos.reshape(-1)

    xs = _scatter_rows(hn3, pos_flat, n_tiles * tm)
    ys = _swiglu(xs, moe_w_gate[0].astype(BF16), moe_w_up[0].astype(BF16), moe_w_down[0].astype(BF16),
                 tile_expert, n_used, tm, F32, "moe_swiglu")
    out = _combine(h3, meta, final_norm, ys, pos_flat)
    return out.reshape(batch, seq, d)
```

```python
import functools

import jax
import jax.numpy as jnp
from jax import lax
from jax.experimental import pallas as pl
from jax.experimental.pallas import tpu as pltpu

F32 = jnp.float32
BF16 = jnp.bfloat16

NORM_EPS = 1e-5

GLA_HEADS = 4
GLA_GATE_RANK = 16
GLA_GATE_TAU = 16.0
GLA_CHUNK = 64
N_Q_HEADS = 16
N_KV_HEADS = 4
HEAD_DIM = 64
GROUP = N_Q_HEADS // N_KV_HEADS
WINDOW = 128
ATT_BLOCK = 128
TOP_K = 2

LANES = 128
VMEM_LIMIT_BYTES = 56 * 2**20

ROW_TILE = 512
GLA_BLOCK = 512
GLA_GROUP = 256
FFN_TILE = 1024
FFN_F_TILE = 256
MOE_TILE = 1024
SWA_LOOKAHEAD = 4


def _params(*sem):
    return pltpu.CompilerParams(dimension_semantics=sem, vmem_limit_bytes=VMEM_LIMIT_BYTES)


def _rms(x, gain):
    y = x * lax.rsqrt(jnp.mean(x * x, axis=-1, keepdims=True) + NORM_EPS)
    return y * gain


def _silu(x):
    return x * (1.0 / (1.0 + jnp.exp(-x)))


def _norm_proj_kernel(x_ref, g_ref, w_ref, b_ref, *o_refs, splits):
    y = _rms(x_ref[...], g_ref[...]).astype(BF16)
    for (start, width), o_ref in zip(splits, o_refs):
        acc = jnp.dot(y, w_ref[:, start:start + width], preferred_element_type=F32)
        acc = acc + b_ref[:, start:start + width]
        o_ref[...] = acc.astype(o_ref.dtype)


def _norm_proj(x, gain, w, bias, splits, dtypes, name):
    n, d = x.shape
    nout = w.shape[1]
    tm = ROW_TILE
    assert n % tm == 0
    return pl.pallas_call(
        functools.partial(_norm_proj_kernel, splits=splits),
        grid=(n // tm,),
        in_specs=[
            pl.BlockSpec((tm, d), lambda i: (i, 0)),
            pl.BlockSpec((1, d), lambda i: (0, 0)),
            pl.BlockSpec((d, nout), lambda i: (0, 0)),
            pl.BlockSpec((1, nout), lambda i: (0, 0)),
        ],
        out_specs=[pl.BlockSpec((tm, wd), lambda i: (i, 0)) for (_, wd) in splits],
        out_shape=[jax.ShapeDtypeStruct((n, wd), dt) for (_, wd), dt in zip(splits, dtypes)],
        compiler_params=_params("arbitrary"),
        name=name,
    )(x, gain.reshape(1, d), w, bias.reshape(1, nout))


def _gla_direction(q_ref, k_ref, v_ref, lr_ref, wg_ref, bg_ref, state_ref, reverse):
    rows, dk = q_ref.shape
    c = GLA_CHUNK
    nc = rows // c
    q = q_ref[...].astype(F32) * (dk ** -0.5)
    k = k_ref[...].astype(F32)
    v = v_ref[...]
    z = jnp.dot(lr_ref[...].astype(BF16), wg_ref[...], preferred_element_type=F32) + bg_ref[...]
    la = (jnp.minimum(z, 0.0) - jnp.log(1.0 + jnp.exp(-jnp.abs(z)))) * (1.0 / GLA_GATE_TAU)
    la_hi = la.astype(BF16)
    la_lo = (la - la_hi.astype(F32)).astype(BF16)

    tn = (((0,), (0,)), ((), ()))
    nt = (((1,), (1,)), ((), ()))
    grp = GLA_GROUP
    ng = rows // grp
    ri = lax.broadcasted_iota(jnp.int32, (grp, grp), 0)
    ci = lax.broadcasted_iota(jnp.int32, (grp, grp), 1)
    same_chunk = (ri // c) == (ci // c)
    if reverse:
        cum_mask = same_chunk & (ci >= ri)
        att_mask = same_chunk & (ci > ri)
        ref_row, last_row = c // 2 - 1, 0
    else:
        cum_mask = same_chunk & (ci <= ri)
        att_mask = cum_mask
        ref_row, last_row = c // 2, c - 1
    cum = jnp.where(cum_mask, 1.0, 0.0).astype(BF16)

    la_hl = jnp.concatenate([la_hi, la_lo], axis=1)
    b = jnp.concatenate(
        [jnp.dot(cum, la_hl[g * grp:(g + 1) * grp], preferred_element_type=F32) for g in range(ng)],
        axis=0)
    b = b[:, :dk] + b[:, dk:]
    b3 = b.reshape(nc, c, dk)
    b_ref = b3[:, ref_row:ref_row + 1, :]
    b_last = b3[:, last_row:last_row + 1, :]
    q3 = q.reshape(nc, c, dk)
    k3 = k.reshape(nc, c, dk)
    qe = (q3 * jnp.exp(b3 - b_ref)).astype(BF16).reshape(rows, dk)
    ke = (k3 * jnp.exp(b_ref - b3)).astype(BF16).reshape(rows, dk)
    kd = (k3 * jnp.exp(b_last - b3)).astype(BF16).reshape(rows, dk)
    qb = (q3 * jnp.exp(b3)).astype(BF16).reshape(rows, dk)

    o_intra = []
    for g in range(ng):
        sl = slice(g * grp, (g + 1) * grp)
        s = lax.dot_general(qe[sl], ke[sl], nt, preferred_element_type=F32)
        s = jnp.where(att_mask, s, 0.0).astype(BF16)
        o_intra.append(jnp.dot(s, v[sl], preferred_element_type=F32))
    o_intra = jnp.concatenate(o_intra, axis=0)

    upd = [lax.dot_general(kd[j * c:(j + 1) * c], v[j * c:(j + 1) * c], tn, preferred_element_type=F32)
           for j in range(nc)]
    decay_rows = jnp.exp(b_last.reshape(nc, dk))
    decay_rows = jnp.concatenate([decay_rows, jnp.zeros((dk - nc, dk), F32)], axis=0)
    decay_cols = decay_rows.T

    state = state_ref[...]
    o_inter = [None] * nc
    for j in (range(nc - 1, -1, -1) if reverse else range(nc)):
        o_inter[j] = jnp.dot(qb[j * c:(j + 1) * c], state.astype(BF16), preferred_element_type=F32)
        state = state * decay_cols[:, j:j + 1] + upd[j]
    state_ref[...] = state
    return o_intra + jnp.concatenate(o_inter, axis=0)


def _gla_kernel(q_ref, k_ref, v_ref, lr_ref, r_ref, wgf_ref, bgf_ref, wgb_ref, bgb_ref, gain_ref,
                o_ref, state_ref, oacc_ref, *, nblk):
    i = pl.program_id(2)
    rows = q_ref.shape[0]

    @pl.when((i == 0) | (i == nblk))
    def _():
        state_ref[...] = jnp.zeros_like(state_ref)

    @pl.when(i < nblk)
    def _():
        o = _gla_direction(q_ref, k_ref, v_ref, lr_ref, wgf_ref, bgf_ref, state_ref, False)
        oacc_ref[pl.ds(pl.multiple_of(i * rows, rows), rows), :] = o

    @pl.when(i >= nblk)
    def _():
        j = 2 * nblk - 1 - i
        o = _gla_direction(q_ref, k_ref, v_ref, lr_ref, wgb_ref, bgb_ref, state_ref, True)
        o = o + oacc_ref[pl.ds(pl.multiple_of(j * rows, rows), rows), :]
        o = _rms(o, gain_ref[...])
        o_ref[...] = (o * _silu(r_ref[...].astype(F32))).astype(o_ref.dtype)


def _gla(q, k, v, lr, r, wgf, bgf, wgb, bgb, gain, batch, seq):
    n = q.shape[0]
    heads = GLA_HEADS
    dk = q.shape[1] // heads
    dv = v.shape[1] // heads
    rows = GLA_BLOCK
    nblk = seq // rows
    assert seq % rows == 0 and rows % GLA_CHUNK == 0

    def blk(i):
        return jnp.where(i < nblk, i, 2 * nblk - 1 - i)

    def late_blk(i):
        return jnp.where(i < nblk, nblk - 1, 2 * nblk - 1 - i)

    row_map = lambda b, h, i: (b * nblk + blk(i), h)
    return pl.pallas_call(
        functools.partial(_gla_kernel, nblk=nblk),
        grid=(batch, heads, 2 * nblk),
        in_specs=[
            pl.BlockSpec((rows, dk), row_map),
            pl.BlockSpec((rows, dk), row_map),
            pl.BlockSpec((rows, dv), row_map),
            pl.BlockSpec((rows, lr.shape[1]), lambda b, h, i: (b * nblk + blk(i), 0)),
            pl.BlockSpec((rows, dv), lambda b, h, i: (b * nblk + late_blk(i), h)),
            pl.BlockSpec((wgf.shape[0], dk), lambda b, h, i: (0, h)),
            pl.BlockSpec((1, dk), lambda b, h, i: (0, h)),
            pl.BlockSpec((wgb.shape[0], dk), lambda b, h, i: (0, h)),
            pl.BlockSpec((1, dk), lambda b, h, i: (0, h)),
            pl.BlockSpec((1, dv), lambda b, h, i: (0, h)),
        ],
        out_specs=pl.BlockSpec((rows, dv), lambda b, h, i: (b * nblk + late_blk(i), h)),
        out_shape=jax.ShapeDtypeStruct((n, heads * dv), BF16),
        scratch_shapes=[pltpu.VMEM((dk, dv), F32), pltpu.VMEM((seq, dv), F32)],
        compiler_params=_params("arbitrary", "arbitrary", "arbitrary"),
        name="gla",
    )(q, k, v, lr, r, wgf, bgf, wgb, bgb, gain)


def _proj_residual_kernel(a_ref, w_ref, b_ref, h_ref, g_ref, h_out_ref, hn_out_ref):
    h = h_ref[...] + (jnp.dot(a_ref[...], w_ref[...], preferred_element_type=F32) + b_ref[...])
    h_out_ref[...] = h
    hn_out_ref[...] = _rms(h, g_ref[...]).astype(hn_out_ref.dtype)


def _proj_residual_router_kernel(a_ref, w_ref, b_ref, h_ref, g_ref, rhi_ref, rlo_ref,
                                 h_out_ref, hn_out_ref, meta_ref, count_ref, *, n_experts):
    i = pl.program_id(0)
    h = h_ref[...] + (jnp.dot(a_ref[...], w_ref[...], preferred_element_type=F32) + b_ref[...])
    h_out_ref[...] = h
    hn = _rms(h, g_ref[...])
    hn_out_ref[...] = hn.astype(hn_out_ref.dtype)

    hn_hi = hn.astype(BF16)
    hn_lo = (hn - hn_hi.astype(F32)).astype(BF16)
    logits = (jnp.dot(hn_hi, rhi_ref[...], preferred_element_type=F32)
              + jnp.dot(hn_lo, rhi_ref[...], preferred_element_type=F32)
              + jnp.dot(hn_hi, rlo_ref[...], preferred_element_type=F32))
    tm = logits.shape[0]
    lane = lax.broadcasted_iota(jnp.int32, logits.shape, 1).astype(F32)
    neg = jnp.float32(-jnp.inf)
    logits = jnp.where(lane < n_experts, logits, neg)
    m1 = jnp.max(logits, axis=-1, keepdims=True)
    i1 = jnp.min(jnp.where(logits == m1, lane, float(LANES)), axis=-1, keepdims=True)
    rest = jnp.where(lane == i1, neg, logits)
    m2 = jnp.max(rest, axis=-1, keepdims=True)
    i2 = jnp.min(jnp.where(rest == m2, lane, float(LANES)), axis=-1, keepdims=True)
    e2 = jnp.exp(m2 - m1)
    w1 = 1.0 / (1.0 + e2)
    w2 = e2 / (1.0 + e2)

    @pl.when(i == 0)
    def _():
        count_ref[...] = jnp.zeros_like(count_ref)

    sel = (lane == i1) | (lane == i2)
    onehot = jnp.where(sel, 1.0, 0.0)
    ri = lax.broadcasted_iota(jnp.int32, (tm, tm), 0)
    ci = lax.broadcasted_iota(jnp.int32, (tm, tm), 1)
    strict_lower = jnp.where(ci < ri, 1.0, 0.0).astype(BF16)
    rank = jnp.dot(strict_lower, onehot.astype(BF16), preferred_element_type=F32) + count_ref[...]
    r1 = jnp.sum(jnp.where(lane == i1, rank, 0.0), axis=-1, keepdims=True)
    r2 = jnp.sum(jnp.where(lane == i2, rank, 0.0), axis=-1, keepdims=True)
    count_ref[...] = count_ref[...] + jnp.sum(onehot, axis=0, keepdims=True)

    meta = jnp.where(lane == 0, i1, 0.0)
    meta = jnp.where(lane == 1, i2, meta)
    meta = jnp.where(lane == 2, w1, meta)
    meta = jnp.where(lane == 3, w2, meta)
    meta = jnp.where(lane == 4, r1, meta)
    meta = jnp.where(lane == 5, r2, meta)
    meta_ref[...] = meta


def _proj_residual(a, w, bias, h, gain, hn_dtype, name, router=None):
    n, kdim = a.shape
    d = w.shape[1]
    tm = ROW_TILE
    assert n % tm == 0
    in_specs = [
        pl.BlockSpec((tm, kdim), lambda i: (i, 0)),
        pl.BlockSpec((kdim, d), lambda i: (0, 0)),
        pl.BlockSpec((1, d), lambda i: (0, 0)),
        pl.BlockSpec((tm, d), lambda i: (i, 0)),
        pl.BlockSpec((1, d), lambda i: (0, 0)),
    ]
    out_specs = [pl.BlockSpec((tm, d), lambda i: (i, 0)), pl.BlockSpec((tm, d), lambda i: (i, 0))]
    out_shape = [jax.ShapeDtypeStruct((n, d), F32), jax.ShapeDtypeStruct((n, d), hn_dtype)]
    args = [a, w, bias.reshape(1, d), h, gain.reshape(1, d)]
    if router is None:
        body = _proj_residual_kernel
    else:
        n_experts = router.shape[1]
        rpad = jnp.zeros((d, LANES), F32).at[:, :n_experts].set(router)
        rhi = rpad.astype(BF16)
        rlo = (rpad - rhi.astype(F32)).astype(BF16)
        args += [rhi, rlo]
        in_specs += [pl.BlockSpec((d, LANES), lambda i: (0, 0))] * 2
        out_specs += [pl.BlockSpec((tm, LANES), lambda i: (i, 0)),
                      pl.BlockSpec((1, LANES), lambda i: (0, 0))]
        out_shape += [jax.ShapeDtypeStruct((n, LANES), F32), jax.ShapeDtypeStruct((1, LANES), F32)]
        body = functools.partial(_proj_residual_router_kernel, n_experts=n_experts)
    return pl.pallas_call(
        body,
        grid=(n // tm,),
        in_specs=in_specs,
        out_specs=out_specs,
        out_shape=out_shape,
        compiler_params=_params("arbitrary"),
        name=name,
    )(*args)


def _swiglu_kernel(te_ref, nu_ref, x_ref, wgu_ref, wd_ref, *rest, residual):
    if residual:
        h_ref, o_ref, acc_ref = rest
    else:
        o_ref, acc_ref = rest
    i = pl.program_id(0)
    j = pl.program_id(1)
    used = i < nu_ref[0]
    tf = wd_ref.shape[0]

    @pl.when(used & (j == 0))
    def _():
        acc_ref[...] = jnp.zeros_like(acc_ref)

    @pl.when(used)
    def _():
        gu = jnp.dot(x_ref[...].astype(BF16), wgu_ref[...], preferred_element_type=F32)
        a = (_silu(gu[:, :tf]) * gu[:, tf:]).astype(BF16)
        acc_ref[...] += jnp.dot(a, wd_ref[...], preferred_element_type=F32)

    @pl.when(j == pl.num_programs(1) - 1)
    def _():
        @pl.when(used)
        def _():
            if residual:
                o_ref[...] = h_ref[...] + acc_ref[...]
            else:
                o_ref[...] = acc_ref[...].astype(o_ref.dtype)

        @pl.when(jnp.logical_not(used))
        def _():
            o_ref[...] = jnp.zeros_like(o_ref)


def _swiglu_weights(wg, wu, wd):
    e, d, f = wg.shape
    tf = FFN_F_TILE
    assert f % tf == 0
    nf = f // tf
    wgu = jnp.concatenate([wg.astype(BF16).reshape(e, d, nf, tf), wu.astype(BF16).reshape(e, d, nf, tf)],
                          axis=-1)
    return wgu.transpose(0, 2, 1, 3), wd.astype(BF16).reshape(e, nf, tf, d)


def _swiglu(x, wgu, wd, tile_expert, n_used, tm, out_dtype, name, residual=None):
    rows, d = x.shape
    nf, tf = wd.shape[1], wd.shape[2]
    assert rows % tm == 0

    def wmap(i, j, te, nu):
        return (te[i], jnp.where(i < nu[0], j, nf - 1), 0, 0)

    def xmap(i, j, te, nu):
        return (jnp.where(i < nu[0], i, nu[0] - 1), 0)

    in_specs = [
        pl.BlockSpec((tm, d), xmap),
        pl.BlockSpec((None, None, d, 2 * tf), wmap),
        pl.BlockSpec((None, None, tf, d), wmap),
    ]
    args = [x, wgu, wd]
    if residual is not None:
        in_specs.append(pl.BlockSpec((tm, d), lambda i, j, te, nu: (i, 0)))
        args.append(residual)
    return pl.pallas_call(
        functools.partial(_swiglu_kernel, residual=residual is not None),
        grid_spec=pltpu.PrefetchScalarGridSpec(
            num_scalar_prefetch=2,
            grid=(rows // tm, nf),
            in_specs=in_specs,
            out_specs=pl.BlockSpec((tm, d), lambda i, j, te, nu: (i, 0)),
            scratch_shapes=[pltpu.VMEM((tm, d), F32)],
        ),
        out_shape=jax.ShapeDtypeStruct((rows, d), out_dtype),
        compiler_params=_params("arbitrary", "arbitrary"),
        name=name,
    )(tile_expert, n_used, *args)


def _alibi_slope(head):
    return float(2.0 ** (-8.0 * (head + 1) / N_Q_HEADS))


def _swa_bias(t):
    qi = jnp.arange(t)[:, None]
    kj = jnp.arange(3 * t)[None, :]
    dist = jnp.abs(qi + t - kj)
    band = dist <= WINDOW
    valid = jnp.stack([band & (kj >= t), band, band & (kj < 2 * t)])
    slopes = jnp.asarray([_alibi_slope(h) for h in range(N_Q_HEADS)], F32)
    bias = -slopes[None, :, None, None] * dist.astype(F32)[None, None]
    bias = jnp.where(valid[:, None], bias, -jnp.inf)
    return bias.reshape(3, N_KV_HEADS, GROUP * t, 3 * t)


def _swa_kernel(sink_ref, q_ref, kp_ref, kc_ref, kn_ref, vp_ref, vc_ref, vn_ref, bias_ref, o_ref):
    t = q_ref.shape[0]
    lane = lax.broadcasted_iota(jnp.int32, (t, LANES), 1)
    low = lane < HEAD_DIM
    row_head = lax.broadcasted_iota(jnp.int32, (GROUP * t, 1), 0) // t
    ones = jnp.ones((3 * t, LANES), BF16)

    def window(p_ref, c_ref, n_ref, kvh):
        ks = slice(kvh * LANES, (kvh + 1) * LANES)
        return jnp.concatenate([p_ref[:, ks], c_ref[:, ks], n_ref[:, ks]], axis=0)

    blocks = []
    for kvh in range(N_KV_HEADS):
        qs = []
        for g in range(GROUP):
            head = kvh * GROUP + g
            col = (head // 2) * LANES
            q2 = q_ref[:, col:col + LANES]
            qs.append(jnp.where(low if head % 2 == 0 else jnp.logical_not(low), q2,
                                jnp.zeros_like(q2)))
        qg = jnp.concatenate(qs, axis=0)
        kw = window(kp_ref, kc_ref, kn_ref, kvh)
        s = lax.dot_general(qg, kw, (((1,), (1,)), ((), ())), preferred_element_type=F32)
        s = s + bias_ref[kvh]
        sink = jnp.full((GROUP * t, 1), sink_ref[kvh * GROUP], F32)
        for g in range(1, GROUP):
            sink = jnp.where(row_head == g, sink_ref[kvh * GROUP + g], sink)
        m = jnp.maximum(jnp.max(s, axis=-1, keepdims=True), sink)
        p = jnp.exp(s - m).astype(BF16)
        vw = jnp.concatenate([window(vp_ref, vc_ref, vn_ref, kvh), ones], axis=1)
        res = jnp.dot(p, vw, preferred_element_type=F32)
        out = res[:, :LANES] * (1.0 / (res[:, LANES:] + jnp.exp(sink - m)))
        for pair in range(GROUP // 2):
            a = out[(2 * pair) * t:(2 * pair + 1) * t]
            b = out[(2 * pair + 1) * t:(2 * pair + 2) * t]
            blocks.append(jnp.where(low, a, b).astype(o_ref.dtype))
    o_ref[...] = jnp.concatenate(blocks, axis=1)


def _swa(q, kdup, vdup, sinks, batch, seq):
    n, qw = q.shape
    t = ATT_BLOCK
    nblk = seq // t
    assert seq % t == 0 and nblk >= 2
    kvw = kdup.shape[1]
    bias = _swa_bias(t)
    prev = lambda b, i, s: (b * nblk + jnp.maximum(i - 1, 0), 0)
    cur = lambda b, i, s: (b * nblk + i, 0)
    nxt = lambda b, i, s: (b * nblk + jnp.minimum(i + 1, nblk - 1), 0)
    variant = lambda b, i, s: (jnp.where(i == 0, 0, jnp.where(i == nblk - 1, 2, 1)), 0, 0, 0)
    return pl.pallas_call(
        _swa_kernel,
        grid_spec=pltpu.PrefetchScalarGridSpec(
            num_scalar_prefetch=1,
            grid=(batch, nblk),
            in_specs=[
                pl.BlockSpec((t, qw), cur),
                pl.BlockSpec((t, kvw), prev), pl.BlockSpec((t, kvw), cur), pl.BlockSpec((t, kvw), nxt),
                pl.BlockSpec((t, kvw), prev), pl.BlockSpec((t, kvw), cur), pl.BlockSpec((t, kvw), nxt),
                pl.BlockSpec((None,) + bias.shape[1:], variant),
            ],
            out_specs=pl.BlockSpec((t, qw), cur),
        ),
        out_shape=jax.ShapeDtypeStruct((n, qw), BF16),
        compiler_params=_params("arbitrary", "arbitrary"),
        name="swa",
    )(sinks, q, kdup, kdup, kdup, vdup, vdup, vdup, bias)


def _scatter_kernel(pos_ref, pad_start_ref, pad_count_ref, x_ref, xs_ref, zero_ref, sem, zero_sem):
    tm = x_ref.shape[0]

    @pl.when(pl.program_id(0) == 0)
    def _():
        zero_ref[...] = jnp.zeros_like(zero_ref)
        for e in range(pad_start_ref.shape[0]):
            def zero_copy(r, e=e):
                dst = pad_start_ref[e] + r
                return pltpu.make_async_copy(zero_ref.at[pl.ds(0, 1)], xs_ref.at[pl.ds(dst, 1)], zero_sem)

            def zero_start(r, carry, zero_copy=zero_copy):
                zero_copy(r).start()
                return carry

            def zero_wait(r, carry, zero_copy=zero_copy):
                zero_copy(r).wait()
                return carry

            lax.fori_loop(0, pad_count_ref[e], zero_start, 0)
            lax.fori_loop(0, pad_count_ref[e], zero_wait, 0)

    def row_copy(t, s):
        dst = pos_ref[TOP_K * t + s]
        return pltpu.make_async_copy(x_ref.at[pl.ds(t, 1)], xs_ref.at[pl.ds(dst, 1)], sem)

    def start(t, carry):
        for s in range(TOP_K):
            row_copy(t, s).start(priority=s % 2)
        return carry

    def wait(t, carry):
        for s in range(TOP_K):
            row_copy(t, s).wait()
        return carry

    lax.fori_loop(0, tm, start, 0, unroll=8)
    lax.fori_loop(0, tm, wait, 0, unroll=8)


def _scatter_rows(x, pos_flat, pad_start, pad_count, total_rows):
    n, d = x.shape
    tm = ROW_TILE
    return pl.pallas_call(
        _scatter_kernel,
        grid=(n // tm,),
        in_specs=[
            pl.BlockSpec((TOP_K * tm,), lambda i: (i,), memory_space=pltpu.SMEM),
            pl.BlockSpec(memory_space=pltpu.SMEM),
            pl.BlockSpec(memory_space=pltpu.SMEM),
            pl.BlockSpec((tm, d), lambda i: (i, 0)),
        ],
        out_specs=pl.BlockSpec(memory_space=pl.ANY),
        out_shape=jax.ShapeDtypeStruct((total_rows, d), x.dtype),
        scratch_shapes=[pltpu.VMEM((8, d), x.dtype), pltpu.SemaphoreType.DMA, pltpu.SemaphoreType.DMA],
        compiler_params=_params("arbitrary"),
        name="moe_scatter",
    )(pos_flat, pad_start, pad_count, x)


def _combine_kernel(pos_ref, h_ref, meta_ref, g_ref, ys_ref, o_ref, y_ref, sem):
    tm = h_ref.shape[0]

    def row_copy(t, s):
        src = pos_ref[TOP_K * t + s]
        return pltpu.make_async_copy(ys_ref.at[pl.ds(src, 1)], y_ref.at[s, pl.ds(t, 1)], sem)

    def start(t, carry):
        for s in range(TOP_K):
            row_copy(t, s).start(priority=s % 2)
        return carry

    def wait(t, carry):
        for s in range(TOP_K):
            row_copy(t, s).wait()
        return carry

    lax.fori_loop(0, tm, start, 0, unroll=8)
    lax.fori_loop(0, tm, wait, 0, unroll=8)
    meta = meta_ref[...]
    w1 = meta[:, 2:3]
    w2 = meta[:, 3:4]
    moe = w1 * y_ref[0].astype(F32) + w2 * y_ref[1].astype(F32)
    o_ref[...] = _rms(h_ref[...] + moe, g_ref[...])


def _combine(h, meta, gain, ys, pos_flat):
    n, d = h.shape
    tm = ROW_TILE
    return pl.pallas_call(
        _combine_kernel,
        grid=(n // tm,),
        in_specs=[
            pl.BlockSpec((TOP_K * tm,), lambda i: (i,), memory_space=pltpu.SMEM),
            pl.BlockSpec((tm, d), lambda i: (i, 0)),
            pl.BlockSpec((tm, LANES), lambda i: (i, 0)),
            pl.BlockSpec((1, d), lambda i: (0, 0)),
            pl.BlockSpec(memory_space=pl.ANY),
        ],
        out_specs=pl.BlockSpec((tm, d), lambda i: (i, 0)),
        out_shape=jax.ShapeDtypeStruct((n, d), F32),
        scratch_shapes=[pltpu.VMEM((TOP_K, tm, d), ys.dtype), pltpu.SemaphoreType.DMA],
        compiler_params=_params("arbitrary"),
        name="moe_combine",
    )(pos_flat, h, meta, gain.reshape(1, d), ys)


def kernel(x, mix_norm, ffn_norm, gla_in_proj, gla_gate_w_fwd, gla_gate_b_fwd, gla_gate_w_bwd, gla_gate_b_bwd, gla_head_norm, gla_out_proj, swa_qkv_proj, swa_qkv_bias, swa_sinks, swa_out_proj, swa_out_bias, dense_w_gate, dense_w_up, dense_w_down, moe_router, moe_w_gate, moe_w_up, moe_w_down, final_norm):
    batch, seq, d = x.shape
    n = batch * seq
    h0 = x.reshape(n, d)
    zeros_d = jnp.zeros((d,), F32)

    key_w = gla_gate_w_fwd.shape[2]
    val_w = gla_head_norm.shape[1]
    rank = GLA_GATE_RANK
    in_w = gla_in_proj.shape[2]
    splits = ((0, key_w), (key_w, key_w), (2 * key_w, val_w), (2 * key_w + val_w, val_w),
              (2 * key_w + 2 * val_w, 2 * rank))
    q, k, v, r, lr = _norm_proj(h0, mix_norm[0], gla_in_proj[0].astype(BF16), jnp.zeros((in_w,), F32),
                                splits, (BF16, BF16, BF16, BF16, F32), "gla_in_proj")
    zero_gate = jnp.zeros((rank, key_w), F32)
    wgf = jnp.concatenate([gla_gate_w_fwd[0], zero_gate], axis=0).astype(BF16)
    wgb = jnp.concatenate([zero_gate, gla_gate_w_bwd[0]], axis=0).astype(BF16)
    og = _gla(q, k, v, lr, r, wgf, gla_gate_b_fwd[0].reshape(1, key_w), wgb,
              gla_gate_b_bwd[0].reshape(1, key_w), gla_head_norm[0].reshape(1, val_w), batch, seq)
    h1, hn1 = _proj_residual(og, gla_out_proj[0].astype(BF16), zeros_d, h0, ffn_norm[0], BF16,
                             "gla_out_proj")

    n_ffn_tiles = n // FFN_TILE
    dense_wgu, dense_wd = _swiglu_weights(dense_w_gate, dense_w_up, dense_w_down)
    h2 = _swiglu(hn1, dense_wgu, dense_wd, jnp.zeros((n_ffn_tiles,), jnp.int32),
                 jnp.full((1,), n_ffn_tiles, jnp.int32), FFN_TILE, F32, "dense_swiglu", residual=h1)

    qw = N_Q_HEADS * HEAD_DIM
    kvw = N_KV_HEADS * HEAD_DIM
    scale = HEAD_DIM ** -0.5
    wqkv, bqkv = swa_qkv_proj[0], swa_qkv_bias[0]

    def dup(m):
        lead = m.shape[:-1]
        m = m.reshape(lead + (N_KV_HEADS, 1, HEAD_DIM))
        return jnp.broadcast_to(m, lead + (N_KV_HEADS, 2, HEAD_DIM)).reshape(lead + (2 * kvw,))

    w_aug = jnp.concatenate([wqkv[:, :qw] * scale, dup(wqkv[:, qw:qw + kvw]), dup(wqkv[:, qw + kvw:])], axis=1)
    b_aug = jnp.concatenate([bqkv[:qw] * scale, dup(bqkv[qw:qw + kvw]), dup(bqkv[qw + kvw:])], axis=0)
    splits = ((0, qw), (qw, 2 * kvw), (qw + 2 * kvw, 2 * kvw))
    aq, ak, av = _norm_proj(h2, mix_norm[1], w_aug.astype(BF16), b_aug, splits, (BF16, BF16, BF16),
                            "swa_qkv_proj")
    oa = _swa(aq, ak, av, swa_sinks[0], batch, seq)

    n_experts = moe_router.shape[2]
    h3, hn3, meta, counts = _proj_residual(oa, swa_out_proj[0].astype(BF16), swa_out_bias[0], h2,
                                           ffn_norm[1], F32, "swa_out_proj", router=moe_router[0])
    tm = MOE_TILE
    counts = counts[0, :n_experts].astype(jnp.int32)
    tiles_per_expert = (counts + tm - 1) // tm
    tile_end = jnp.cumsum(tiles_per_expert)
    offsets = (tile_end - tiles_per_expert) * tm
    n_tiles = (TOP_K * n) // tm + n_experts
    tile_ids = jnp.arange(n_tiles, dtype=jnp.int32)
    tile_expert = jnp.sum((tile_ids[:, None] >= tile_end[None, :]).astype(jnp.int32), axis=1)
    tile_expert = jnp.minimum(tile_expert, n_experts - 1)
    n_used = tile_end[-1:].astype(jnp.int32)
    eidx = meta[:, 0:TOP_K].astype(jnp.int32)
    pos = offsets[eidx] + meta[:, 4:4 + TOP_K].astype(jnp.int32)
    pos_flat = pos.reshape(-1)

    pad_start = offsets + counts
    pad_count = tiles_per_expert * tm - counts
    xs = _scatter_rows(hn3, pos_flat, pad_start, pad_count, n_tiles * tm)
    moe_wgu, moe_wd = _swiglu_weights(moe_w_gate[0], moe_w_up[0], moe_w_down[0])
    ys = _swiglu(xs, moe_wgu, moe_wd, tile_expert, n_used, tm, F32, "moe_swiglu")
    out = _combine(h3, meta, final_norm, ys, pos_flat)
    return out.reshape(batch, seq, d)
```

```python
import functools

import jax
import jax.numpy as jnp
from jax import lax
from jax.experimental import pallas as pl
from jax.experimental.pallas import tpu as pltpu

F32 = jnp.float32
BF16 = jnp.bfloat16

NORM_EPS = 1e-5

GLA_HEADS = 4
GLA_GATE_RANK = 16
GLA_GATE_TAU = 16.0
GLA_CHUNK = 64
N_Q_HEADS = 16
N_KV_HEADS = 4
HEAD_DIM = 64
GROUP = N_Q_HEADS // N_KV_HEADS
WINDOW = 128
ATT_BLOCK = 128
TOP_K = 2

LANES = 128
VMEM_LIMIT_BYTES = 56 * 2**20

ROW_TILE = 512
GLA_BLOCK = 512
GLA_GROUP = 256
GLA_HEADS_PER_STEP = 4
FFN_TILE = 1024
FFN_F_TILE = 256
MOE_TILE = 1024
SWA_LOOKAHEAD = 4


def _params(*sem):
    return pltpu.CompilerParams(dimension_semantics=sem, vmem_limit_bytes=VMEM_LIMIT_BYTES)


def _rms(x, gain):
    y = x * lax.rsqrt(jnp.mean(x * x, axis=-1, keepdims=True) + NORM_EPS)
    return y * gain


def _silu(x):
    return x * (1.0 / (1.0 + jnp.exp(-x)))


def _norm_proj_kernel(x_ref, g_ref, w_ref, b_ref, *o_refs, splits):
    y = _rms(x_ref[...], g_ref[...]).astype(BF16)
    for (start, width), o_ref in zip(splits, o_refs):
        acc = jnp.dot(y, w_ref[:, start:start + width], preferred_element_type=F32)
        acc = acc + b_ref[:, start:start + width]
        o_ref[...] = acc.astype(o_ref.dtype)


def _norm_proj(x, gain, w, bias, splits, dtypes, name):
    n, d = x.shape
    nout = w.shape[1]
    tm = ROW_TILE
    assert n % tm == 0
    return pl.pallas_call(
        functools.partial(_norm_proj_kernel, splits=splits),
        grid=(n // tm,),
        in_specs=[
            pl.BlockSpec((tm, d), lambda i: (i, 0)),
            pl.BlockSpec((1, d), lambda i: (0, 0)),
            pl.BlockSpec((d, nout), lambda i: (0, 0)),
            pl.BlockSpec((1, nout), lambda i: (0, 0)),
        ],
        out_specs=[pl.BlockSpec((tm, wd), lambda i: (i, 0)) for (_, wd) in splits],
        out_shape=[jax.ShapeDtypeStruct((n, wd), dt) for (_, wd), dt in zip(splits, dtypes)],
        compiler_params=_params("arbitrary"),
        name=name,
    )(x, gain.reshape(1, d), w, bias.reshape(1, nout))


def _gla_direction(q_ref, k_ref, v_ref, lr_ref, wg_ref, bg_ref, state_ref, reverse):
    heads, dk, dv = state_ref.shape
    rows, width = q_ref.shape
    c = GLA_CHUNK
    nc = rows // c
    q = q_ref[...].astype(F32) * (dk ** -0.5)
    k = k_ref[...].astype(F32)
    v = v_ref[...]
    z = jnp.dot(lr_ref[...].astype(BF16), wg_ref[...], preferred_element_type=F32) + bg_ref[...]
    la = (jnp.minimum(z, 0.0) - jnp.log(1.0 + jnp.exp(-jnp.abs(z)))) * (1.0 / GLA_GATE_TAU)
    la_hi = la.astype(BF16)
    la_lo = (la - la_hi.astype(F32)).astype(BF16)

    tn = (((0,), (0,)), ((), ()))
    nt = (((1,), (1,)), ((), ()))
    grp = GLA_GROUP
    ng = rows // grp
    ri = lax.broadcasted_iota(jnp.int32, (grp, grp), 0)
    ci = lax.broadcasted_iota(jnp.int32, (grp, grp), 1)
    same_chunk = (ri // c) == (ci // c)
    if reverse:
        cum_mask = same_chunk & (ci >= ri)
        att_mask = same_chunk & (ci > ri)
        ref_row, last_row = c // 2 - 1, 0
    else:
        cum_mask = same_chunk & (ci <= ri)
        att_mask = cum_mask
        ref_row, last_row = c // 2, c - 1
    cum = jnp.where(cum_mask, 1.0, 0.0).astype(BF16)

    la_hl = jnp.concatenate([la_hi, la_lo], axis=1)
    b = jnp.concatenate(
        [jnp.dot(cum, la_hl[g * grp:(g + 1) * grp], preferred_element_type=F32) for g in range(ng)],
        axis=0)
    b = b[:, :width] + b[:, width:]
    b3 = b.reshape(nc, c, width)
    b_ref = b3[:, ref_row:ref_row + 1, :]
    b_last = b3[:, last_row:last_row + 1, :]
    q3 = q.reshape(nc, c, width)
    k3 = k.reshape(nc, c, width)
    qe = (q3 * jnp.exp(b3 - b_ref)).astype(BF16).reshape(rows, width)
    ke = (k3 * jnp.exp(b_ref - b3)).astype(BF16).reshape(rows, width)
    kd = (k3 * jnp.exp(b_last - b3)).astype(BF16).reshape(rows, width)
    qb = (q3 * jnp.exp(b3)).astype(BF16).reshape(rows, width)
    decay_rows = jnp.exp(b_last.reshape(nc, width))

    outs = []
    for h in range(heads):
        ks = slice(h * dk, (h + 1) * dk)
        vh = v[:, h * dv:(h + 1) * dv]
        o_intra = []
        for g in range(ng):
            sl = slice(g * grp, (g + 1) * grp)
            s = lax.dot_general(qe[sl, ks], ke[sl, ks], nt, preferred_element_type=F32)
            s = jnp.where(att_mask, s, 0.0).astype(BF16)
            o_intra.append(jnp.dot(s, vh[sl], preferred_element_type=F32))
        o_intra = jnp.concatenate(o_intra, axis=0)

        upd = [lax.dot_general(kd[j * c:(j + 1) * c, ks], vh[j * c:(j + 1) * c], tn,
                               preferred_element_type=F32) for j in range(nc)]
        decay_cols = jnp.concatenate([decay_rows[:, ks], jnp.zeros((dk - nc, dk), F32)], axis=0).T

        state = state_ref[h]
        o_inter = [None] * nc
        for j in (range(nc - 1, -1, -1) if reverse else range(nc)):
            o_inter[j] = jnp.dot(qb[j * c:(j + 1) * c, ks], state.astype(BF16),
                                 preferred_element_type=F32)
            state = state * decay_cols[:, j:j + 1] + upd[j]
        state_ref[h] = state
        outs.append(o_intra + jnp.concatenate(o_inter, axis=0))
    return jnp.concatenate(outs, axis=1)


def _gla_kernel(q_ref, k_ref, v_ref, lr_ref, r_ref, wgf_ref, bgf_ref, wgb_ref, bgb_ref, gain_ref,
                o_ref, state_ref, oacc_ref, *, nblk):
    i = pl.program_id(2)
    rows = q_ref.shape[0]
    heads, _, dv = state_ref.shape

    @pl.when((i == 0) | (i == nblk))
    def _():
        state_ref[...] = jnp.zeros_like(state_ref)

    @pl.when(i < nblk)
    def _():
        o = _gla_direction(q_ref, k_ref, v_ref, lr_ref, wgf_ref, bgf_ref, state_ref, False)
        oacc_ref[pl.ds(pl.multiple_of(i * rows, rows), rows), :] = o

    @pl.when(i >= nblk)
    def _():
        j = 2 * nblk - 1 - i
        o = _gla_direction(q_ref, k_ref, v_ref, lr_ref, wgb_ref, bgb_ref, state_ref, True)
        o = o + oacc_ref[pl.ds(pl.multiple_of(j * rows, rows), rows), :]
        gain = gain_ref[...]
        o = jnp.concatenate([_rms(o[:, h * dv:(h + 1) * dv], gain[:, h * dv:(h + 1) * dv])
                             for h in range(heads)], axis=1)
        o_ref[...] = (o * _silu(r_ref[...].astype(F32))).astype(o_ref.dtype)


def _gla(q, k, v, lr, r, wgf, bgf, wgb, bgb, gain, batch, seq):
    n = q.shape[0]
    heads = GLA_HEADS
    hps = GLA_HEADS_PER_STEP
    dk = q.shape[1] // heads
    dv = v.shape[1] // heads
    rows = GLA_BLOCK
    nblk = seq // rows
    assert seq % rows == 0 and rows % GLA_GROUP == 0 and GLA_GROUP % GLA_CHUNK == 0 and heads % hps == 0

    def blk(i):
        return jnp.where(i < nblk, i, 2 * nblk - 1 - i)

    def late_blk(i):
        return jnp.where(i < nblk, nblk - 1, 2 * nblk - 1 - i)

    row_map = lambda b, h, i: (b * nblk + blk(i), h)
    return pl.pallas_call(
        functools.partial(_gla_kernel, nblk=nblk),
        grid=(batch, heads // hps, 2 * nblk),
        in_specs=[
            pl.BlockSpec((rows, hps * dk), row_map),
            pl.BlockSpec((rows, hps * dk), row_map),
            pl.BlockSpec((rows, hps * dv), row_map),
            pl.BlockSpec((rows, lr.shape[1]), lambda b, h, i: (b * nblk + blk(i), 0)),
            pl.BlockSpec((rows, hps * dv), lambda b, h, i: (b * nblk + late_blk(i), h)),
            pl.BlockSpec((wgf.shape[0], hps * dk), lambda b, h, i: (0, h)),
            pl.BlockSpec((1, hps * dk), lambda b, h, i: (0, h)),
            pl.BlockSpec((wgb.shape[0], hps * dk), lambda b, h, i: (0, h)),
            pl.BlockSpec((1, hps * dk), lambda b, h, i: (0, h)),
            pl.BlockSpec((1, hps * dv), lambda b, h, i: (0, h)),
        ],
        out_specs=pl.BlockSpec((rows, hps * dv), lambda b, h, i: (b * nblk + late_blk(i), h)),
        out_shape=jax.ShapeDtypeStruct((n, heads * dv), BF16),
        scratch_shapes=[pltpu.VMEM((hps, dk, dv), F32), pltpu.VMEM((seq, hps * dv), F32)],
        compiler_params=_params("arbitrary", "arbitrary", "arbitrary"),
        name="gla",
    )(q, k, v, lr, r, wgf, bgf, wgb, bgb, gain)


def _proj_residual_kernel(a_ref, w_ref, b_ref, h_ref, g_ref, h_out_ref, hn_out_ref):
    h = h_ref[...] + (jnp.dot(a_ref[...], w_ref[...], preferred_element_type=F32) + b_ref[...])
    h_out_ref[...] = h
    hn_out_ref[...] = _rms(h, g_ref[...]).astype(hn_out_ref.dtype)


def _proj_residual_router_kernel(a_ref, w_ref, b_ref, h_ref, g_ref, rhi_ref, rlo_ref,
                                 h_out_ref, hn_out_ref, meta_ref, count_ref, *, n_experts):
    i = pl.program_id(0)
    h = h_ref[...] + (jnp.dot(a_ref[...], w_ref[...], preferred_element_type=F32) + b_ref[...])
    h_out_ref[...] = h
    hn = _rms(h, g_ref[...])
    hn_out_ref[...] = hn.astype(hn_out_ref.dtype)

    hn_hi = hn.astype(BF16)
    hn_lo = (hn - hn_hi.astype(F32)).astype(BF16)
    logits = (jnp.dot(hn_hi, rhi_ref[...], preferred_element_type=F32)
              + jnp.dot(hn_lo, rhi_ref[...], preferred_element_type=F32)
              + jnp.dot(hn_hi, rlo_ref[...], preferred_element_type=F32))
    tm = logits.shape[0]
    lane = lax.broadcasted_iota(jnp.int32, logits.shape, 1).astype(F32)
    neg = jnp.float32(-jnp.inf)
    logits = jnp.where(lane < n_experts, logits, neg)
    m1 = jnp.max(logits, axis=-1, keepdims=True)
    i1 = jnp.min(jnp.where(logits == m1, lane, float(LANES)), axis=-1, keepdims=True)
    rest = jnp.where(lane == i1, neg, logits)
    m2 = jnp.max(rest, axis=-1, keepdims=True)
    i2 = jnp.min(jnp.where(rest == m2, lane, float(LANES)), axis=-1, keepdims=True)
    e2 = jnp.exp(m2 - m1)
    w1 = 1.0 / (1.0 + e2)
    w2 = e2 / (1.0 + e2)

    @pl.when(i == 0)
    def _():
        count_ref[...] = jnp.zeros_like(count_ref)

    sel = (lane == i1) | (lane == i2)
    onehot = jnp.where(sel, 1.0, 0.0)
    ri = lax.broadcasted_iota(jnp.int32, (tm, tm), 0)
    ci = lax.broadcasted_iota(jnp.int32, (tm, tm), 1)
    strict_lower = jnp.where(ci < ri, 1.0, 0.0).astype(BF16)
    rank = jnp.dot(strict_lower, onehot.astype(BF16), preferred_element_type=F32) + count_ref[...]
    r1 = jnp.sum(jnp.where(lane == i1, rank, 0.0), axis=-1, keepdims=True)
    r2 = jnp.sum(jnp.where(lane == i2, rank, 0.0), axis=-1, keepdims=True)
    count_ref[...] = count_ref[...] + jnp.sum(onehot, axis=0, keepdims=True)

    meta = jnp.where(lane == 0, i1, 0.0)
    meta = jnp.where(lane == 1, i2, meta)
    meta = jnp.where(lane == 2, w1, meta)
    meta = jnp.where(lane == 3, w2, meta)
    meta = jnp.where(lane == 4, r1, meta)
    meta = jnp.where(lane == 5, r2, meta)
    meta_ref[...] = meta


def _proj_residual(a, w, bias, h, gain, hn_dtype, name, router=None):
    n, kdim = a.shape
    d = w.shape[1]
    tm = ROW_TILE
    assert n % tm == 0
    in_specs = [
        pl.BlockSpec((tm, kdim), lambda i: (i, 0)),
        pl.BlockSpec((kdim, d), lambda i: (0, 0)),
        pl.BlockSpec((1, d), lambda i: (0, 0)),
        pl.BlockSpec((tm, d), lambda i: (i, 0)),
        pl.BlockSpec((1, d), lambda i: (0, 0)),
    ]
    out_specs = [pl.BlockSpec((tm, d), lambda i: (i, 0)), pl.BlockSpec((tm, d), lambda i: (i, 0))]
    out_shape = [jax.ShapeDtypeStruct((n, d), F32), jax.ShapeDtypeStruct((n, d), hn_dtype)]
    args = [a, w, bias.reshape(1, d), h, gain.reshape(1, d)]
    if router is None:
        body = _proj_residual_kernel
    else:
        n_experts = router.shape[1]
        rpad = jnp.zeros((d, LANES), F32).at[:, :n_experts].set(router)
        rhi = rpad.astype(BF16)
        rlo = (rpad - rhi.astype(F32)).astype(BF16)
        args += [rhi, rlo]
        in_specs += [pl.BlockSpec((d, LANES), lambda i: (0, 0))] * 2
        out_specs += [pl.BlockSpec((tm, LANES), lambda i: (i, 0)),
                      pl.BlockSpec((1, LANES), lambda i: (0, 0))]
        out_shape += [jax.ShapeDtypeStruct((n, LANES), F32), jax.ShapeDtypeStruct((1, LANES), F32)]
        body = functools.partial(_proj_residual_router_kernel, n_experts=n_experts)
    return pl.pallas_call(
        body,
        grid=(n // tm,),
        in_specs=in_specs,
        out_specs=out_specs,
        out_shape=out_shape,
        compiler_params=_params("arbitrary"),
        name=name,
    )(*args)


def _swiglu_kernel(te_ref, nu_ref, x_ref, wgu_ref, wd_ref, *rest, residual):
    if residual:
        h_ref, o_ref, acc_ref = rest
    else:
        o_ref, acc_ref = rest
    i = pl.program_id(0)
    j = pl.program_id(1)
    used = i < nu_ref[0]
    tf = wd_ref.shape[0]

    @pl.when(used & (j == 0))
    def _():
        acc_ref[...] = jnp.zeros_like(acc_ref)

    @pl.when(used)
    def _():
        gu = jnp.dot(x_ref[...].astype(BF16), wgu_ref[...], preferred_element_type=F32)
        a = (_silu(gu[:, :tf]) * gu[:, tf:]).astype(BF16)
        acc_ref[...] += jnp.dot(a, wd_ref[...], preferred_element_type=F32)

    @pl.when(j == pl.num_programs(1) - 1)
    def _():
        @pl.when(used)
        def _():
            if residual:
                o_ref[...] = h_ref[...] + acc_ref[...]
            else:
                o_ref[...] = acc_ref[...].astype(o_ref.dtype)

        @pl.when(jnp.logical_not(used))
        def _():
            o_ref[...] = jnp.zeros_like(o_ref)


def _swiglu_weights(wg, wu, wd):
    e, d, f = wg.shape
    tf = FFN_F_TILE
    assert f % tf == 0
    nf = f // tf
    wgu = jnp.concatenate([wg.astype(BF16).reshape(e, d, nf, tf), wu.astype(BF16).reshape(e, d, nf, tf)],
                          axis=-1)
    return wgu.transpose(0, 2, 1, 3), wd.astype(BF16).reshape(e, nf, tf, d)


def _swiglu(x, wgu, wd, tile_expert, n_used, tm, out_dtype, name, residual=None):
    rows, d = x.shape
    nf, tf = wd.shape[1], wd.shape[2]
    assert rows % tm == 0

    def wmap(i, j, te, nu):
        return (te[i], jnp.where(i < nu[0], j, nf - 1), 0, 0)

    def xmap(i, j, te, nu):
        return (jnp.where(i < nu[0], i, nu[0] - 1), 0)

    in_specs = [
        pl.BlockSpec((tm, d), xmap),
        pl.BlockSpec((None, None, d, 2 * tf), wmap),
        pl.BlockSpec((None, None, tf, d), wmap),
    ]
    args = [x, wgu, wd]
    if residual is not None:
        in_specs.append(pl.BlockSpec((tm, d), lambda i, j, te, nu: (i, 0)))
        args.append(residual)
    return pl.pallas_call(
        functools.partial(_swiglu_kernel, residual=residual is not None),
        grid_spec=pltpu.PrefetchScalarGridSpec(
            num_scalar_prefetch=2,
            grid=(rows // tm, nf),
            in_specs=in_specs,
            out_specs=pl.BlockSpec((tm, d), lambda i, j, te, nu: (i, 0)),
            scratch_shapes=[pltpu.VMEM((tm, d), F32)],
        ),
        out_shape=jax.ShapeDtypeStruct((rows, d), out_dtype),
        compiler_params=_params("arbitrary", "arbitrary"),
        name=name,
    )(tile_expert, n_used, *args)


def _alibi_slope(head):
    return float(2.0 ** (-8.0 * (head + 1) / N_Q_HEADS))


def _swa_bias(t):
    qi = jnp.arange(t)[:, None]
    kj = jnp.arange(3 * t)[None, :]
    dist = jnp.abs(qi + t - kj)
    band = dist <= WINDOW
    valid = jnp.stack([band & (kj >= t), band, band & (kj < 2 * t)])
    slopes = jnp.asarray([_alibi_slope(h) for h in range(N_Q_HEADS)], F32)
    bias = -slopes[None, :, None, None] * dist.astype(F32)[None, None]
    bias = jnp.where(valid[:, None], bias, -jnp.inf)
    return bias.reshape(3, N_KV_HEADS, GROUP * t, 3 * t)


def _swa_kernel(sink_ref, q_ref, kp_ref, kc_ref, kn_ref, vp_ref, vc_ref, vn_ref, bias_ref, o_ref):
    t = q_ref.shape[0]
    lane = lax.broadcasted_iota(jnp.int32, (t, LANES), 1)
    low = lane < HEAD_DIM
    row_head = lax.broadcasted_iota(jnp.int32, (GROUP * t, 1), 0) // t
    ones = jnp.ones((3 * t, LANES), BF16)

    def window(p_ref, c_ref, n_ref, kvh):
        ks = slice(kvh * LANES, (kvh + 1) * LANES)
        return jnp.concatenate([p_ref[:, ks], c_ref[:, ks], n_ref[:, ks]], axis=0)

    blocks = []
    for kvh in range(N_KV_HEADS):
        qs = []
        for g in range(GROUP):
            head = kvh * GROUP + g
            col = (head // 2) * LANES
            q2 = q_ref[:, col:col + LANES]
            qs.append(jnp.where(low if head % 2 == 0 else jnp.logical_not(low), q2,
                                jnp.zeros_like(q2)))
        qg = jnp.concatenate(qs, axis=0)
        kw = window(kp_ref, kc_ref, kn_ref, kvh)
        s = lax.dot_general(qg, kw, (((1,), (1,)), ((), ())), preferred_element_type=F32)
        s = s + bias_ref[kvh]
        sink = jnp.full((GROUP * t, 1), sink_ref[kvh * GROUP], F32)
        for g in range(1, GROUP):
            sink = jnp.where(row_head == g, sink_ref[kvh * GROUP + g], sink)
        m = jnp.maximum(jnp.max(s, axis=-1, keepdims=True), sink)
        p = jnp.exp(s - m).astype(BF16)
        vw = jnp.concatenate([window(vp_ref, vc_ref, vn_ref, kvh), ones], axis=1)
        res = jnp.dot(p, vw, preferred_element_type=F32)
        out = res[:, :LANES] * (1.0 / (res[:, LANES:] + jnp.exp(sink - m)))
        for pair in range(GROUP // 2):
            a = out[(2 * pair) * t:(2 * pair + 1) * t]
            b = out[(2 * pair + 1) * t:(2 * pair + 2) * t]
            blocks.append(jnp.where(low, a, b).astype(o_ref.dtype))
    o_ref[...] = jnp.concatenate(blocks, axis=1)


def _swa(q, kdup, vdup, sinks, batch, seq):
    n, qw = q.shape
    t = ATT_BLOCK
    nblk = seq // t
    assert seq % t == 0 and nblk >= 2
    kvw = kdup.shape[1]
    bias = _swa_bias(t)
    prev = lambda b, i, s: (b * nblk + jnp.maximum(i - 1, 0), 0)
    cur = lambda b, i, s: (b * nblk + i, 0)
    nxt = lambda b, i, s: (b * nblk + jnp.minimum(i + 1, nblk - 1), 0)
    variant = lambda b, i, s: (jnp.where(i == 0, 0, jnp.where(i == nblk - 1, 2, 1)), 0, 0, 0)
    return pl.pallas_call(
        _swa_kernel,
        grid_spec=pltpu.PrefetchScalarGridSpec(
            num_scalar_prefetch=1,
            grid=(batch, nblk),
            in_specs=[
                pl.BlockSpec((t, qw), cur),
                pl.BlockSpec((t, kvw), prev), pl.BlockSpec((t, kvw), cur), pl.BlockSpec((t, kvw), nxt),
                pl.BlockSpec((t, kvw), prev), pl.BlockSpec((t, kvw), cur), pl.BlockSpec((t, kvw), nxt),
                pl.BlockSpec((None,) + bias.shape[1:], variant),
            ],
            out_specs=pl.BlockSpec((t, qw), cur),
        ),
        out_shape=jax.ShapeDtypeStruct((n, qw), BF16),
        compiler_params=_params("arbitrary", "arbitrary"),
        name="swa",
    )(sinks, q, kdup, kdup, kdup, vdup, vdup, vdup, bias)


def _scatter_kernel(pos_ref, pad_start_ref, pad_count_ref, x_ref, xs_ref, zero_ref, sem, zero_sem):
    tm = x_ref.shape[0]

    @pl.when(pl.program_id(0) == 0)
    def _():
        zero_ref[...] = jnp.zeros_like(zero_ref)
        for e in range(pad_start_ref.shape[0]):
            def zero_copy(r, e=e):
                dst = pad_start_ref[e] + r
                return pltpu.make_async_copy(zero_ref.at[pl.ds(0, 1)], xs_ref.at[pl.ds(dst, 1)], zero_sem)

            def zero_start(r, carry, zero_copy=zero_copy):
                zero_copy(r).start()
                return carry

            def zero_wait(r, carry, zero_copy=zero_copy):
                zero_copy(r).wait()
                return carry

            lax.fori_loop(0, pad_count_ref[e], zero_start, 0)
            lax.fori_loop(0, pad_count_ref[e], zero_wait, 0)

    def row_copy(t, s):
        dst = pos_ref[TOP_K * t + s]
        return pltpu.make_async_copy(x_ref.at[pl.ds(t, 1)], xs_ref.at[pl.ds(dst, 1)], sem)

    def start(t, carry):
        for s in range(TOP_K):
            row_copy(t, s).start()
        return carry

    def wait(t, carry):
        for s in range(TOP_K):
            row_copy(t, s).wait()
        return carry

    lax.fori_loop(0, tm, start, 0, unroll=8)
    lax.fori_loop(0, tm, wait, 0, unroll=8)


def _scatter_rows(x, pos_flat, pad_start, pad_count, total_rows):
    n, d = x.shape
    tm = ROW_TILE
    return pl.pallas_call(
        _scatter_kernel,
        grid=(n // tm,),
        in_specs=[
            pl.BlockSpec((TOP_K * tm,), lambda i: (i,), memory_space=pltpu.SMEM),
            pl.BlockSpec(memory_space=pltpu.SMEM),
            pl.BlockSpec(memory_space=pltpu.SMEM),
            pl.BlockSpec((tm, d), lambda i: (i, 0)),
        ],
        out_specs=pl.BlockSpec(memory_space=pl.ANY),
        out_shape=jax.ShapeDtypeStruct((total_rows, d), x.dtype),
        scratch_shapes=[pltpu.VMEM((8, d), x.dtype), pltpu.SemaphoreType.DMA, pltpu.SemaphoreType.DMA],
        compiler_params=_params("arbitrary"),
        name="moe_scatter",
    )(pos_flat, pad_start, pad_count, x)


def _combine_kernel(pos_ref, h_ref, meta_ref, g_ref, ys_ref, o_ref, y_ref, sem):
    tm = h_ref.shape[0]

    def row_copy(t, s):
        src = pos_ref[TOP_K * t + s]
        return pltpu.make_async_copy(ys_ref.at[pl.ds(src, 1)], y_ref.at[s, pl.ds(t, 1)], sem)

    def start(t, carry):
        for s in range(TOP_K):
            row_copy(t, s).start()
        return carry

    def wait(t, carry):
        for s in range(TOP_K):
            row_copy(t, s).wait()
        return carry

    lax.fori_loop(0, tm, start, 0, unroll=8)
    lax.fori_loop(0, tm, wait, 0, unroll=8)
    meta = meta_ref[...]
    w1 = meta[:, 2:3]
    w2 = meta[:, 3:4]
    moe = w1 * y_ref[0].astype(F32) + w2 * y_ref[1].astype(F32)
    o_ref[...] = _rms(h_ref[...] + moe, g_ref[...])


def _combine(h, meta, gain, ys, pos_flat):
    n, d = h.shape
    tm = ROW_TILE
    return pl.pallas_call(
        _combine_kernel,
        grid=(n // tm,),
        in_specs=[
            pl.BlockSpec((TOP_K * tm,), lambda i: (i,), memory_space=pltpu.SMEM),
            pl.BlockSpec((tm, d), lambda i: (i, 0)),
            pl.BlockSpec((tm, LANES), lambda i: (i, 0)),
            pl.BlockSpec((1, d), lambda i: (0, 0)),
            pl.BlockSpec(memory_space=pl.ANY),
        ],
        out_specs=pl.BlockSpec((tm, d), lambda i: (i, 0)),
        out_shape=jax.ShapeDtypeStruct((n, d), F32),
        scratch_shapes=[pltpu.VMEM((TOP_K, tm, d), ys.dtype), pltpu.SemaphoreType.DMA],
        compiler_params=_params("arbitrary"),
        name="moe_combine",
    )(pos_flat, h, meta, gain.reshape(1, d), ys)


def kernel(x, mix_norm, ffn_norm, gla_in_proj, gla_gate_w_fwd, gla_gate_b_fwd, gla_gate_w_bwd, gla_gate_b_bwd, gla_head_norm, gla_out_proj, swa_qkv_proj, swa_qkv_bias, swa_sinks, swa_out_proj, swa_out_bias, dense_w_gate, dense_w_up, dense_w_down, moe_router, moe_w_gate, moe_w_up, moe_w_down, final_norm):
    batch, seq, d = x.shape
    n = batch * seq
    h0 = x.reshape(n, d)
    zeros_d = jnp.zeros((d,), F32)

    key_w = gla_gate_w_fwd.shape[2]
    val_w = gla_head_norm.shape[1]
    rank = GLA_GATE_RANK
    in_w = gla_in_proj.shape[2]
    splits = ((0, key_w), (key_w, key_w), (2 * key_w, val_w), (2 * key_w + val_w, val_w),
              (2 * key_w + 2 * val_w, 2 * rank))
    q, k, v, r, lr = _norm_proj(h0, mix_norm[0], gla_in_proj[0].astype(BF16), jnp.zeros((in_w,), F32),
                                splits, (BF16, BF16, BF16, BF16, F32), "gla_in_proj")
    zero_gate = jnp.zeros((rank, key_w), F32)
    wgf = jnp.concatenate([gla_gate_w_fwd[0], zero_gate], axis=0).astype(BF16)
    wgb = jnp.concatenate([zero_gate, gla_gate_w_bwd[0]], axis=0).astype(BF16)
    og = _gla(q, k, v, lr, r, wgf, gla_gate_b_fwd[0].reshape(1, key_w), wgb,
              gla_gate_b_bwd[0].reshape(1, key_w), gla_head_norm[0].reshape(1, val_w), batch, seq)
    h1, hn1 = _proj_residual(og, gla_out_proj[0].astype(BF16), zeros_d, h0, ffn_norm[0], BF16,
                             "gla_out_proj")

    n_ffn_tiles = n // FFN_TILE
    dense_wgu, dense_wd = _swiglu_weights(dense_w_gate, dense_w_up, dense_w_down)
    h2 = _swiglu(hn1, dense_wgu, dense_wd, jnp.zeros((n_ffn_tiles,), jnp.int32),
                 jnp.full((1,), n_ffn_tiles, jnp.int32), FFN_TILE, F32, "dense_swiglu", residual=h1)

    qw = N_Q_HEADS * HEAD_DIM
    kvw = N_KV_HEADS * HEAD_DIM
    scale = HEAD_DIM ** -0.5
    wqkv, bqkv = swa_qkv_proj[0], swa_qkv_bias[0]

    def dup(m):
        lead = m.shape[:-1]
        m = m.reshape(lead + (N_KV_HEADS, 1, HEAD_DIM))
        return jnp.broadcast_to(m, lead + (N_KV_HEADS, 2, HEAD_DIM)).reshape(lead + (2 * kvw,))

    w_aug = jnp.concatenate([wqkv[:, :qw] * scale, dup(wqkv[:, qw:qw + kvw]), dup(wqkv[:, qw + kvw:])], axis=1)
    b_aug = jnp.concatenate([bqkv[:qw] * scale, dup(bqkv[qw:qw + kvw]), dup(bqkv[qw + kvw:])], axis=0)
    splits = ((0, qw), (qw, 2 * kvw), (qw + 2 * kvw, 2 * kvw))
    aq, ak, av = _norm_proj(h2, mix_norm[1], w_aug.astype(BF16), b_aug, splits, (BF16, BF16, BF16),
                            "swa_qkv_proj")
    oa = _swa(aq, ak, av, swa_sinks[0], batch, seq)

    n_experts = moe_router.shape[2]
    h3, hn3, meta, counts = _proj_residual(oa, swa_out_proj[0].astype(BF16), swa_out_bias[0], h2,
                                           ffn_norm[1], F32, "swa_out_proj", router=moe_router[0])
    tm = MOE_TILE
    counts = counts[0, :n_experts].astype(jnp.int32)
    tiles_per_expert = (counts + tm - 1) // tm
    tile_end = jnp.cumsum(tiles_per_expert)
    offsets = (tile_end - tiles_per_expert) * tm
    n_tiles = (TOP_K * n) // tm + n_experts
    tile_ids = jnp.arange(n_tiles, dtype=jnp.int32)
    tile_expert = jnp.sum((tile_ids[:, None] >= tile_end[None, :]).astype(jnp.int32), axis=1)
    tile_expert = jnp.minimum(tile_expert, n_experts - 1)
    n_used = tile_end[-1:].astype(jnp.int32)
    eidx = meta[:, 0:TOP_K].astype(jnp.int32)
    pos = offsets[eidx] + meta[:, 4:4 + TOP_K].astype(jnp.int32)
    pos_flat = pos.reshape(-1)

    pad_start = offsets + counts
    pad_count = tiles_per_expert * tm - counts
    xs = _scatter_rows(hn3, pos_flat, pad_start, pad_count, n_tiles * tm)
    moe_wgu, moe_wd = _swiglu_weights(moe_w_gate[0], moe_w_up[0], moe_w_down[0])
    ys = _swiglu(xs, moe_wgu, moe_wd, tile_expert, n_used, tm, F32, "moe_swiglu")
    out = _combine(h3, meta, final_norm, ys, pos_flat)
    return out.reshape(batch, seq, d)
```

```python
import functools

import jax
import jax.numpy as jnp
from jax import lax
from jax.experimental import pallas as pl
from jax.experimental.pallas import tpu as pltpu

F32 = jnp.float32
BF16 = jnp.bfloat16

NORM_EPS = 1e-5

GLA_HEADS = 4
GLA_GATE_RANK = 16
GLA_GATE_TAU = 16.0
GLA_CHUNK = 64
N_Q_HEADS = 16
N_KV_HEADS = 4
HEAD_DIM = 64
GROUP = N_Q_HEADS // N_KV_HEADS
WINDOW = 128
ATT_BLOCK = 128
TOP_K = 2

LANES = 128
VMEM_LIMIT_BYTES = 56 * 2**20

ROW_TILE = 512
GLA_BLOCK = 512
GLA_GROUP = 256
GLA_HEADS_PER_STEP = 4
FFN_TILE = 1024
FFN_F_TILE = 256
MOE_TILE = 1024
META_WIDTH = 8


def _params(*sem):
    return pltpu.CompilerParams(dimension_semantics=sem, vmem_limit_bytes=VMEM_LIMIT_BYTES)


def _rms(x, gain):
    y = x * lax.rsqrt(jnp.mean(x * x, axis=-1, keepdims=True) + NORM_EPS)
    return y * gain


def _silu(x):
    return x * (1.0 / (1.0 + jnp.exp(-x)))


def _norm_proj_kernel(x_ref, g_ref, w_ref, b_ref, *o_refs, splits):
    y = _rms(x_ref[...], g_ref[...]).astype(BF16)
    for (start, width), o_ref in zip(splits, o_refs):
        acc = jnp.dot(y, w_ref[:, start:start + width], preferred_element_type=F32)
        acc = acc + b_ref[:, start:start + width]
        o_ref[...] = acc.astype(o_ref.dtype)


def _norm_proj(x, gain, w, bias, splits, dtypes, name):
    n, d = x.shape
    nout = w.shape[1]
    tm = ROW_TILE
    assert n % tm == 0
    return pl.pallas_call(
        functools.partial(_norm_proj_kernel, splits=splits),
        grid=(n // tm,),
        in_specs=[
            pl.BlockSpec((tm, d), lambda i: (i, 0)),
            pl.BlockSpec((1, d), lambda i: (0, 0)),
            pl.BlockSpec((d, nout), lambda i: (0, 0)),
            pl.BlockSpec((1, nout), lambda i: (0, 0)),
        ],
        out_specs=[pl.BlockSpec((tm, wd), lambda i: (i, 0)) for (_, wd) in splits],
        out_shape=[jax.ShapeDtypeStruct((n, wd), dt) for (_, wd), dt in zip(splits, dtypes)],
        compiler_params=_params("arbitrary"),
        name=name,
    )(x, gain.reshape(1, d), w, bias.reshape(1, nout))


def _gla_direction(q_ref, k_ref, v_ref, lr_ref, wg_ref, bg_ref, state_ref, reverse):
    heads, dk, dv = state_ref.shape
    rows, width = q_ref.shape
    c = GLA_CHUNK
    nc = rows // c
    q = q_ref[...].astype(F32) * (dk ** -0.5)
    k = k_ref[...].astype(F32)
    v = v_ref[...]
    z = jnp.dot(lr_ref[...].astype(BF16), wg_ref[...], preferred_element_type=F32) + bg_ref[...]
    la = (jnp.minimum(z, 0.0) - jnp.log(1.0 + jnp.exp(-jnp.abs(z)))) * (1.0 / GLA_GATE_TAU)
    la_hi = la.astype(BF16)
    la_lo = (la - la_hi.astype(F32)).astype(BF16)

    tn = (((0,), (0,)), ((), ()))
    nt = (((1,), (1,)), ((), ()))
    grp = GLA_GROUP
    ng = rows // grp
    ri = lax.broadcasted_iota(jnp.int32, (grp, grp), 0)
    ci = lax.broadcasted_iota(jnp.int32, (grp, grp), 1)
    same_chunk = (ri // c) == (ci // c)
    if reverse:
        cum_mask = same_chunk & (ci >= ri)
        att_mask = same_chunk & (ci > ri)
        ref_row, last_row = c // 2 - 1, 0
    else:
        cum_mask = same_chunk & (ci <= ri)
        att_mask = cum_mask
        ref_row, last_row = c // 2, c - 1
    cum = jnp.where(cum_mask, 1.0, 0.0).astype(BF16)

    la_hl = jnp.concatenate([la_hi, la_lo], axis=1)
    b = jnp.concatenate(
        [jnp.dot(cum, la_hl[g * grp:(g + 1) * grp], preferred_element_type=F32) for g in range(ng)],
        axis=0)
    b = b[:, :width] + b[:, width:]
    b3 = b.reshape(nc, c, width)
    b_ref = b3[:, ref_row:ref_row + 1, :]
    b_last = b3[:, last_row:last_row + 1, :]
    q3 = q.reshape(nc, c, width)
    k3 = k.reshape(nc, c, width)
    qe = (q3 * jnp.exp(b3 - b_ref)).astype(BF16).reshape(rows, width)
    ke = (k3 * jnp.exp(b_ref - b3)).astype(BF16).reshape(rows, width)
    kd = (k3 * jnp.exp(b_last - b3)).astype(BF16).reshape(rows, width)
    qb = (q3 * jnp.exp(b3)).astype(BF16).reshape(rows, width)
    decay_rows = jnp.exp(b_last.reshape(nc, width))

    outs = []
    for h in range(heads):
        ks = slice(h * dk, (h + 1) * dk)
        vh = v[:, h * dv:(h + 1) * dv]
        o_intra = []
        for g in range(ng):
            sl = slice(g * grp, (g + 1) * grp)
            s = lax.dot_general(qe[sl, ks], ke[sl, ks], nt, preferred_element_type=F32)
            s = jnp.where(att_mask, s, 0.0).astype(BF16)
            o_intra.append(jnp.dot(s, vh[sl], preferred_element_type=F32))
        o_intra = jnp.concatenate(o_intra, axis=0)

        upd = [lax.dot_general(kd[j * c:(j + 1) * c, ks], vh[j * c:(j + 1) * c], tn,
                               preferred_element_type=F32) for j in range(nc)]
        decay_cols = jnp.concatenate([decay_rows[:, ks], jnp.zeros((dk - nc, dk), F32)], axis=0).T

        state = state_ref[h]
        o_inter = [None] * nc
        for j in (range(nc - 1, -1, -1) if reverse else range(nc)):
            o_inter[j] = jnp.dot(qb[j * c:(j + 1) * c, ks], state.astype(BF16),
                                 preferred_element_type=F32)
            state = state * decay_cols[:, j:j + 1] + upd[j]
        state_ref[h] = state
        outs.append(o_intra + jnp.concatenate(o_inter, axis=0))
    return jnp.concatenate(outs, axis=1)


def _gla_kernel(q_ref, k_ref, v_ref, lr_ref, r_ref, wgf_ref, bgf_ref, wgb_ref, bgb_ref, gain_ref,
                o_ref, state_ref, oacc_ref, *, nblk):
    i = pl.program_id(2)
    rows = q_ref.shape[0]
    heads, _, dv = state_ref.shape

    @pl.when((i == 0) | (i == nblk))
    def _():
        state_ref[...] = jnp.zeros_like(state_ref)

    @pl.when(i < nblk)
    def _():
        o = _gla_direction(q_ref, k_ref, v_ref, lr_ref, wgf_ref, bgf_ref, state_ref, False)
        oacc_ref[pl.ds(pl.multiple_of(i * rows, rows), rows), :] = o

    @pl.when(i >= nblk)
    def _():
        j = 2 * nblk - 1 - i
        o = _gla_direction(q_ref, k_ref, v_ref, lr_ref, wgb_ref, bgb_ref, state_ref, True)
        o = o + oacc_ref[pl.ds(pl.multiple_of(j * rows, rows), rows), :]
        gain = gain_ref[...]
        o = jnp.concatenate([_rms(o[:, h * dv:(h + 1) * dv], gain[:, h * dv:(h + 1) * dv])
                             for h in range(heads)], axis=1)
        o_ref[...] = (o * _silu(r_ref[...].astype(F32))).astype(o_ref.dtype)


def _gla(q, k, v, lr, r, wgf, bgf, wgb, bgb, gain, batch, seq):
    n = q.shape[0]
    heads = GLA_HEADS
    hps = GLA_HEADS_PER_STEP
    dk = q.shape[1] // heads
    dv = v.shape[1] // heads
    rows = GLA_BLOCK
    nblk = seq // rows
    assert seq % rows == 0 and rows % GLA_GROUP == 0 and GLA_GROUP % GLA_CHUNK == 0 and heads % hps == 0

    def blk(i):
        return jnp.where(i < nblk, i, 2 * nblk - 1 - i)

    def late_blk(i):
        return jnp.where(i < nblk, nblk - 1, 2 * nblk - 1 - i)

    row_map = lambda b, h, i: (b * nblk + blk(i), h)
    return pl.pallas_call(
        functools.partial(_gla_kernel, nblk=nblk),
        grid=(batch, heads // hps, 2 * nblk),
        in_specs=[
            pl.BlockSpec((rows, hps * dk), row_map),
            pl.BlockSpec((rows, hps * dk), row_map),
            pl.BlockSpec((rows, hps * dv), row_map),
            pl.BlockSpec((rows, lr.shape[1]), lambda b, h, i: (b * nblk + blk(i), 0)),
            pl.BlockSpec((rows, hps * dv), lambda b, h, i: (b * nblk + late_blk(i), h)),
            pl.BlockSpec((wgf.shape[0], hps * dk), lambda b, h, i: (0, h)),
            pl.BlockSpec((1, hps * dk), lambda b, h, i: (0, h)),
            pl.BlockSpec((wgb.shape[0], hps * dk), lambda b, h, i: (0, h)),
            pl.BlockSpec((1, hps * dk), lambda b, h, i: (0, h)),
            pl.BlockSpec((1, hps * dv), lambda b, h, i: (0, h)),
        ],
        out_specs=pl.BlockSpec((rows, hps * dv), lambda b, h, i: (b * nblk + late_blk(i), h)),
        out_shape=jax.ShapeDtypeStruct((n, heads * dv), BF16),
        scratch_shapes=[pltpu.VMEM((hps, dk, dv), F32), pltpu.VMEM((seq, hps * dv), F32)],
        compiler_params=_params("arbitrary", "arbitrary", "arbitrary"),
        name="gla",
    )(q, k, v, lr, r, wgf, bgf, wgb, bgb, gain)


def _proj_residual_kernel(a_ref, w_ref, b_ref, h_ref, g_ref, h_out_ref, hn_out_ref):
    h = h_ref[...] + (jnp.dot(a_ref[...], w_ref[...], preferred_element_type=F32) + b_ref[...])
    h_out_ref[...] = h
    hn_out_ref[...] = _rms(h, g_ref[...]).astype(hn_out_ref.dtype)


def _proj_residual_router_kernel(a_ref, w_ref, b_ref, h_ref, g_ref, rhl_ref,
                                 h_out_ref, hn_out_ref, meta_ref, count_ref, *, n_experts):
    i = pl.program_id(0)
    h = h_ref[...] + (jnp.dot(a_ref[...], w_ref[...], preferred_element_type=F32) + b_ref[...])
    h_out_ref[...] = h
    hn = _rms(h, g_ref[...])
    hn_out_ref[...] = hn.astype(hn_out_ref.dtype)

    hn_hi = hn.astype(BF16)
    hn_lo = (hn - hn_hi.astype(F32)).astype(BF16)
    hh = jnp.dot(hn_hi, rhl_ref[...], preferred_element_type=F32)
    logits = (hh[:, :LANES] + hh[:, LANES:]
              + jnp.dot(hn_lo, rhl_ref[:, :LANES], preferred_element_type=F32))
    tm = logits.shape[0]
    lane = lax.broadcasted_iota(jnp.int32, logits.shape, 1).astype(F32)
    neg = jnp.float32(-jnp.inf)
    logits = jnp.where(lane < n_experts, logits, neg)
    m1 = jnp.max(logits, axis=-1, keepdims=True)
    i1 = jnp.min(jnp.where(logits == m1, lane, float(LANES)), axis=-1, keepdims=True)
    rest = jnp.where(lane == i1, neg, logits)
    m2 = jnp.max(rest, axis=-1, keepdims=True)
    i2 = jnp.min(jnp.where(rest == m2, lane, float(LANES)), axis=-1, keepdims=True)
    e2 = jnp.exp(m2 - m1)
    w1 = 1.0 / (1.0 + e2)
    w2 = e2 / (1.0 + e2)

    @pl.when(i == 0)
    def _():
        count_ref[...] = jnp.zeros_like(count_ref)

    sel = (lane == i1) | (lane == i2)
    onehot = jnp.where(sel, 1.0, 0.0)
    ri = lax.broadcasted_iota(jnp.int32, (tm, tm), 0)
    ci = lax.broadcasted_iota(jnp.int32, (tm, tm), 1)
    strict_lower = jnp.where(ci < ri, 1.0, 0.0).astype(BF16)
    rank = jnp.dot(strict_lower, onehot.astype(BF16), preferred_element_type=F32) + count_ref[...]
    r1 = jnp.sum(jnp.where(lane == i1, rank, 0.0), axis=-1, keepdims=True)
    r2 = jnp.sum(jnp.where(lane == i2, rank, 0.0), axis=-1, keepdims=True)
    count_ref[...] = count_ref[...] + jnp.sum(onehot, axis=0, keepdims=True)

    meta = jnp.where(lane == 0, i1, 0.0)
    meta = jnp.where(lane == 1, i2, meta)
    meta = jnp.where(lane == 2, w1, meta)
    meta = jnp.where(lane == 3, w2, meta)
    meta = jnp.where(lane == 4, r1, meta)
    meta = jnp.where(lane == 5, r2, meta)
    meta_ref[...] = meta[:, :meta_ref.shape[1]]


def _proj_residual(a, w, bias, h, gain, hn_dtype, name, router=None):
    n, kdim = a.shape
    d = w.shape[1]
    tm = ROW_TILE
    assert n % tm == 0
    in_specs = [
        pl.BlockSpec((tm, kdim), lambda i: (i, 0)),
        pl.BlockSpec((kdim, d), lambda i: (0, 0)),
        pl.BlockSpec((1, d), lambda i: (0, 0)),
        pl.BlockSpec((tm, d), lambda i: (i, 0)),
        pl.BlockSpec((1, d), lambda i: (0, 0)),
    ]
    out_specs = [pl.BlockSpec((tm, d), lambda i: (i, 0)), pl.BlockSpec((tm, d), lambda i: (i, 0))]
    out_shape = [jax.ShapeDtypeStruct((n, d), F32), jax.ShapeDtypeStruct((n, d), hn_dtype)]
    args = [a, w, bias.reshape(1, d), h, gain.reshape(1, d)]
    if router is None:
        body = _proj_residual_kernel
    else:
        n_experts = router.shape[1]
        rpad = jnp.zeros((d, LANES), F32).at[:, :n_experts].set(router)
        rhi = rpad.astype(BF16)
        rlo = (rpad - rhi.astype(F32)).astype(BF16)
        args += [jnp.concatenate([rhi, rlo], axis=1)]
        in_specs += [pl.BlockSpec((d, 2 * LANES), lambda i: (0, 0))]
        out_specs += [pl.BlockSpec((tm, META_WIDTH), lambda i: (i, 0)),
                      pl.BlockSpec((1, LANES), lambda i: (0, 0))]
        out_shape += [jax.ShapeDtypeStruct((n, META_WIDTH), F32), jax.ShapeDtypeStruct((1, LANES), F32)]
        body = functools.partial(_proj_residual_router_kernel, n_experts=n_experts)
    return pl.pallas_call(
        body,
        grid=(n // tm,),
        in_specs=in_specs,
        out_specs=out_specs,
        out_shape=out_shape,
        compiler_params=_params("arbitrary"),
        name=name,
    )(*args)


def _swiglu_kernel(te_ref, nu_ref, x_ref, wgu_ref, wd_ref, *rest, residual):
    if residual:
        h_ref, o_ref, acc_ref = rest
    else:
        o_ref, acc_ref = rest
    i = pl.program_id(0)
    j = pl.program_id(1)
    used = i < nu_ref[0]
    tf = wd_ref.shape[0]

    @pl.when(used & (j == 0))
    def _():
        acc_ref[...] = jnp.zeros_like(acc_ref)

    @pl.when(used)
    def _():
        gu = jnp.dot(x_ref[...].astype(BF16), wgu_ref[...], preferred_element_type=F32)
        a = (_silu(gu[:, :tf]) * gu[:, tf:]).astype(BF16)
        acc_ref[...] += jnp.dot(a, wd_ref[...], preferred_element_type=F32)

    @pl.when(j == pl.num_programs(1) - 1)
    def _():
        @pl.when(used)
        def _():
            if residual:
                o_ref[...] = h_ref[...] + acc_ref[...]
            else:
                o_ref[...] = acc_ref[...].astype(o_ref.dtype)

        @pl.when(jnp.logical_not(used))
        def _():
            o_ref[...] = jnp.zeros_like(o_ref)


def _swiglu_weights(wg, wu, wd):
    e, d, f = wg.shape
    tf = FFN_F_TILE
    assert f % tf == 0
    nf = f // tf
    wgu = jnp.concatenate([wg.astype(BF16).reshape(e, d, nf, tf), wu.astype(BF16).reshape(e, d, nf, tf)],
                          axis=-1)
    return wgu.reshape(e, d, 2 * f), wd.astype(BF16).reshape(e, nf, tf, d)


def _swiglu(x, wgu, wd, tile_expert, n_used, tm, out_dtype, name, residual=None):
    rows, d = x.shape
    nf, tf = wd.shape[1], wd.shape[2]
    assert rows % tm == 0

    def fcol(i, j, nu):
        return jnp.where(i < nu[0], j, nf - 1)

    def xmap(i, j, te, nu):
        return (jnp.where(i < nu[0], i, nu[0] - 1), 0)

    in_specs = [
        pl.BlockSpec((tm, d), xmap),
        pl.BlockSpec((None, d, 2 * tf), lambda i, j, te, nu: (te[i], 0, fcol(i, j, nu))),
        pl.BlockSpec((None, None, tf, d), lambda i, j, te, nu: (te[i], fcol(i, j, nu), 0, 0)),
    ]
    args = [x, wgu, wd]
    if residual is not None:
        in_specs.append(pl.BlockSpec((tm, d), lambda i, j, te, nu: (i, 0)))
        args.append(residual)
    return pl.pallas_call(
        functools.partial(_swiglu_kernel, residual=residual is not None),
        grid_spec=pltpu.PrefetchScalarGridSpec(
            num_scalar_prefetch=2,
            grid=(rows // tm, nf),
            in_specs=in_specs,
            out_specs=pl.BlockSpec((tm, d), lambda i, j, te, nu: (i, 0)),
            scratch_shapes=[pltpu.VMEM((tm, d), F32)],
        ),
        out_shape=jax.ShapeDtypeStruct((rows, d), out_dtype),
        compiler_params=_params("arbitrary", "arbitrary"),
        name=name,
    )(tile_expert, n_used, *args)


def _alibi_slope(head):
    return float(2.0 ** (-8.0 * (head + 1) / N_Q_HEADS))


def _swa_bias(t):
    qi = jnp.arange(t)[:, None]
    kj = jnp.arange(3 * t)[None, :]
    dist = jnp.abs(qi + t - kj)
    band = dist <= WINDOW
    valid = jnp.stack([band & (kj >= t), band, band & (kj < 2 * t)])
    slopes = jnp.asarray([_alibi_slope(h) for h in range(N_Q_HEADS)], F32)
    bias = -slopes[None, :, None, None] * dist.astype(F32)[None, None]
    bias = jnp.where(valid[:, None], bias, -jnp.inf)
    return bias.reshape(3, N_KV_HEADS, GROUP * t, 3 * t)


def _swa_kernel(sink_ref, q_ref, kp_ref, kc_ref, kn_ref, vp_ref, vc_ref, vn_ref, bias0_ref, bias1_ref,
                o_ref):
    t = ATT_BLOCK
    lane = lax.broadcasted_iota(jnp.int32, (t, LANES), 1)
    low = lane < HEAD_DIM
    row_head = lax.broadcasted_iota(jnp.int32, (GROUP * t, 1), 0) // t
    ones = jnp.ones((3 * t, LANES), BF16)

    def window(p_ref, c_ref, n_ref, kvh, sub):
        ks = slice(kvh * LANES, (kvh + 1) * LANES)
        if sub == 0:
            return jnp.concatenate([p_ref[t:, ks], c_ref[:, ks]], axis=0)
        return jnp.concatenate([c_ref[:, ks], n_ref[:t, ks]], axis=0)

    row_blocks = []
    for sub, bias_ref in enumerate((bias0_ref, bias1_ref)):
        rows = slice(sub * t, (sub + 1) * t)
        blocks = []
        for kvh in range(N_KV_HEADS):
            qs = []
            for g in range(GROUP):
                head = kvh * GROUP + g
                col = (head // 2) * LANES
                q2 = q_ref[rows, col:col + LANES]
                qs.append(jnp.where(low if head % 2 == 0 else jnp.logical_not(low), q2,
                                    jnp.zeros_like(q2)))
            qg = jnp.concatenate(qs, axis=0)
            kw = window(kp_ref, kc_ref, kn_ref, kvh, sub)
            s = lax.dot_general(qg, kw, (((1,), (1,)), ((), ())), preferred_element_type=F32)
            s = s + bias_ref[kvh]
            sink = jnp.full((GROUP * t, 1), sink_ref[kvh * GROUP], F32)
            for g in range(1, GROUP):
                sink = jnp.where(row_head == g, sink_ref[kvh * GROUP + g], sink)
            m = jnp.maximum(jnp.max(s, axis=-1, keepdims=True), sink)
            p = jnp.exp(s - m).astype(BF16)
            vw = jnp.concatenate([window(vp_ref, vc_ref, vn_ref, kvh, sub), ones], axis=1)
            res = jnp.dot(p, vw, preferred_element_type=F32)
            out = res[:, :LANES] * (1.0 / (res[:, LANES:] + jnp.exp(sink - m)))
            for pair in range(GROUP // 2):
                a = out[(2 * pair) * t:(2 * pair + 1) * t]
                b = out[(2 * pair + 1) * t:(2 * pair + 2) * t]
                blocks.append(jnp.where(low, a, b).astype(o_ref.dtype))
        row_blocks.append(jnp.concatenate(blocks, axis=1))
    o_ref[...] = jnp.concatenate(row_blocks, axis=0)


def _swa(q, kdup, vdup, sinks, batch, seq):
    n, qw = q.shape
    t = ATT_BLOCK
    rows = 2 * t
    nblk = seq // rows
    assert seq % rows == 0 and nblk >= 2
    kvw = kdup.shape[1]
    bias = _swa_bias(t)
    prev = lambda b, i, s: (b * nblk + jnp.maximum(i - 1, 0), 0)
    cur = lambda b, i, s: (b * nblk + i, 0)
    nxt = lambda b, i, s: (b * nblk + jnp.minimum(i + 1, nblk - 1), 0)
    first = lambda b, i, s: (jnp.where(i == 0, 0, 1), 0, 0, 0)
    second = lambda b, i, s: (jnp.where(i == nblk - 1, 2, 1), 0, 0, 0)
    return pl.pallas_call(
        _swa_kernel,
        grid_spec=pltpu.PrefetchScalarGridSpec(
            num_scalar_prefetch=1,
            grid=(batch, nblk),
            in_specs=[
                pl.BlockSpec((rows, qw), cur),
                pl.BlockSpec((rows, kvw), prev), pl.BlockSpec((rows, kvw), cur), pl.BlockSpec((rows, kvw), nxt),
                pl.BlockSpec((rows, kvw), prev), pl.BlockSpec((rows, kvw), cur), pl.BlockSpec((rows, kvw), nxt),
                pl.BlockSpec((None,) + bias.shape[1:], first),
                pl.BlockSpec((None,) + bias.shape[1:], second),
            ],
            out_specs=pl.BlockSpec((rows, qw), cur),
        ),
        out_shape=jax.ShapeDtypeStruct((n, qw), BF16),
        compiler_params=_params("arbitrary", "arbitrary"),
        name="swa",
    )(sinks, q, kdup, kdup, kdup, vdup, vdup, vdup, bias, bias)


def _scatter_kernel(pos_ref, pad_start_ref, pad_count_ref, x_ref, xs_ref, zero_ref, sem, zero_sem):
    tm = x_ref.shape[0]

    @pl.when(pl.program_id(0) == 0)
    def _():
        zero_ref[...] = jnp.zeros_like(zero_ref)
        for e in range(pad_start_ref.shape[0]):
            def zero_copy(r, e=e):
                dst = pad_start_ref[e] + r
                return pltpu.make_async_copy(zero_ref.at[pl.ds(0, 1)], xs_ref.at[pl.ds(dst, 1)], zero_sem)

            def zero_start(r, carry, zero_copy=zero_copy):
                zero_copy(r).start()
                return carry

            def zero_wait(r, carry, zero_copy=zero_copy):
                zero_copy(r).wait()
                return carry

            lax.fori_loop(0, pad_count_ref[e], zero_start, 0)
            lax.fori_loop(0, pad_count_ref[e], zero_wait, 0)

    def row_copy(t, s):
        dst = pos_ref[TOP_K * t + s]
        return pltpu.make_async_copy(x_ref.at[pl.ds(t, 1)], xs_ref.at[pl.ds(dst, 1)], sem)

    def start(t, carry):
        for s in range(TOP_K):
            row_copy(t, s).start()
        return carry

    def wait(t, carry):
        for s in range(TOP_K):
            row_copy(t, s).wait()
        return carry

    lax.fori_loop(0, tm, start, 0, unroll=8)
    lax.fori_loop(0, tm, wait, 0, unroll=8)


def _scatter_rows(x, pos_flat, pad_start, pad_count, total_rows):
    n, d = x.shape
    tm = ROW_TILE
    return pl.pallas_call(
        _scatter_kernel,
        grid=(n // tm,),
        in_specs=[
            pl.BlockSpec((TOP_K * tm,), lambda i: (i,), memory_space=pltpu.SMEM),
            pl.BlockSpec(memory_space=pltpu.SMEM),
            pl.BlockSpec(memory_space=pltpu.SMEM),
            pl.BlockSpec((tm, d), lambda i: (i, 0)),
        ],
        out_specs=pl.BlockSpec(memory_space=pl.ANY),
        out_shape=jax.ShapeDtypeStruct((total_rows, d), x.dtype),
        scratch_shapes=[pltpu.VMEM((8, d), x.dtype), pltpu.SemaphoreType.DMA, pltpu.SemaphoreType.DMA],
        compiler_params=_params("arbitrary"),
        name="moe_scatter",
    )(pos_flat, pad_start, pad_count, x)


def _combine_kernel(pos_ref, h_ref, meta_ref, g_ref, ys_ref, o_ref, y_ref, sem):
    tm = h_ref.shape[0]

    def row_copy(t, s):
        src = pos_ref[TOP_K * t + s]
        return pltpu.make_async_copy(ys_ref.at[pl.ds(src, 1)], y_ref.at[s, pl.ds(t, 1)], sem)

    def start(t, carry):
        for s in range(TOP_K):
            row_copy(t, s).start()
        return carry

    def wait(t, carry):
        for s in range(TOP_K):
            row_copy(t, s).wait()
        return carry

    lax.fori_loop(0, tm, start, 0, unroll=8)
    lax.fori_loop(0, tm, wait, 0, unroll=8)
    meta = meta_ref[...]
    w1 = meta[:, 2:3]
    w2 = meta[:, 3:4]
    moe = w1 * y_ref[0].astype(F32) + w2 * y_ref[1].astype(F32)
    o_ref[...] = _rms(h_ref[...] + moe, g_ref[...])


def _combine(h, meta, gain, ys, pos_flat):
    n, d = h.shape
    tm = ROW_TILE
    return pl.pallas_call(
        _combine_kernel,
        grid=(n // tm,),
        in_specs=[
            pl.BlockSpec((TOP_K * tm,), lambda i: (i,), memory_space=pltpu.SMEM),
            pl.BlockSpec((tm, d), lambda i: (i, 0)),
            pl.BlockSpec((tm, META_WIDTH), lambda i: (i, 0)),
            pl.BlockSpec((1, d), lambda i: (0, 0)),
            pl.BlockSpec(memory_space=pl.ANY),
        ],
        out_specs=pl.BlockSpec((tm, d), lambda i: (i, 0)),
        out_shape=jax.ShapeDtypeStruct((n, d), F32),
        scratch_shapes=[pltpu.VMEM((TOP_K, tm, d), ys.dtype), pltpu.SemaphoreType.DMA],
        compiler_params=_params("arbitrary"),
        name="moe_combine",
    )(pos_flat, h, meta, gain.reshape(1, d), ys)


def kernel(x, mix_norm, ffn_norm, gla_in_proj, gla_gate_w_fwd, gla_gate_b_fwd, gla_gate_w_bwd, gla_gate_b_bwd, gla_head_norm, gla_out_proj, swa_qkv_proj, swa_qkv_bias, swa_sinks, swa_out_proj, swa_out_bias, dense_w_gate, dense_w_up, dense_w_down, moe_router, moe_w_gate, moe_w_up, moe_w_down, final_norm):
    batch, seq, d = x.shape
    n = batch * seq
    h0 = x.reshape(n, d)
    zeros_d = jnp.zeros((d,), F32)

    key_w = gla_gate_w_fwd.shape[2]
    val_w = gla_head_norm.shape[1]
    rank = GLA_GATE_RANK
    in_w = gla_in_proj.shape[2]
    splits = ((0, key_w), (key_w, key_w), (2 * key_w, val_w), (2 * key_w + val_w, val_w),
              (2 * key_w + 2 * val_w, 2 * rank))
    q, k, v, r, lr = _norm_proj(h0, mix_norm[0], gla_in_proj[0].astype(BF16), jnp.zeros((in_w,), F32),
                                splits, (BF16, BF16, BF16, BF16, F32), "gla_in_proj")
    zero_gate = jnp.zeros((rank, key_w), F32)
    wgf = jnp.concatenate([gla_gate_w_fwd[0], zero_gate], axis=0).astype(BF16)
    wgb = jnp.concatenate([zero_gate, gla_gate_w_bwd[0]], axis=0).astype(BF16)
    og = _gla(q, k, v, lr, r, wgf, gla_gate_b_fwd[0].reshape(1, key_w), wgb,
              gla_gate_b_bwd[0].reshape(1, key_w), gla_head_norm[0].reshape(1, val_w), batch, seq)
    h1, hn1 = _proj_residual(og, gla_out_proj[0].astype(BF16), zeros_d, h0, ffn_norm[0], BF16,
                             "gla_out_proj")

    n_ffn_tiles = n // FFN_TILE
    dense_wgu, dense_wd = _swiglu_weights(dense_w_gate, dense_w_up, dense_w_down)
    h2 = _swiglu(hn1, dense_wgu, dense_wd, jnp.zeros((n_ffn_tiles,), jnp.int32),
                 jnp.full((1,), n_ffn_tiles, jnp.int32), FFN_TILE, F32, "dense_swiglu", residual=h1)

    qw = N_Q_HEADS * HEAD_DIM
    kvw = N_KV_HEADS * HEAD_DIM
    scale = HEAD_DIM ** -0.5
    wqkv, bqkv = swa_qkv_proj[0], swa_qkv_bias[0]

    def dup(m):
        lead = m.shape[:-1]
        m = m.reshape(lead + (N_KV_HEADS, 1, HEAD_DIM))
        return jnp.broadcast_to(m, lead + (N_KV_HEADS, 2, HEAD_DIM)).reshape(lead + (2 * kvw,))

    w_aug = jnp.concatenate([wqkv[:, :qw] * scale, dup(wqkv[:, qw:qw + kvw]), dup(wqkv[:, qw + kvw:])], axis=1)
    b_aug = jnp.concatenate([bqkv[:qw] * scale, dup(bqkv[qw:qw + kvw]), dup(bqkv[qw + kvw:])], axis=0)
    splits = ((0, qw), (qw, 2 * kvw), (qw + 2 * kvw, 2 * kvw))
    aq, ak, av = _norm_proj(h2, mix_norm[1], w_aug.astype(BF16), b_aug, splits, (BF16, BF16, BF16),
                            "swa_qkv_proj")
    oa = _swa(aq, ak, av, swa_sinks[0], batch, seq)

    n_experts = moe_router.shape[2]
    h3, hn3, meta, counts = _proj_residual(oa, swa_out_proj[0].astype(BF16), swa_out_bias[0], h2,
                                           ffn_norm[1], F32, "swa_out_proj", router=moe_router[0])
    tm = MOE_TILE
    counts = counts[0, :n_experts].astype(jnp.int32)
    tiles_per_expert = (counts + tm - 1) // tm
    tile_end = jnp.cumsum(tiles_per_expert)
    offsets = (tile_end - tiles_per_expert) * tm
    n_tiles = (TOP_K * n) // tm + n_experts
    tile_ids = jnp.arange(n_tiles, dtype=jnp.int32)
    tile_expert = jnp.sum((tile_ids[:, None] >= tile_end[None, :]).astype(jnp.int32), axis=1)
    tile_expert = jnp.minimum(tile_expert, n_experts - 1)
    n_used = tile_end[-1:].astype(jnp.int32)
    eidx = meta[:, 0:TOP_K].astype(jnp.int32)
    pos = offsets[eidx] + meta[:, 4:4 + TOP_K].astype(jnp.int32)
    pos_flat = pos.reshape(-1)

    pad_start = offsets + counts
    pad_count = tiles_per_expert * tm - counts
    xs = _scatter_rows(hn3, pos_flat, pad_start, pad_count, n_tiles * tm)
    moe_wgu, moe_wd = _swiglu_weights(moe_w_gate[0], moe_w_up[0], moe_w_down[0])
    ys = _swiglu(xs, moe_wgu, moe_wd, tile_expert, n_used, tm, F32, "moe_swiglu")
    out = _combine(h3, meta, final_norm, ys, pos_flat)
    return out.reshape(batch, seq, d)
```

```python
import functools

import jax
import jax.numpy as jnp
from jax import lax
from jax.experimental import pallas as pl
from jax.experimental.pallas import tpu as pltpu

F32 = jnp.float32
BF16 = jnp.bfloat16

NORM_EPS = 1e-5

GLA_HEADS = 4
GLA_GATE_RANK = 16
GLA_GATE_TAU = 16.0
GLA_CHUNK = 64
N_Q_HEADS = 16
N_KV_HEADS = 4
HEAD_DIM = 64
GROUP = N_Q_HEADS // N_KV_HEADS
WINDOW = 128
ATT_BLOCK = 128
TOP_K = 2

LANES = 128
VMEM_LIMIT_BYTES = 56 * 2**20

ROW_TILE = 512
GLA_BLOCK = 512
GLA_GROUP = 256
GLA_HEADS_PER_STEP = 4
FFN_TILE = 1024
FFN_F_TILE = 256
MOE_TILE = 1024
META_WIDTH = 8


def _params(*sem):
    return pltpu.CompilerParams(dimension_semantics=sem, vmem_limit_bytes=VMEM_LIMIT_BYTES)


def _rms(x, gain):
    y = x * lax.rsqrt(jnp.mean(x * x, axis=-1, keepdims=True) + NORM_EPS)
    return y * gain


def _silu(x):
    return x * (1.0 / (1.0 + jnp.exp(-x)))


def _norm_proj_kernel(x_ref, g_ref, w_ref, b_ref, *o_refs, splits):
    y = _rms(x_ref[...], g_ref[...]).astype(BF16)
    for (start, width), o_ref in zip(splits, o_refs):
        acc = jnp.dot(y, w_ref[:, start:start + width], preferred_element_type=F32)
        acc = acc + b_ref[:, start:start + width]
        o_ref[...] = acc.astype(o_ref.dtype)


def _norm_proj(x, gain, w, bias, splits, dtypes, name):
    n, d = x.shape
    nout = w.shape[1]
    tm = ROW_TILE
    assert n % tm == 0
    return pl.pallas_call(
        functools.partial(_norm_proj_kernel, splits=splits),
        grid=(n // tm,),
        in_specs=[
            pl.BlockSpec((tm, d), lambda i: (i, 0)),
            pl.BlockSpec((1, d), lambda i: (0, 0)),
            pl.BlockSpec((d, nout), lambda i: (0, 0)),
            pl.BlockSpec((1, nout), lambda i: (0, 0)),
        ],
        out_specs=[pl.BlockSpec((tm, wd), lambda i: (i, 0)) for (_, wd) in splits],
        out_shape=[jax.ShapeDtypeStruct((n, wd), dt) for (_, wd), dt in zip(splits, dtypes)],
        compiler_params=_params("arbitrary"),
        name=name,
    )(x, gain.reshape(1, d), w, bias.reshape(1, nout))


def _gla_direction(q_ref, k_ref, v_ref, lr_ref, wg_ref, bg_ref, state_ref, reverse):
    heads, dk, dv = state_ref.shape
    rows, width = q_ref.shape
    c = GLA_CHUNK
    nc = rows // c
    q = q_ref[...].astype(F32) * (dk ** -0.5)
    k = k_ref[...].astype(F32)
    v = v_ref[...]
    z = jnp.dot(lr_ref[...].astype(BF16), wg_ref[...], preferred_element_type=F32) + bg_ref[...]
    la = (jnp.minimum(z, 0.0) - jnp.log(1.0 + jnp.exp(-jnp.abs(z)))) * (1.0 / GLA_GATE_TAU)
    la_hi = la.astype(BF16)
    la_lo = (la - la_hi.astype(F32)).astype(BF16)

    tn = (((0,), (0,)), ((), ()))
    nt = (((1,), (1,)), ((), ()))
    grp = GLA_GROUP
    ng = rows // grp
    ri = lax.broadcasted_iota(jnp.int32, (grp, grp), 0)
    ci = lax.broadcasted_iota(jnp.int32, (grp, grp), 1)
    same_chunk = (ri // c) == (ci // c)
    if reverse:
        cum_mask = same_chunk & (ci >= ri)
        att_mask = same_chunk & (ci > ri)
        ref_row, last_row = c // 2 - 1, 0
    else:
        cum_mask = same_chunk & (ci <= ri)
        att_mask = cum_mask
        ref_row, last_row = c // 2, c - 1
    cum = jnp.where(cum_mask, 1.0, 0.0).astype(BF16)

    la_hl = jnp.concatenate([la_hi, la_lo], axis=1)
    b = jnp.concatenate(
        [jnp.dot(cum, la_hl[g * grp:(g + 1) * grp], preferred_element_type=F32) for g in range(ng)],
        axis=0)
    b = b[:, :width] + b[:, width:]
    b3 = b.reshape(nc, c, width)
    b_ref = b3[:, ref_row:ref_row + 1, :]
    b_last = b3[:, last_row:last_row + 1, :]
    q3 = q.reshape(nc, c, width)
    k3 = k.reshape(nc, c, width)
    qe = (q3 * jnp.exp(b3 - b_ref)).astype(BF16).reshape(rows, width)
    ke = (k3 * jnp.exp(b_ref - b3)).astype(BF16).reshape(rows, width)
    kd = (k3 * jnp.exp(b_last - b3)).astype(BF16).reshape(rows, width)
    qb = (q3 * jnp.exp(b3)).astype(BF16).reshape(rows, width)
    decay_rows = jnp.exp(b_last.reshape(nc, width))

    outs = []
    for h in range(heads):
        ks = slice(h * dk, (h + 1) * dk)
        vh = v[:, h * dv:(h + 1) * dv]
        o_intra = []
        for g in range(ng):
            sl = slice(g * grp, (g + 1) * grp)
            s = lax.dot_general(qe[sl, ks], ke[sl, ks], nt, preferred_element_type=F32)
            s = jnp.where(att_mask, s, 0.0).astype(BF16)
            o_intra.append(jnp.dot(s, vh[sl], preferred_element_type=F32))
        o_intra = jnp.concatenate(o_intra, axis=0)

        upd = [lax.dot_general(kd[j * c:(j + 1) * c, ks], vh[j * c:(j + 1) * c], tn,
                               preferred_element_type=F32) for j in range(nc)]
        decay_cols = jnp.concatenate([decay_rows[:, ks], jnp.zeros((dk - nc, dk), F32)], axis=0).T

        state = state_ref[h]
        o_inter = [None] * nc
        for j in (range(nc - 1, -1, -1) if reverse else range(nc)):
            o_inter[j] = jnp.dot(qb[j * c:(j + 1) * c, ks], state.astype(BF16),
                                 preferred_element_type=F32)
            state = state * decay_cols[:, j:j + 1] + upd[j]
        state_ref[h] = state
        outs.append(o_intra + jnp.concatenate(o_inter, axis=0))
    return jnp.concatenate(outs, axis=1)


def _gla_kernel(q_ref, k_ref, v_ref, lr_ref, r_ref, wgf_ref, bgf_ref, wgb_ref, bgb_ref, gain_ref,
                o_ref, state_ref, oacc_ref, *, nblk):
    i = pl.program_id(2)
    rows = q_ref.shape[0]
    heads, _, dv = state_ref.shape

    @pl.when((i == 0) | (i == nblk))
    def _():
        state_ref[...] = jnp.zeros_like(state_ref)

    @pl.when(i < nblk)
    def _():
        o = _gla_direction(q_ref, k_ref, v_ref, lr_ref, wgf_ref, bgf_ref, state_ref, False)
        oacc_ref[pl.ds(pl.multiple_of(i * rows, rows), rows), :] = o

    @pl.when(i >= nblk)
    def _():
        j = 2 * nblk - 1 - i
        o = _gla_direction(q_ref, k_ref, v_ref, lr_ref, wgb_ref, bgb_ref, state_ref, True)
        o = o + oacc_ref[pl.ds(pl.multiple_of(j * rows, rows), rows), :]
        gain = gain_ref[...]
        o = jnp.concatenate([_rms(o[:, h * dv:(h + 1) * dv], gain[:, h * dv:(h + 1) * dv])
                             for h in range(heads)], axis=1)
        o_ref[...] = (o * _silu(r_ref[...].astype(F32))).astype(o_ref.dtype)


def _gla(q, k, v, lr, r, wgf, bgf, wgb, bgb, gain, batch, seq):
    n = q.shape[0]
    heads = GLA_HEADS
    hps = GLA_HEADS_PER_STEP
    dk = q.shape[1] // heads
    dv = v.shape[1] // heads
    rows = GLA_BLOCK
    nblk = seq // rows
    assert seq % rows == 0 and rows % GLA_GROUP == 0 and GLA_GROUP % GLA_CHUNK == 0 and heads % hps == 0

    def blk(i):
        return jnp.where(i < nblk, i, 2 * nblk - 1 - i)

    def late_blk(i):
        return jnp.where(i < nblk, nblk - 1, 2 * nblk - 1 - i)

    row_map = lambda b, h, i: (b * nblk + blk(i), h)
    return pl.pallas_call(
        functools.partial(_gla_kernel, nblk=nblk),
        grid=(batch, heads // hps, 2 * nblk),
        in_specs=[
            pl.BlockSpec((rows, hps * dk), row_map),
            pl.BlockSpec((rows, hps * dk), row_map),
            pl.BlockSpec((rows, hps * dv), row_map),
            pl.BlockSpec((rows, lr.shape[1]), lambda b, h, i: (b * nblk + blk(i), 0)),
            pl.BlockSpec((rows, hps * dv), lambda b, h, i: (b * nblk + late_blk(i), h)),
            pl.BlockSpec((wgf.shape[0], hps * dk), lambda b, h, i: (0, h)),
            pl.BlockSpec((1, hps * dk), lambda b, h, i: (0, h)),
            pl.BlockSpec((wgb.shape[0], hps * dk), lambda b, h, i: (0, h)),
            pl.BlockSpec((1, hps * dk), lambda b, h, i: (0, h)),
            pl.BlockSpec((1, hps * dv), lambda b, h, i: (0, h)),
        ],
        out_specs=pl.BlockSpec((rows, hps * dv), lambda b, h, i: (b * nblk + late_blk(i), h)),
        out_shape=jax.ShapeDtypeStruct((n, heads * dv), BF16),
        scratch_shapes=[pltpu.VMEM((hps, dk, dv), F32), pltpu.VMEM((seq, hps * dv), F32)],
        compiler_params=_params("arbitrary", "arbitrary", "arbitrary"),
        name="gla",
    )(q, k, v, lr, r, wgf, bgf, wgb, bgb, gain)


def _proj_residual_router_kernel(a_ref, w_ref, b_ref, h_ref, g_ref, rhl_ref,
                                 h_out_ref, hn_out_ref, meta_ref, count_ref, *, n_experts):
    i = pl.program_id(0)
    h = h_ref[...] + (jnp.dot(a_ref[...], w_ref[...], preferred_element_type=F32) + b_ref[...])
    h_out_ref[...] = h
    hn = _rms(h, g_ref[...])
    hn_out_ref[...] = hn.astype(hn_out_ref.dtype)

    hn_hi = hn.astype(BF16)
    hn_lo = (hn - hn_hi.astype(F32)).astype(BF16)
    hh = jnp.dot(hn_hi, rhl_ref[...], preferred_element_type=F32)
    logits = (hh[:, :LANES] + hh[:, LANES:]
              + jnp.dot(hn_lo, rhl_ref[:, :LANES], preferred_element_type=F32))
    tm = logits.shape[0]
    lane = lax.broadcasted_iota(jnp.int32, logits.shape, 1).astype(F32)
    neg = jnp.float32(-jnp.inf)
    logits = jnp.where(lane < n_experts, logits, neg)
    m1 = jnp.max(logits, axis=-1, keepdims=True)
    i1 = jnp.min(jnp.where(logits == m1, lane, float(LANES)), axis=-1, keepdims=True)
    rest = jnp.where(lane == i1, neg, logits)
    m2 = jnp.max(rest, axis=-1, keepdims=True)
    i2 = jnp.min(jnp.where(rest == m2, lane, float(LANES)), axis=-1, keepdims=True)
    e2 = jnp.exp(m2 - m1)
    w1 = 1.0 / (1.0 + e2)
    w2 = e2 / (1.0 + e2)

    @pl.when(i == 0)
    def _():
        count_ref[...] = jnp.zeros_like(count_ref)

    sel = (lane == i1) | (lane == i2)
    onehot = jnp.where(sel, 1.0, 0.0)
    ri = lax.broadcasted_iota(jnp.int32, (tm, tm), 0)
    ci = lax.broadcasted_iota(jnp.int32, (tm, tm), 1)
    strict_lower = jnp.where(ci < ri, 1.0, 0.0).astype(BF16)
    rank = jnp.dot(strict_lower, onehot.astype(BF16), preferred_element_type=F32) + count_ref[...]
    r1 = jnp.sum(jnp.where(lane == i1, rank, 0.0), axis=-1, keepdims=True)
    r2 = jnp.sum(jnp.where(lane == i2, rank, 0.0), axis=-1, keepdims=True)
    count_ref[...] = count_ref[...] + jnp.sum(onehot, axis=0, keepdims=True)

    meta = jnp.where(lane == 0, i1, 0.0)
    meta = jnp.where(lane == 1, i2, meta)
    meta = jnp.where(lane == 2, w1, meta)
    meta = jnp.where(lane == 3, w2, meta)
    meta = jnp.where(lane == 4, r1, meta)
    meta = jnp.where(lane == 5, r2, meta)
    meta_ref[...] = meta[:, :meta_ref.shape[1]]


def _proj_residual_router(a, w, bias, h, gain, router, name):
    n, kdim = a.shape
    d = w.shape[1]
    tm = ROW_TILE
    assert n % tm == 0
    n_experts = router.shape[1]
    rpad = jnp.zeros((d, LANES), F32).at[:, :n_experts].set(router)
    rhi = rpad.astype(BF16)
    rlo = (rpad - rhi.astype(F32)).astype(BF16)
    row = lambda i: (i, 0)
    fixed = lambda i: (0, 0)
    return pl.pallas_call(
        functools.partial(_proj_residual_router_kernel, n_experts=n_experts),
        grid=(n // tm,),
        in_specs=[
            pl.BlockSpec((tm, kdim), row),
            pl.BlockSpec((kdim, d), fixed),
            pl.BlockSpec((1, d), fixed),
            pl.BlockSpec((tm, d), row),
            pl.BlockSpec((1, d), fixed),
            pl.BlockSpec((d, 2 * LANES), fixed),
        ],
        out_specs=[pl.BlockSpec((tm, d), row), pl.BlockSpec((tm, d), row),
                   pl.BlockSpec((tm, META_WIDTH), row), pl.BlockSpec((1, LANES), fixed)],
        out_shape=[jax.ShapeDtypeStruct((n, d), F32), jax.ShapeDtypeStruct((n, d), F32),
                   jax.ShapeDtypeStruct((n, META_WIDTH), F32), jax.ShapeDtypeStruct((1, LANES), F32)],
        compiler_params=_params("arbitrary"),
        name=name,
    )(a, w, bias.reshape(1, d), h, gain.reshape(1, d), jnp.concatenate([rhi, rlo], axis=1))


def _swiglu_step(x, wgu_ref, wd_ref, acc_ref):
    tf = wd_ref.shape[0]
    gu = jnp.dot(x, wgu_ref[...], preferred_element_type=F32)
    a = (_silu(gu[:, :tf]) * gu[:, tf:]).astype(BF16)
    acc_ref[...] += jnp.dot(a, wd_ref[...], preferred_element_type=F32)


def _moe_swiglu_kernel(te_ref, nu_ref, x_ref, wgu_ref, wd_ref, o_ref, acc_ref):
    i = pl.program_id(0)
    j = pl.program_id(1)
    used = i < nu_ref[0]

    @pl.when(used & (j == 0))
    def _():
        acc_ref[...] = jnp.zeros_like(acc_ref)

    @pl.when(used)
    def _():
        _swiglu_step(x_ref[...].astype(BF16), wgu_ref, wd_ref, acc_ref)

    @pl.when(j == pl.num_programs(1) - 1)
    def _():
        @pl.when(used)
        def _():
            o_ref[...] = acc_ref[...].astype(o_ref.dtype)

        @pl.when(jnp.logical_not(used))
        def _():
            o_ref[...] = jnp.zeros_like(o_ref)


def _swiglu_weights(wg, wu, wd):
    e, d, f = wg.shape
    tf = FFN_F_TILE
    assert f % tf == 0
    nf = f // tf
    tiles = []
    for j in range(nf):
        tiles += [wg[:, :, j * tf:(j + 1) * tf], wu[:, :, j * tf:(j + 1) * tf]]
    wgu = jnp.concatenate(tiles, axis=-1).astype(BF16)
    return wgu, wd.astype(BF16).reshape(e, nf, tf, d)


def _moe_swiglu(x, wgu, wd, tile_expert, n_used, tm, out_dtype):
    rows, d = x.shape
    nf, tf = wd.shape[1], wd.shape[2]
    assert rows % tm == 0

    def fcol(i, j, nu):
        return jnp.where(i < nu[0], j, nf - 1)

    def xmap(i, j, te, nu):
        return (jnp.where(i < nu[0], i, nu[0] - 1), 0)

    return pl.pallas_call(
        _moe_swiglu_kernel,
        grid_spec=pltpu.PrefetchScalarGridSpec(
            num_scalar_prefetch=2,
            grid=(rows // tm, nf),
            in_specs=[
                pl.BlockSpec((tm, d), xmap),
                pl.BlockSpec((None, d, 2 * tf), lambda i, j, te, nu: (te[i], 0, fcol(i, j, nu))),
                pl.BlockSpec((None, None, tf, d), lambda i, j, te, nu: (te[i], fcol(i, j, nu), 0, 0)),
            ],
            out_specs=pl.BlockSpec((tm, d), lambda i, j, te, nu: (i, 0)),
            scratch_shapes=[pltpu.VMEM((tm, d), F32)],
        ),
        out_shape=jax.ShapeDtypeStruct((rows, d), out_dtype),
        compiler_params=_params("arbitrary", "arbitrary"),
        name="moe_swiglu",
    )(tile_expert, n_used, x, wgu, wd)


def _proj_swiglu_kernel(a_ref, wo_ref, h_ref, g_ref, wgu_ref, wd_ref, o_ref, hres_ref, xn_ref, acc_ref):
    j = pl.program_id(1)

    @pl.when(j == 0)
    def _():
        h = h_ref[...] + jnp.dot(a_ref[...], wo_ref[...], preferred_element_type=F32)
        hres_ref[...] = h
        xn_ref[...] = _rms(h, g_ref[...]).astype(xn_ref.dtype)
        acc_ref[...] = jnp.zeros_like(acc_ref)

    _swiglu_step(xn_ref[...], wgu_ref, wd_ref, acc_ref)

    @pl.when(j == pl.num_programs(1) - 1)
    def _():
        o_ref[...] = hres_ref[...] + acc_ref[...]


def _proj_swiglu(a, wo, h, gain, wgu, wd):
    n, kdim = a.shape
    d = wo.shape[1]
    nf, tf = wd.shape[1], wd.shape[2]
    tm = FFN_TILE
    assert n % tm == 0
    return pl.pallas_call(
        _proj_swiglu_kernel,
        grid=(n // tm, nf),
        in_specs=[
            pl.BlockSpec((tm, kdim), lambda i, j: (i, 0)),
            pl.BlockSpec((kdim, d), lambda i, j: (0, 0)),
            pl.BlockSpec((tm, d), lambda i, j: (i, 0)),
            pl.BlockSpec((1, d), lambda i, j: (0, 0)),
            pl.BlockSpec((None, d, 2 * tf), lambda i, j: (0, 0, j)),
            pl.BlockSpec((None, None, tf, d), lambda i, j: (0, j, 0, 0)),
        ],
        out_specs=pl.BlockSpec((tm, d), lambda i, j: (i, 0)),
        out_shape=jax.ShapeDtypeStruct((n, d), F32),
        scratch_shapes=[pltpu.VMEM((tm, d), F32), pltpu.VMEM((tm, d), BF16), pltpu.VMEM((tm, d), F32)],
        compiler_params=_params("arbitrary", "arbitrary"),
        name="gla_out_proj_dense_swiglu",
    )(a, wo, h, gain.reshape(1, d), wgu, wd)


def _alibi_slope(head):
    return float(2.0 ** (-8.0 * (head + 1) / N_Q_HEADS))


def _swa_bias(t):
    qi = jnp.arange(t)[:, None]
    kj = jnp.arange(3 * t)[None, :]
    dist = jnp.abs(qi + t - kj)
    band = dist <= WINDOW
    valid = jnp.stack([band & (kj >= t), band, band & (kj < 2 * t)])
    slopes = jnp.asarray([_alibi_slope(h) for h in range(N_Q_HEADS)], F32)
    bias = -slopes[None, :, None, None] * dist.astype(F32)[None, None]
    bias = jnp.where(valid[:, None], bias, -jnp.inf)
    return bias.reshape(3, N_KV_HEADS, GROUP * t, 3 * t)


def _swa_kernel(sink_ref, q_ref, kp_ref, kc_ref, kn_ref, vp_ref, vc_ref, vn_ref, bias0_ref, bias1_ref,
                o_ref):
    t = ATT_BLOCK
    lane = lax.broadcasted_iota(jnp.int32, (t, LANES), 1)
    low = lane < HEAD_DIM
    row_head = lax.broadcasted_iota(jnp.int32, (GROUP * t, 1), 0) // t
    ones = jnp.ones((3 * t, LANES), BF16)

    def window(p_ref, c_ref, n_ref, kvh, sub):
        ks = slice(kvh * LANES, (kvh + 1) * LANES)
        if sub == 0:
            return jnp.concatenate([p_ref[t:, ks], c_ref[:, ks]], axis=0)
        return jnp.concatenate([c_ref[:, ks], n_ref[:t, ks]], axis=0)

    row_blocks = []
    for sub, bias_ref in enumerate((bias0_ref, bias1_ref)):
        rows = slice(sub * t, (sub + 1) * t)
        blocks = []
        for kvh in range(N_KV_HEADS):
            qs = []
            for g in range(GROUP):
                head = kvh * GROUP + g
                col = (head // 2) * LANES
                q2 = q_ref[rows, col:col + LANES]
                qs.append(jnp.where(low if head % 2 == 0 else jnp.logical_not(low), q2,
                                    jnp.zeros_like(q2)))
            qg = jnp.concatenate(qs, axis=0)
            kw = window(kp_ref, kc_ref, kn_ref, kvh, sub)
            s = lax.dot_general(qg, kw, (((1,), (1,)), ((), ())), preferred_element_type=F32)
            s = s + bias_ref[kvh]
            sink = jnp.full((GROUP * t, 1), sink_ref[kvh * GROUP], F32)
            for g in range(1, GROUP):
                sink = jnp.where(row_head == g, sink_ref[kvh * GROUP + g], sink)
            m = jnp.maximum(jnp.max(s, axis=-1, keepdims=True), sink)
            p = jnp.exp(s - m).astype(BF16)
            vw = jnp.concatenate([window(vp_ref, vc_ref, vn_ref, kvh, sub), ones], axis=1)
            res = jnp.dot(p, vw, preferred_element_type=F32)
            out = res[:, :LANES] * (1.0 / (res[:, LANES:] + jnp.exp(sink - m)))
            for pair in range(GROUP // 2):
                a = out[(2 * pair) * t:(2 * pair + 1) * t]
                b = out[(2 * pair + 1) * t:(2 * pair + 2) * t]
                blocks.append(jnp.where(low, a, b).astype(o_ref.dtype))
        row_blocks.append(jnp.concatenate(blocks, axis=1))
    o_ref[...] = jnp.concatenate(row_blocks, axis=0)


def _swa(q, kdup, vdup, sinks, batch, seq):
    n, qw = q.shape
    t = ATT_BLOCK
    rows = 2 * t
    nblk = seq // rows
    assert seq % rows == 0 and nblk >= 2
    kvw = kdup.shape[1]
    bias = _swa_bias(t)
    prev = lambda b, i, s: (b * nblk + jnp.maximum(i - 1, 0), 0)
    cur = lambda b, i, s: (b * nblk + i, 0)
    nxt = lambda b, i, s: (b * nblk + jnp.minimum(i + 1, nblk - 1), 0)
    first = lambda b, i, s: (jnp.where(i == 0, 0, 1), 0, 0, 0)
    second = lambda b, i, s: (jnp.where(i == nblk - 1, 2, 1), 0, 0, 0)
    return pl.pallas_call(
        _swa_kernel,
        grid_spec=pltpu.PrefetchScalarGridSpec(
            num_scalar_prefetch=1,
            grid=(batch, nblk),
            in_specs=[
                pl.BlockSpec((rows, qw), cur),
                pl.BlockSpec((rows, kvw), prev), pl.BlockSpec((rows, kvw), cur), pl.BlockSpec((rows, kvw), nxt),
                pl.BlockSpec((rows, kvw), prev), pl.BlockSpec((rows, kvw), cur), pl.BlockSpec((rows, kvw), nxt),
                pl.BlockSpec((None,) + bias.shape[1:], first),
                pl.BlockSpec((None,) + bias.shape[1:], second),
            ],
            out_specs=pl.BlockSpec((rows, qw), cur),
        ),
        out_shape=jax.ShapeDtypeStruct((n, qw), BF16),
        compiler_params=_params("arbitrary", "arbitrary"),
        name="swa",
    )(sinks, q, kdup, kdup, kdup, vdup, vdup, vdup, bias, bias)


def _scatter_kernel(pos_ref, pad_start_ref, pad_count_ref, x_ref, xs_ref, zero_ref, sem, zero_sem):
    tm = x_ref.shape[0]

    @pl.when(pl.program_id(0) == 0)
    def _():
        zero_ref[...] = jnp.zeros_like(zero_ref)
        for e in range(pad_start_ref.shape[0]):
            def zero_copy(r, e=e):
                dst = pad_start_ref[e] + r
                return pltpu.make_async_copy(zero_ref.at[pl.ds(0, 1)], xs_ref.at[pl.ds(dst, 1)], zero_sem)

            def zero_start(r, carry, zero_copy=zero_copy):
                zero_copy(r).start()
                return carry

            def zero_wait(r, carry, zero_copy=zero_copy):
                zero_copy(r).wait()
                return carry

            lax.fori_loop(0, pad_count_ref[e], zero_start, 0)
            lax.fori_loop(0, pad_count_ref[e], zero_wait, 0)

    def row_copy(t, s):
        dst = pos_ref[TOP_K * t + s]
        return pltpu.make_async_copy(x_ref.at[pl.ds(t, 1)], xs_ref.at[pl.ds(dst, 1)], sem)

    def start(t, carry):
        for s in range(TOP_K):
            row_copy(t, s).start()
        return carry

    def wait(t, carry):
        for s in range(TOP_K):
            row_copy(t, s).wait()
        return carry

    lax.fori_loop(0, tm, start, 0, unroll=8)
    lax.fori_loop(0, tm, wait, 0, unroll=8)


def _scatter_rows(x, pos_flat, pad_start, pad_count, total_rows):
    n, d = x.shape
    tm = ROW_TILE
    return pl.pallas_call(
        _scatter_kernel,
        grid=(n // tm,),
        in_specs=[
            pl.BlockSpec((TOP_K * tm,), lambda i: (i,), memory_space=pltpu.SMEM),
            pl.BlockSpec(memory_space=pltpu.SMEM),
            pl.BlockSpec(memory_space=pltpu.SMEM),
            pl.BlockSpec((tm, d), lambda i: (i, 0)),
        ],
        out_specs=pl.BlockSpec(memory_space=pl.ANY),
        out_shape=jax.ShapeDtypeStruct((total_rows, d), x.dtype),
        scratch_shapes=[pltpu.VMEM((8, d), x.dtype), pltpu.SemaphoreType.DMA, pltpu.SemaphoreType.DMA],
        compiler_params=_params("arbitrary"),
        name="moe_scatter",
    )(pos_flat, pad_start, pad_count, x)


def _combine_kernel(pos_ref, h_ref, meta_ref, g_ref, ys_ref, o_ref, y_ref, sem):
    tm = h_ref.shape[0]

    def row_copy(t, s):
        src = pos_ref[TOP_K * t + s]
        return pltpu.make_async_copy(ys_ref.at[pl.ds(src, 1)], y_ref.at[s, pl.ds(t, 1)], sem)

    def start(t, carry):
        for s in range(TOP_K):
            row_copy(t, s).start()
        return carry

    def wait(t, carry):
        for s in range(TOP_K):
            row_copy(t, s).wait()
        return carry

    lax.fori_loop(0, tm, start, 0, unroll=8)
    lax.fori_loop(0, tm, wait, 0, unroll=8)
    meta = meta_ref[...]
    w1 = meta[:, 2:3]
    w2 = meta[:, 3:4]
    moe = w1 * y_ref[0].astype(F32) + w2 * y_ref[1].astype(F32)
    o_ref[...] = _rms(h_ref[...] + moe, g_ref[...])


def _combine(h, meta, gain, ys, pos_flat):
    n, d = h.shape
    tm = ROW_TILE
    return pl.pallas_call(
        _combine_kernel,
        grid=(n // tm,),
        in_specs=[
            pl.BlockSpec((TOP_K * tm,), lambda i: (i,), memory_space=pltpu.SMEM),
            pl.BlockSpec((tm, d), lambda i: (i, 0)),
            pl.BlockSpec((tm, META_WIDTH), lambda i: (i, 0)),
            pl.BlockSpec((1, d), lambda i: (0, 0)),
            pl.BlockSpec(memory_space=pl.ANY),
        ],
        out_specs=pl.BlockSpec((tm, d), lambda i: (i, 0)),
        out_shape=jax.ShapeDtypeStruct((n, d), F32),
        scratch_shapes=[pltpu.VMEM((TOP_K, tm, d), ys.dtype), pltpu.SemaphoreType.DMA],
        compiler_params=_params("arbitrary"),
        name="moe_combine",
    )(pos_flat, h, meta, gain.reshape(1, d), ys)


def kernel(x, mix_norm, ffn_norm, gla_in_proj, gla_gate_w_fwd, gla_gate_b_fwd, gla_gate_w_bwd, gla_gate_b_bwd, gla_head_norm, gla_out_proj, swa_qkv_proj, swa_qkv_bias, swa_sinks, swa_out_proj, swa_out_bias, dense_w_gate, dense_w_up, dense_w_down, moe_router, moe_w_gate, moe_w_up, moe_w_down, final_norm):
    batch, seq, d = x.shape
    n = batch * seq
    h0 = x.reshape(n, d)

    key_w = gla_gate_w_fwd.shape[2]
    val_w = gla_head_norm.shape[1]
    rank = GLA_GATE_RANK
    in_w = gla_in_proj.shape[2]
    splits = ((0, key_w), (key_w, key_w), (2 * key_w, val_w), (2 * key_w + val_w, val_w),
              (2 * key_w + 2 * val_w, 2 * rank))
    q, k, v, r, lr = _norm_proj(h0, mix_norm[0], gla_in_proj[0].astype(BF16), jnp.zeros((in_w,), F32),
                                splits, (BF16, BF16, BF16, BF16, F32), "gla_in_proj")
    zero_gate = jnp.zeros((rank, key_w), F32)
    wgf = jnp.concatenate([gla_gate_w_fwd[0], zero_gate], axis=0).astype(BF16)
    wgb = jnp.concatenate([zero_gate, gla_gate_w_bwd[0]], axis=0).astype(BF16)
    og = _gla(q, k, v, lr, r, wgf, gla_gate_b_fwd[0].reshape(1, key_w), wgb,
              gla_gate_b_bwd[0].reshape(1, key_w), gla_head_norm[0].reshape(1, val_w), batch, seq)

    dense_wgu, dense_wd = _swiglu_weights(dense_w_gate, dense_w_up, dense_w_down)
    h2 = _proj_swiglu(og, gla_out_proj[0].astype(BF16), h0, ffn_norm[0], dense_wgu, dense_wd)

    qw = N_Q_HEADS * HEAD_DIM
    kvw = N_KV_HEADS * HEAD_DIM
    scale = HEAD_DIM ** -0.5
    wqkv, bqkv = swa_qkv_proj[0], swa_qkv_bias[0]

    def dup(m):
        lead = m.shape[:-1]
        m = m.reshape(lead + (N_KV_HEADS, 1, HEAD_DIM))
        return jnp.broadcast_to(m, lead + (N_KV_HEADS, 2, HEAD_DIM)).reshape(lead + (2 * kvw,))

    w_aug = jnp.concatenate([wqkv[:, :qw] * scale, dup(wqkv[:, qw:qw + kvw]), dup(wqkv[:, qw + kvw:])], axis=1)
    b_aug = jnp.concatenate([bqkv[:qw] * scale, dup(bqkv[qw:qw + kvw]), dup(bqkv[qw + kvw:])], axis=0)
    splits = ((0, qw), (qw, 2 * kvw), (qw + 2 * kvw, 2 * kvw))
    aq, ak, av = _norm_proj(h2, mix_norm[1], w_aug.astype(BF16), b_aug, splits, (BF16, BF16, BF16),
                            "swa_qkv_proj")
    oa = _swa(aq, ak, av, swa_sinks[0], batch, seq)

    n_experts = moe_router.shape[2]
    h3, hn3, meta, counts = _proj_residual_router(oa, swa_out_proj[0].astype(BF16), swa_out_bias[0], h2,
                                                  ffn_norm[1], moe_router[0], "swa_out_proj")
    tm = MOE_TILE
    counts = counts[0, :n_experts].astype(jnp.int32)
    tiles_per_expert = (counts + tm - 1) // tm
    tile_end = jnp.cumsum(tiles_per_expert)
    offsets = (tile_end - tiles_per_expert) * tm
    n_tiles = (TOP_K * n) // tm + n_experts
    tile_ids = jnp.arange(n_tiles, dtype=jnp.int32)
    tile_expert = jnp.sum((tile_ids[:, None] >= tile_end[None, :]).astype(jnp.int32), axis=1)
    tile_expert = jnp.minimum(tile_expert, n_experts - 1)
    n_used = tile_end[-1:].astype(jnp.int32)
    eidx = meta[:, 0:TOP_K].astype(jnp.int32)
    pos = offsets[eidx] + meta[:, 4:4 + TOP_K].astype(jnp.int32)
    pos_flat = pos.reshape(-1)

    pad_start = offsets + counts
    pad_count = tiles_per_expert * tm - counts
    xs = _scatter_rows(hn3, pos_flat, pad_start, pad_count, n_tiles * tm)
    moe_wgu, moe_wd = _swiglu_weights(moe_w_gate[0], moe_w_up[0], moe_w_down[0])
    ys = _moe_swiglu(xs, moe_wgu, moe_wd, tile_expert, n_used, tm, F32)
    out = _combine(h3, meta, final_norm, ys, pos_flat)
    return out.reshape(batch, seq, d)
```

```python
import functools

import jax
import jax.numpy as jnp
from jax import lax
from jax.experimental import pallas as pl
from jax.experimental.pallas import tpu as pltpu

F32 = jnp.float32
BF16 = jnp.bfloat16

NORM_EPS = 1e-5

GLA_HEADS = 4
GLA_GATE_RANK = 16
GLA_GATE_TAU = 16.0
GLA_CHUNK = 64
N_Q_HEADS = 16
N_KV_HEADS = 4
HEAD_DIM = 64
GROUP = N_Q_HEADS // N_KV_HEADS
WINDOW = 128
ATT_BLOCK = 128
TOP_K = 2

LANES = 128
VMEM_LIMIT_BYTES = 56 * 2**20

ROW_TILE = 512
GLA_BLOCK = 512
GLA_GROUP = 256
GLA_HEADS_PER_STEP = 4
FFN_TILE = 1024
FFN_F_TILE = 256
MOE_TILE = 1024
COMBINE_TILE = 256
META_WIDTH = 8


def _params(*sem):
    return pltpu.CompilerParams(dimension_semantics=sem, vmem_limit_bytes=VMEM_LIMIT_BYTES)


def _rms(x, gain):
    y = x * lax.rsqrt(jnp.mean(x * x, axis=-1, keepdims=True) + NORM_EPS)
    return y * gain


def _silu(x):
    return x * (1.0 / (1.0 + jnp.exp(-x)))


def _norm_proj_kernel(x_ref, g_ref, w_ref, b_ref, *o_refs, splits):
    y = _rms(x_ref[...], g_ref[...]).astype(BF16)
    for (start, width), o_ref in zip(splits, o_refs):
        acc = jnp.dot(y, w_ref[:, start:start + width], preferred_element_type=F32)
        acc = acc + b_ref[:, start:start + width]
        o_ref[...] = acc.astype(o_ref.dtype)


def _norm_proj(x, gain, w, bias, splits, dtypes, name):
    n, d = x.shape
    nout = w.shape[1]
    tm = ROW_TILE
    assert n % tm == 0
    return pl.pallas_call(
        functools.partial(_norm_proj_kernel, splits=splits),
        grid=(n // tm,),
        in_specs=[
            pl.BlockSpec((tm, d), lambda i: (i, 0)),
            pl.BlockSpec((1, d), lambda i: (0, 0)),
            pl.BlockSpec((d, nout), lambda i: (0, 0)),
            pl.BlockSpec((1, nout), lambda i: (0, 0)),
        ],
        out_specs=[pl.BlockSpec((tm, wd), lambda i: (i, 0)) for (_, wd) in splits],
        out_shape=[jax.ShapeDtypeStruct((n, wd), dt) for (_, wd), dt in zip(splits, dtypes)],
        compiler_params=_params("arbitrary"),
        name=name,
    )(x, gain.reshape(1, d), w, bias.reshape(1, nout))


def _gla_direction(q_ref, k_ref, v_ref, lr_ref, wg_ref, bg_ref, state_ref, reverse):
    heads, dk, dv = state_ref.shape
    rows, width = q_ref.shape
    c = GLA_CHUNK
    nc = rows // c
    q = q_ref[...].astype(F32) * (dk ** -0.5)
    k = k_ref[...].astype(F32)
    v = v_ref[...]
    z = jnp.dot(lr_ref[...].astype(BF16), wg_ref[...], preferred_element_type=F32) + bg_ref[...]
    la = (jnp.minimum(z, 0.0) - jnp.log(1.0 + jnp.exp(-jnp.abs(z)))) * (1.0 / GLA_GATE_TAU)
    la_hi = la.astype(BF16)
    la_lo = (la - la_hi.astype(F32)).astype(BF16)

    tn = (((0,), (0,)), ((), ()))
    nt = (((1,), (1,)), ((), ()))
    grp = GLA_GROUP
    ng = rows // grp
    ri = lax.broadcasted_iota(jnp.int32, (grp, grp), 0)
    ci = lax.broadcasted_iota(jnp.int32, (grp, grp), 1)
    same_chunk = (ri // c) == (ci // c)
    if reverse:
        cum_mask = same_chunk & (ci >= ri)
        att_mask = same_chunk & (ci > ri)
        ref_row, last_row = c // 2 - 1, 0
    else:
        cum_mask = same_chunk & (ci <= ri)
        att_mask = cum_mask
        ref_row, last_row = c // 2, c - 1
    cum = jnp.where(cum_mask, 1.0, 0.0).astype(BF16)

    la_hl = jnp.concatenate([la_hi, la_lo], axis=1)
    b = jnp.concatenate(
        [jnp.dot(cum, la_hl[g * grp:(g + 1) * grp], preferred_element_type=F32) for g in range(ng)],
        axis=0)
    b = b[:, :width] + b[:, width:]
    b3 = b.reshape(nc, c, width)
    b_ref = b3[:, ref_row:ref_row + 1, :]
    b_last = b3[:, last_row:last_row + 1, :]
    q3 = q.reshape(nc, c, width)
    k3 = k.reshape(nc, c, width)
    qe = (q3 * jnp.exp(b3 - b_ref)).astype(BF16).reshape(rows, width)
    ke = (k3 * jnp.exp(b_ref - b3)).astype(BF16).reshape(rows, width)
    kd = (k3 * jnp.exp(b_last - b3)).astype(BF16).reshape(rows, width)
    qb = (q3 * jnp.exp(b3)).astype(BF16).reshape(rows, width)
    decay_rows = jnp.exp(b_last.reshape(nc, width))

    outs = []
    for h in range(heads):
        ks = slice(h * dk, (h + 1) * dk)
        vh = v[:, h * dv:(h + 1) * dv]
        o_intra = []
        for g in range(ng):
            sl = slice(g * grp, (g + 1) * grp)
            s = lax.dot_general(qe[sl, ks], ke[sl, ks], nt, preferred_element_type=F32)
            s = jnp.where(att_mask, s, 0.0).astype(BF16)
            o_intra.append(jnp.dot(s, vh[sl], preferred_element_type=F32))
        o_intra = jnp.concatenate(o_intra, axis=0)

        upd = [lax.dot_general(kd[j * c:(j + 1) * c, ks], vh[j * c:(j + 1) * c], tn,
                               preferred_element_type=F32) for j in range(nc)]
        decay_cols = jnp.concatenate([decay_rows[:, ks], jnp.zeros((dk - nc, dk), F32)], axis=0).T

        state = state_ref[h]
        o_inter = [None] * nc
        for j in (range(nc - 1, -1, -1) if reverse else range(nc)):
            o_inter[j] = jnp.dot(qb[j * c:(j + 1) * c, ks], state.astype(BF16),
                                 preferred_element_type=F32)
            state = state * decay_cols[:, j:j + 1] + upd[j]
        state_ref[h] = state
        outs.append(o_intra + jnp.concatenate(o_inter, axis=0))
    return jnp.concatenate(outs, axis=1)


def _gla_kernel(q_ref, k_ref, v_ref, lr_ref, r_ref, wgf_ref, bgf_ref, wgb_ref, bgb_ref, gain_ref,
                o_ref, state_ref, oacc_ref, *, nblk):
    i = pl.program_id(2)
    rows = q_ref.shape[0]
    heads, _, dv = state_ref.shape

    @pl.when((i == 0) | (i == nblk))
    def _():
        state_ref[...] = jnp.zeros_like(state_ref)

    @pl.when(i < nblk)
    def _():
        o = _gla_direction(q_ref, k_ref, v_ref, lr_ref, wgf_ref, bgf_ref, state_ref, False)
        oacc_ref[pl.ds(pl.multiple_of(i * rows, rows), rows), :] = o

    @pl.when(i >= nblk)
    def _():
        j = 2 * nblk - 1 - i
        o = _gla_direction(q_ref, k_ref, v_ref, lr_ref, wgb_ref, bgb_ref, state_ref, True)
        o = o + oacc_ref[pl.ds(pl.multiple_of(j * rows, rows), rows), :]
        gain = gain_ref[...]
        o = jnp.concatenate([_rms(o[:, h * dv:(h + 1) * dv], gain[:, h * dv:(h + 1) * dv])
                             for h in range(heads)], axis=1)
        o_ref[...] = (o * _silu(r_ref[...].astype(F32))).astype(o_ref.dtype)


def _gla(q, k, v, lr, r, wgf, bgf, wgb, bgb, gain, batch, seq):
    n = q.shape[0]
    heads = GLA_HEADS
    hps = GLA_HEADS_PER_STEP
    dk = q.shape[1] // heads
    dv = v.shape[1] // heads
    rows = GLA_BLOCK
    nblk = seq // rows
    assert seq % rows == 0 and rows % GLA_GROUP == 0 and GLA_GROUP % GLA_CHUNK == 0 and heads % hps == 0

    def blk(i):
        return jnp.where(i < nblk, i, 2 * nblk - 1 - i)

    def late_blk(i):
        return jnp.where(i < nblk, nblk - 1, 2 * nblk - 1 - i)

    row_map = lambda b, h, i: (b * nblk + blk(i), h)
    return pl.pallas_call(
        functools.partial(_gla_kernel, nblk=nblk),
        grid=(batch, heads // hps, 2 * nblk),
        in_specs=[
            pl.BlockSpec((rows, hps * dk), row_map),
            pl.BlockSpec((rows, hps * dk), row_map),
            pl.BlockSpec((rows, hps * dv), row_map),
            pl.BlockSpec((rows, lr.shape[1]), lambda b, h, i: (b * nblk + blk(i), 0)),
            pl.BlockSpec((rows, hps * dv), lambda b, h, i: (b * nblk + late_blk(i), h)),
            pl.BlockSpec((wgf.shape[0], hps * dk), lambda b, h, i: (0, h)),
            pl.BlockSpec((1, hps * dk), lambda b, h, i: (0, h)),
            pl.BlockSpec((wgb.shape[0], hps * dk), lambda b, h, i: (0, h)),
            pl.BlockSpec((1, hps * dk), lambda b, h, i: (0, h)),
            pl.BlockSpec((1, hps * dv), lambda b, h, i: (0, h)),
        ],
        out_specs=pl.BlockSpec((rows, hps * dv), lambda b, h, i: (b * nblk + late_blk(i), h)),
        out_shape=jax.ShapeDtypeStruct((n, heads * dv), BF16),
        scratch_shapes=[pltpu.VMEM((hps, dk, dv), F32), pltpu.VMEM((seq, hps * dv), F32)],
        compiler_params=_params("arbitrary", "arbitrary", "arbitrary"),
        name="gla",
    )(q, k, v, lr, r, wgf, bgf, wgb, bgb, gain)


def _proj_residual_router_kernel(a_ref, w_ref, b_ref, h_ref, g_ref, rhl_ref,
                                 h_out_ref, hn_out_ref, meta_ref, count_ref, *, n_experts):
    i = pl.program_id(0)
    h = h_ref[...] + (jnp.dot(a_ref[...], w_ref[...], preferred_element_type=F32) + b_ref[...])
    h_out_ref[...] = h
    hn = _rms(h, g_ref[...])
    hn_out_ref[...] = hn.astype(hn_out_ref.dtype)

    hn_hi = hn.astype(BF16)
    hn_lo = (hn - hn_hi.astype(F32)).astype(BF16)
    hh = jnp.dot(hn_hi, rhl_ref[...], preferred_element_type=F32)
    logits = (hh[:, :LANES] + hh[:, LANES:]
              + jnp.dot(hn_lo, rhl_ref[:, :LANES], preferred_element_type=F32))
    tm = logits.shape[0]
    lane = lax.broadcasted_iota(jnp.int32, logits.shape, 1).astype(F32)
    neg = jnp.float32(-jnp.inf)
    logits = jnp.where(lane < n_experts, logits, neg)
    m1 = jnp.max(logits, axis=-1, keepdims=True)
    i1 = jnp.min(jnp.where(logits == m1, lane, float(LANES)), axis=-1, keepdims=True)
    rest = jnp.where(lane == i1, neg, logits)
    m2 = jnp.max(rest, axis=-1, keepdims=True)
    i2 = jnp.min(jnp.where(rest == m2, lane, float(LANES)), axis=-1, keepdims=True)
    e2 = jnp.exp(m2 - m1)
    w1 = 1.0 / (1.0 + e2)
    w2 = e2 / (1.0 + e2)

    @pl.when(i == 0)
    def _():
        count_ref[...] = jnp.zeros_like(count_ref)

    sel = (lane == i1) | (lane == i2)
    onehot = jnp.where(sel, 1.0, 0.0)
    ri = lax.broadcasted_iota(jnp.int32, (tm, tm), 0)
    ci = lax.broadcasted_iota(jnp.int32, (tm, tm), 1)
    strict_lower = jnp.where(ci < ri, 1.0, 0.0).astype(BF16)
    rank = jnp.dot(strict_lower, onehot.astype(BF16), preferred_element_type=F32) + count_ref[...]
    r1 = jnp.sum(jnp.where(lane == i1, rank, 0.0), axis=-1, keepdims=True)
    r2 = jnp.sum(jnp.where(lane == i2, rank, 0.0), axis=-1, keepdims=True)
    count_ref[...] = count_ref[...] + jnp.sum(onehot, axis=0, keepdims=True)

    meta = jnp.where(lane == 0, i1, 0.0)
    meta = jnp.where(lane == 1, i2, meta)
    meta = jnp.where(lane == 2, w1, meta)
    meta = jnp.where(lane == 3, w2, meta)
    meta = jnp.where(lane == 4, r1, meta)
    meta = jnp.where(lane == 5, r2, meta)
    meta_ref[...] = meta[:, :meta_ref.shape[1]]


def _proj_residual_router(a, w, bias, h, gain, router, name):
    n, kdim = a.shape
    d = w.shape[1]
    tm = ROW_TILE
    assert n % tm == 0
    n_experts = router.shape[1]
    rpad = jnp.zeros((d, LANES), F32).at[:, :n_experts].set(router)
    rhi = rpad.astype(BF16)
    rlo = (rpad - rhi.astype(F32)).astype(BF16)
    row = lambda i: (i, 0)
    fixed = lambda i: (0, 0)
    return pl.pallas_call(
        functools.partial(_proj_residual_router_kernel, n_experts=n_experts),
        grid=(n // tm,),
        in_specs=[
            pl.BlockSpec((tm, kdim), row),
            pl.BlockSpec((kdim, d), fixed),
            pl.BlockSpec((1, d), fixed),
            pl.BlockSpec((tm, d), row),
            pl.BlockSpec((1, d), fixed),
            pl.BlockSpec((d, 2 * LANES), fixed),
        ],
        out_specs=[pl.BlockSpec((tm, d), row), pl.BlockSpec((tm, d), row),
                   pl.BlockSpec((tm, META_WIDTH), row), pl.BlockSpec((1, LANES), fixed)],
        out_shape=[jax.ShapeDtypeStruct((n, d), F32), jax.ShapeDtypeStruct((n, d), F32),
                   jax.ShapeDtypeStruct((n, META_WIDTH), F32), jax.ShapeDtypeStruct((1, LANES), F32)],
        compiler_params=_params("arbitrary"),
        name=name,
    )(a, w, bias.reshape(1, d), h, gain.reshape(1, d), jnp.concatenate([rhi, rlo], axis=1))


def _swiglu_step(x, wgu_ref, wd_ref, acc_ref):
    tf = wd_ref.shape[0]
    gu = jnp.dot(x, wgu_ref[...], preferred_element_type=F32)
    a = (_silu(gu[:, :tf]) * gu[:, tf:]).astype(BF16)
    acc_ref[...] += jnp.dot(a, wd_ref[...], preferred_element_type=F32)


def _moe_swiglu_kernel(te_ref, nu_ref, x_ref, wgu_ref, wd_ref, o_ref, acc_ref):
    i = pl.program_id(0)
    j = pl.program_id(1)
    used = i < nu_ref[0]

    @pl.when(used & (j == 0))
    def _():
        acc_ref[...] = jnp.zeros_like(acc_ref)

    @pl.when(used)
    def _():
        _swiglu_step(x_ref[...].astype(BF16), wgu_ref, wd_ref, acc_ref)

    @pl.when(j == pl.num_programs(1) - 1)
    def _():
        @pl.when(used)
        def _():
            o_ref[...] = acc_ref[...].astype(o_ref.dtype)

        @pl.when(jnp.logical_not(used))
        def _():
            o_ref[...] = jnp.zeros_like(o_ref)


def _swiglu_weights(wg, wu, wd):
    e, d, f = wg.shape
    tf = FFN_F_TILE
    assert f % tf == 0
    nf = f // tf
    tiles = []
    for j in range(nf):
        tiles += [wg[:, :, j * tf:(j + 1) * tf], wu[:, :, j * tf:(j + 1) * tf]]
    wgu = jnp.concatenate(tiles, axis=-1).astype(BF16)
    return wgu, wd.astype(BF16).reshape(e, nf, tf, d)


def _moe_swiglu(x, wgu, wd, tile_expert, n_used, tm, out_dtype):
    rows, d = x.shape
    nf, tf = wd.shape[1], wd.shape[2]
    assert rows % tm == 0

    def fcol(i, j, nu):
        return jnp.where(i < nu[0], j, nf - 1)

    def xmap(i, j, te, nu):
        return (jnp.where(i < nu[0], i, nu[0] - 1), 0)

    return pl.pallas_call(
        _moe_swiglu_kernel,
        grid_spec=pltpu.PrefetchScalarGridSpec(
            num_scalar_prefetch=2,
            grid=(rows // tm, nf),
            in_specs=[
                pl.BlockSpec((tm, d), xmap),
                pl.BlockSpec((None, d, 2 * tf), lambda i, j, te, nu: (te[i], 0, fcol(i, j, nu))),
                pl.BlockSpec((None, None, tf, d), lambda i, j, te, nu: (te[i], fcol(i, j, nu), 0, 0)),
            ],
            out_specs=pl.BlockSpec((tm, d), lambda i, j, te, nu: (i, 0)),
            scratch_shapes=[pltpu.VMEM((tm, d), F32)],
        ),
        out_shape=jax.ShapeDtypeStruct((rows, d), out_dtype),
        compiler_params=_params("arbitrary", "arbitrary"),
        name="moe_swiglu",
    )(tile_expert, n_used, x, wgu, wd)


def _proj_swiglu_kernel(a_ref, wo_ref, h_ref, g_ref, wgu_ref, wd_ref, o_ref, hres_ref, xn_ref, acc_ref):
    j = pl.program_id(1)

    @pl.when(j == 0)
    def _():
        h = h_ref[...] + jnp.dot(a_ref[...], wo_ref[...], preferred_element_type=F32)
        hres_ref[...] = h
        xn_ref[...] = _rms(h, g_ref[...]).astype(xn_ref.dtype)
        acc_ref[...] = jnp.zeros_like(acc_ref)

    _swiglu_step(xn_ref[...], wgu_ref, wd_ref, acc_ref)

    @pl.when(j == pl.num_programs(1) - 1)
    def _():
        o_ref[...] = hres_ref[...] + acc_ref[...]


def _proj_swiglu(a, wo, h, gain, wgu, wd):
    n, kdim = a.shape
    d = wo.shape[1]
    nf, tf = wd.shape[1], wd.shape[2]
    tm = FFN_TILE
    assert n % tm == 0
    return pl.pallas_call(
        _proj_swiglu_kernel,
        grid=(n // tm, nf),
        in_specs=[
            pl.BlockSpec((tm, kdim), lambda i, j: (i, 0)),
            pl.BlockSpec((kdim, d), lambda i, j: (0, 0)),
            pl.BlockSpec((tm, d), lambda i, j: (i, 0)),
            pl.BlockSpec((1, d), lambda i, j: (0, 0)),
            pl.BlockSpec((None, d, 2 * tf), lambda i, j: (0, 0, j)),
            pl.BlockSpec((None, None, tf, d), lambda i, j: (0, j, 0, 0)),
        ],
        out_specs=pl.BlockSpec((tm, d), lambda i, j: (i, 0)),
        out_shape=jax.ShapeDtypeStruct((n, d), F32),
        scratch_shapes=[pltpu.VMEM((tm, d), F32), pltpu.VMEM((tm, d), BF16), pltpu.VMEM((tm, d), F32)],
        compiler_params=_params("arbitrary", "arbitrary"),
        name="gla_out_proj_dense_swiglu",
    )(a, wo, h, gain.reshape(1, d), wgu, wd)


def _alibi_slope(head):
    return float(2.0 ** (-8.0 * (head + 1) / N_Q_HEADS))


def _swa_bias(t):
    qi = jnp.arange(t)[:, None]
    kj = jnp.arange(3 * t)[None, :]
    dist = jnp.abs(qi + t - kj)
    band = dist <= WINDOW
    valid = jnp.stack([band & (kj >= t), band, band & (kj < 2 * t)])
    slopes = jnp.asarray([_alibi_slope(h) for h in range(N_Q_HEADS)], F32)
    bias = -slopes[None, :, None, None] * dist.astype(F32)[None, None]
    bias = jnp.where(valid[:, None], bias, -jnp.inf)
    return bias.reshape(3, N_KV_HEADS, GROUP * t, 3 * t)


def _swa_kernel(sink_ref, q_ref, kp_ref, kc_ref, kn_ref, vp_ref, vc_ref, vn_ref, bias0_ref, bias1_ref,
                o_ref):
    t = ATT_BLOCK
    lane = lax.broadcasted_iota(jnp.int32, (t, LANES), 1)
    low = lane < HEAD_DIM
    row_head = lax.broadcasted_iota(jnp.int32, (GROUP * t, 1), 0) // t
    ones = jnp.ones((3 * t, LANES), BF16)

    def window(p_ref, c_ref, n_ref, kvh, sub):
        ks = slice(kvh * LANES, (kvh + 1) * LANES)
        if sub == 0:
            return jnp.concatenate([p_ref[t:, ks], c_ref[:, ks]], axis=0)
        return jnp.concatenate([c_ref[:, ks], n_ref[:t, ks]], axis=0)

    row_blocks = []
    for sub, bias_ref in enumerate((bias0_ref, bias1_ref)):
        rows = slice(sub * t, (sub + 1) * t)
        blocks = []
        for kvh in range(N_KV_HEADS):
            qs = []
            for g in range(GROUP):
                head = kvh * GROUP + g
                col = (head // 2) * LANES
                q2 = q_ref[rows, col:col + LANES]
                qs.append(jnp.where(low if head % 2 == 0 else jnp.logical_not(low), q2,
                                    jnp.zeros_like(q2)))
            qg = jnp.concatenate(qs, axis=0)
            kw = window(kp_ref, kc_ref, kn_ref, kvh, sub)
            s = lax.dot_general(qg, kw, (((1,), (1,)), ((), ())), preferred_element_type=F32)
            s = s + bias_ref[kvh]
            sink = jnp.full((GROUP * t, 1), sink_ref[kvh * GROUP], F32)
            for g in range(1, GROUP):
                sink = jnp.where(row_head == g, sink_ref[kvh * GROUP + g], sink)
            m = jnp.maximum(jnp.max(s, axis=-1, keepdims=True), sink)
            p = jnp.exp(s - m).astype(BF16)
            vw = jnp.concatenate([window(vp_ref, vc_ref, vn_ref, kvh, sub), ones], axis=1)
            res = jnp.dot(p, vw, preferred_element_type=F32)
            out = res[:, :LANES] * (1.0 / (res[:, LANES:] + jnp.exp(sink - m)))
            for pair in range(GROUP // 2):
                a = out[(2 * pair) * t:(2 * pair + 1) * t]
                b = out[(2 * pair + 1) * t:(2 * pair + 2) * t]
                blocks.append(jnp.where(low, a, b).astype(o_ref.dtype))
        row_blocks.append(jnp.concatenate(blocks, axis=1))
    o_ref[...] = jnp.concatenate(row_blocks, axis=0)


def _swa(q, kdup, vdup, sinks, batch, seq):
    n, qw = q.shape
    t = ATT_BLOCK
    rows = 2 * t
    nblk = seq // rows
    assert seq % rows == 0 and nblk >= 2
    kvw = kdup.shape[1]
    bias = _swa_bias(t)
    prev = lambda b, i, s: (b * nblk + jnp.maximum(i - 1, 0), 0)
    cur = lambda b, i, s: (b * nblk + i, 0)
    nxt = lambda b, i, s: (b * nblk + jnp.minimum(i + 1, nblk - 1), 0)
    first = lambda b, i, s: (jnp.where(i == 0, 0, 1), 0, 0, 0)
    second = lambda b, i, s: (jnp.where(i == nblk - 1, 2, 1), 0, 0, 0)
    return pl.pallas_call(
        _swa_kernel,
        grid_spec=pltpu.PrefetchScalarGridSpec(
            num_scalar_prefetch=1,
            grid=(batch, nblk),
            in_specs=[
                pl.BlockSpec((rows, qw), cur),
                pl.BlockSpec((rows, kvw), prev), pl.BlockSpec((rows, kvw), cur), pl.BlockSpec((rows, kvw), nxt),
                pl.BlockSpec((rows, kvw), prev), pl.BlockSpec((rows, kvw), cur), pl.BlockSpec((rows, kvw), nxt),
                pl.BlockSpec((None,) + bias.shape[1:], first),
                pl.BlockSpec((None,) + bias.shape[1:], second),
            ],
            out_specs=pl.BlockSpec((rows, qw), cur),
        ),
        out_shape=jax.ShapeDtypeStruct((n, qw), BF16),
        compiler_params=_params("arbitrary", "arbitrary"),
        name="swa",
    )(sinks, q, kdup, kdup, kdup, vdup, vdup, vdup, bias, bias)


def _scatter_kernel(pos_ref, pad_start_ref, pad_count_ref, x_ref, xs_ref, zero_ref, sem, zero_sem):
    tm = x_ref.shape[0]

    @pl.when(pl.program_id(0) == 0)
    def _():
        zero_ref[...] = jnp.zeros_like(zero_ref)
        for e in range(pad_start_ref.shape[0]):
            def zero_copy(r, e=e):
                dst = pad_start_ref[e] + r
                return pltpu.make_async_copy(zero_ref.at[pl.ds(0, 1)], xs_ref.at[pl.ds(dst, 1)], zero_sem)

            def zero_start(r, carry, zero_copy=zero_copy):
                zero_copy(r).start()
                return carry

            def zero_wait(r, carry, zero_copy=zero_copy):
                zero_copy(r).wait()
                return carry

            lax.fori_loop(0, pad_count_ref[e], zero_start, 0)
            lax.fori_loop(0, pad_count_ref[e], zero_wait, 0)

    def row_copy(t, s):
        dst = pos_ref[TOP_K * t + s]
        return pltpu.make_async_copy(x_ref.at[pl.ds(t, 1)], xs_ref.at[pl.ds(dst, 1)], sem)

    def start(t, carry):
        for s in range(TOP_K):
            row_copy(t, s).start()
        return carry

    def wait(t, carry):
        for s in range(TOP_K):
            row_copy(t, s).wait()
        return carry

    lax.fori_loop(0, tm, start, 0, unroll=8)
    lax.fori_loop(0, tm, wait, 0, unroll=8)


def _scatter_rows(x, pos_flat, pad_start, pad_count, total_rows):
    n, d = x.shape
    tm = ROW_TILE
    return pl.pallas_call(
        _scatter_kernel,
        grid=(n // tm,),
        in_specs=[
            pl.BlockSpec((TOP_K * tm,), lambda i: (i,), memory_space=pltpu.SMEM),
            pl.BlockSpec(memory_space=pltpu.SMEM),
            pl.BlockSpec(memory_space=pltpu.SMEM),
            pl.BlockSpec((tm, d), lambda i: (i, 0)),
        ],
        out_specs=pl.BlockSpec(memory_space=pl.ANY),
        out_shape=jax.ShapeDtypeStruct((total_rows, d), x.dtype),
        scratch_shapes=[pltpu.VMEM((8, d), x.dtype), pltpu.SemaphoreType.DMA, pltpu.SemaphoreType.DMA],
        compiler_params=_params("arbitrary"),
        name="moe_scatter",
    )(pos_flat, pad_start, pad_count, x)


SEG_ALIGN = 8
SEG_BIG = 64


def _combine_block_rows(tm, n_experts):
    worst = TOP_K * tm + 2 * (SEG_ALIGN - 1) * n_experts
    return -(-worst // LANES) * LANES


def _combine_kernel(src_ref, len_ref, dst_ref, h_ref, route_ref, g_ref, ys_ref, o_ref, yblk_ref, sems,
                    *, n_experts):
    i = pl.program_id(0)
    steps = pl.num_programs(0)
    tm = h_ref.shape[0]
    rows = yblk_ref.shape[1]

    def run_copies(tile, act):
        slot = lax.rem(tile, 2)
        for e in range(n_experts):
            k = tile * n_experts + e
            src0, dst0, length = src_ref[k], dst_ref[k], len_ref[k]
            n_big = lax.shift_right_logical(length, 6)
            n_small = lax.shift_right_logical(length - n_big * SEG_BIG, 3)

            def copy(off, size, src0=src0, dst0=dst0):
                src = pl.multiple_of(src0 + off, SEG_ALIGN)
                dst = pl.multiple_of(dst0 + off, SEG_ALIGN)
                return pltpu.make_async_copy(ys_ref.at[pl.ds(src, size)],
                                             yblk_ref.at[slot, pl.ds(dst, size)], sems.at[slot])

            def big(j, carry, copy=copy):
                act(copy(j * SEG_BIG, SEG_BIG))
                return carry

            def small(j, carry, copy=copy, n_big=n_big):
                act(copy(n_big * SEG_BIG + j * SEG_ALIGN, SEG_ALIGN))
                return carry

            lax.fori_loop(0, n_big, big, 0)
            lax.fori_loop(0, n_small, small, 0)

    start = lambda c: c.start()
    wait = lambda c: c.wait()

    @pl.when(i == 0)
    def _():
        yblk_ref[...] = jnp.zeros_like(yblk_ref)
        run_copies(i, start)

    @pl.when(i + 1 < steps)
    def _():
        run_copies(i + 1, start)

    run_copies(i, wait)

    route = route_ref[...]
    w1, w2 = route[:, 0:1], route[:, 1:2]
    col = lax.broadcasted_iota(jnp.int32, (tm, rows), 1).astype(F32)
    pick = jnp.concatenate([jnp.where(col == route[:, 2:3], 1.0, 0.0),
                            jnp.where(col == route[:, 3:4], 1.0, 0.0)], axis=0).astype(BF16)
    yblk = yblk_ref[lax.rem(i, 2)].astype(BF16)
    y = jnp.dot(pick, yblk, preferred_element_type=F32)
    o_ref[...] = _rms(h_ref[...] + (w1 * y[:tm] + w2 * y[tm:]), g_ref[...])


def _combine(h, route, gain, ys, seg_src, seg_len, seg_dst, n_experts):
    n, d = h.shape
    tm = COMBINE_TILE
    assert n % tm == 0
    rows = _combine_block_rows(tm, n_experts)
    smem = pl.BlockSpec(memory_space=pltpu.SMEM)
    return pl.pallas_call(
        functools.partial(_combine_kernel, n_experts=n_experts),
        grid=(n // tm,),
        in_specs=[
            smem, smem, smem,
            pl.BlockSpec((tm, d), lambda i: (i, 0)),
            pl.BlockSpec((tm, META_WIDTH), lambda i: (i, 0)),
            pl.BlockSpec((1, d), lambda i: (0, 0)),
            pl.BlockSpec(memory_space=pl.ANY),
        ],
        out_specs=pl.BlockSpec((tm, d), lambda i: (i, 0)),
        out_shape=jax.ShapeDtypeStruct((n, d), F32),
        scratch_shapes=[pltpu.VMEM((2, rows, d), ys.dtype), pltpu.SemaphoreType.DMA((2,))],
        compiler_params=_params("arbitrary"),
        name="moe_combine",
    )(seg_src, seg_len, seg_dst, h, route, gain.reshape(1, d), ys)


def kernel(x, mix_norm, ffn_norm, gla_in_proj, gla_gate_w_fwd, gla_gate_b_fwd, gla_gate_w_bwd, gla_gate_b_bwd, gla_head_norm, gla_out_proj, swa_qkv_proj, swa_qkv_bias, swa_sinks, swa_out_proj, swa_out_bias, dense_w_gate, dense_w_up, dense_w_down, moe_router, moe_w_gate, moe_w_up, moe_w_down, final_norm):
    batch, seq, d = x.shape
    n = batch * seq
    h0 = x.reshape(n, d)

    key_w = gla_gate_w_fwd.shape[2]
    val_w = gla_head_norm.shape[1]
    rank = GLA_GATE_RANK
    in_w = gla_in_proj.shape[2]
    splits = ((0, key_w), (key_w, key_w), (2 * key_w, val_w), (2 * key_w + val_w, val_w),
              (2 * key_w + 2 * val_w, 2 * rank))
    q, k, v, r, lr = _norm_proj(h0, mix_norm[0], gla_in_proj[0].astype(BF16), jnp.zeros((in_w,), F32),
                                splits, (BF16, BF16, BF16, BF16, F32), "gla_in_proj")
    zero_gate = jnp.zeros((rank, key_w), F32)
    wgf = jnp.concatenate([gla_gate_w_fwd[0], zero_gate], axis=0).astype(BF16)
    wgb = jnp.concatenate([zero_gate, gla_gate_w_bwd[0]], axis=0).astype(BF16)
    og = _gla(q, k, v, lr, r, wgf, gla_gate_b_fwd[0].reshape(1, key_w), wgb,
              gla_gate_b_bwd[0].reshape(1, key_w), gla_head_norm[0].reshape(1, val_w), batch, seq)

    dense_wgu, dense_wd = _swiglu_weights(dense_w_gate, dense_w_up, dense_w_down)
    h2 = _proj_swiglu(og, gla_out_proj[0].astype(BF16), h0, ffn_norm[0], dense_wgu, dense_wd)

    qw = N_Q_HEADS * HEAD_DIM
    kvw = N_KV_HEADS * HEAD_DIM
    scale = HEAD_DIM ** -0.5
    wqkv, bqkv = swa_qkv_proj[0], swa_qkv_bias[0]

    def dup(m):
        lead = m.shape[:-1]
        m = m.reshape(lead + (N_KV_HEADS, 1, HEAD_DIM))
        return jnp.broadcast_to(m, lead + (N_KV_HEADS, 2, HEAD_DIM)).reshape(lead + (2 * kvw,))

    w_aug = jnp.concatenate([wqkv[:, :qw] * scale, dup(wqkv[:, qw:qw + kvw]), dup(wqkv[:, qw + kvw:])], axis=1)
    b_aug = jnp.concatenate([bqkv[:qw] * scale, dup(bqkv[qw:qw + kvw]), dup(bqkv[qw + kvw:])], axis=0)
    splits = ((0, qw), (qw, 2 * kvw), (qw + 2 * kvw, 2 * kvw))
    aq, ak, av = _norm_proj(h2, mix_norm[1], w_aug.astype(BF16), b_aug, splits, (BF16, BF16, BF16),
                            "swa_qkv_proj")
    oa = _swa(aq, ak, av, swa_sinks[0], batch, seq)

    n_experts = moe_router.shape[2]
    h3, hn3, meta, counts = _proj_residual_router(oa, swa_out_proj[0].astype(BF16), swa_out_bias[0], h2,
                                                  ffn_norm[1], moe_router[0], "swa_out_proj")
    tm = MOE_TILE
    counts = counts[0, :n_experts].astype(jnp.int32)
    tiles_per_expert = (counts + tm - 1) // tm
    tile_end = jnp.cumsum(tiles_per_expert)
    offsets = (tile_end - tiles_per_expert) * tm
    n_tiles = (TOP_K * n) // tm + n_experts
    tile_ids = jnp.arange(n_tiles, dtype=jnp.int32)
    tile_expert = jnp.sum((tile_ids[:, None] >= tile_end[None, :]).astype(jnp.int32), axis=1)
    tile_expert = jnp.minimum(tile_expert, n_experts - 1)
    n_used = tile_end[-1:].astype(jnp.int32)
    eidx = meta[:, 0:TOP_K].astype(jnp.int32)
    pos = offsets[eidx] + meta[:, 4:4 + TOP_K].astype(jnp.int32)
    pos_flat = pos.reshape(-1)

    pad_start = offsets + counts
    pad_count = tiles_per_expert * tm - counts
    xs = _scatter_rows(hn3, pos_flat, pad_start, pad_count, n_tiles * tm)
    moe_wgu, moe_wd = _swiglu_weights(moe_w_gate[0], moe_w_up[0], moe_w_down[0])
    ys = _moe_swiglu(xs, moe_wgu, moe_wd, tile_expert, n_used, tm, F32)

    ct = COMBINE_TILE
    sel =(eidx[:, :, None] == jnp.arange(n_experts, dtype=jnp.int32)[None, None, :]).astype(jnp.int32)
    tile_cnt = jnp.sum(sel.reshape(n // ct, ct * TOP_K, n_experts), axis=1)
    run_start = offsets[None, :] + jnp.cumsum(tile_cnt, axis=0) - tile_cnt
    seg_src = run_start - run_start % SEG_ALIGN
    run_end = run_start + tile_cnt + (SEG_ALIGN - 1)
    seg_len = jnp.where(tile_cnt > 0, run_end - run_end % SEG_ALIGN - seg_src, 0)
    seg_dst = jnp.cumsum(seg_len, axis=1) - seg_len
    shift = jnp.repeat(seg_dst - seg_src, ct, axis=0)
    block_row = pos + jnp.sum(sel * shift[:, None, :], axis=-1)
    route = jnp.concatenate([meta[:, 2:2 + TOP_K], block_row.astype(F32),
                             jnp.zeros((n, META_WIDTH - 2 * TOP_K), F32)], axis=1)
    out = _combine(h3, route, final_norm, ys, seg_src.reshape(-1), seg_len.reshape(-1),
                   seg_dst.reshape(-1), n_experts)
    return out.reshape(batch, seq, d)
```

```python
import functools

import jax
import jax.numpy as jnp
from jax import lax
from jax.experimental import pallas as pl
from jax.experimental.pallas import tpu as pltpu

F32 = jnp.float32
BF16 = jnp.bfloat16

NORM_EPS = 1e-5

GLA_HEADS = 4
GLA_GATE_RANK = 16
GLA_GATE_TAU = 16.0
GLA_CHUNK = 64
N_Q_HEADS = 16
N_KV_HEADS = 4
HEAD_DIM = 64
GROUP = N_Q_HEADS // N_KV_HEADS
WINDOW = 128
ATT_BLOCK = 128
TOP_K = 2

LANES = 128
VMEM_LIMIT_BYTES = 56 * 2**20

ROW_TILE = 512
GLA_BLOCK = 512
GLA_GROUP = 256
GLA_HEADS_PER_STEP = 4
FFN_TILE = 1024
FFN_F_TILE = 256
MOE_TILE = 1024
COMBINE_TILE = 256
META_WIDTH = 8


def _params(*sem):
    return pltpu.CompilerParams(dimension_semantics=sem, vmem_limit_bytes=VMEM_LIMIT_BYTES)


def _rms(x, gain):
    y = x * lax.rsqrt(jnp.mean(x * x, axis=-1, keepdims=True) + NORM_EPS)
    return y * gain


def _silu(x):
    return x * (1.0 / (1.0 + jnp.exp(-x)))


def _norm_proj_kernel(x_ref, g_ref, w_ref, b_ref, *o_refs, splits):
    y = _rms(x_ref[...], g_ref[...]).astype(BF16)
    for (start, width), o_ref in zip(splits, o_refs):
        acc = jnp.dot(y, w_ref[:, start:start + width], preferred_element_type=F32)
        acc = acc + b_ref[:, start:start + width]
        o_ref[...] = acc.astype(o_ref.dtype)


def _norm_proj(x, gain, w, bias, splits, dtypes, name):
    n, d = x.shape
    nout = w.shape[1]
    tm = ROW_TILE
    assert n % tm == 0
    return pl.pallas_call(
        functools.partial(_norm_proj_kernel, splits=splits),
        grid=(n // tm,),
        in_specs=[
            pl.BlockSpec((tm, d), lambda i: (i, 0)),
            pl.BlockSpec((1, d), lambda i: (0, 0)),
            pl.BlockSpec((d, nout), lambda i: (0, 0)),
            pl.BlockSpec((1, nout), lambda i: (0, 0)),
        ],
        out_specs=[pl.BlockSpec((tm, wd), lambda i: (i, 0)) for (_, wd) in splits],
        out_shape=[jax.ShapeDtypeStruct((n, wd), dt) for (_, wd), dt in zip(splits, dtypes)],
        compiler_params=_params("arbitrary"),
        name=name,
    )(x, gain.reshape(1, d), w, bias.reshape(1, nout))


def _gla_direction(q_ref, k_ref, v_ref, lr_ref, wg_ref, bg_ref, state_ref, reverse):
    heads, dk, dv = state_ref.shape
    rows, width = q_ref.shape
    c = GLA_CHUNK
    nc = rows // c
    q = q_ref[...].astype(F32) * (dk ** -0.5)
    k = k_ref[...].astype(F32)
    v = v_ref[...]
    z = jnp.dot(lr_ref[...].astype(BF16), wg_ref[...], preferred_element_type=F32) + bg_ref[...]
    la = (jnp.minimum(z, 0.0) - jnp.log(1.0 + jnp.exp(-jnp.abs(z)))) * (1.0 / GLA_GATE_TAU)
    la_hi = la.astype(BF16)
    la_lo = (la - la_hi.astype(F32)).astype(BF16)

    tn = (((0,), (0,)), ((), ()))
    nt = (((1,), (1,)), ((), ()))
    grp = GLA_GROUP
    ng = rows // grp
    ri = lax.broadcasted_iota(jnp.int32, (grp, grp), 0)
    ci = lax.broadcasted_iota(jnp.int32, (grp, grp), 1)
    same_chunk = (ri // c) == (ci // c)
    if reverse:
        cum_mask = same_chunk & (ci >= ri)
        att_mask = same_chunk & (ci > ri)
        ref_row, last_row = c // 2 - 1, 0
    else:
        cum_mask = same_chunk & (ci <= ri)
        att_mask = cum_mask
        ref_row, last_row = c // 2, c - 1
    cum = jnp.where(cum_mask, 1.0, 0.0).astype(BF16)

    la_hl = jnp.concatenate([la_hi, la_lo], axis=1)
    b = jnp.concatenate(
        [jnp.dot(cum, la_hl[g * grp:(g + 1) * grp], preferred_element_type=F32) for g in range(ng)],
        axis=0)
    b = b[:, :width] + b[:, width:]
    b3 = b.reshape(nc, c, width)
    b_ref = b3[:, ref_row:ref_row + 1, :]
    b_last = b3[:, last_row:last_row + 1, :]
    q3 = q.reshape(nc, c, width)
    k3 = k.reshape(nc, c, width)
    qe = (q3 * jnp.exp(b3 - b_ref)).astype(BF16).reshape(rows, width)
    ke = (k3 * jnp.exp(b_ref - b3)).astype(BF16).reshape(rows, width)
    kd = (k3 * jnp.exp(b_last - b3)).astype(BF16).reshape(rows, width)
    qb = (q3 * jnp.exp(b3)).astype(BF16).reshape(rows, width)
    decay_rows = jnp.exp(b_last.reshape(nc, width))

    outs = []
    for h in range(heads):
        ks = slice(h * dk, (h + 1) * dk)
        vh = v[:, h * dv:(h + 1) * dv]
        o_intra = []
        for g in range(ng):
            sl = slice(g * grp, (g + 1) * grp)
            s = lax.dot_general(qe[sl, ks], ke[sl, ks], nt, preferred_element_type=F32)
            s = jnp.where(att_mask, s, 0.0).astype(BF16)
            o_intra.append(jnp.dot(s, vh[sl], preferred_element_type=F32))
        o_intra = jnp.concatenate(o_intra, axis=0)

        upd = [lax.dot_general(kd[j * c:(j + 1) * c, ks], vh[j * c:(j + 1) * c], tn,
                               preferred_element_type=F32) for j in range(nc)]
        decay_cols = jnp.concatenate([decay_rows[:, ks], jnp.zeros((dk - nc, dk), F32)], axis=0).T

        state = state_ref[h]
        o_inter = [None] * nc
        for j in (range(nc - 1, -1, -1) if reverse else range(nc)):
            o_inter[j] = jnp.dot(qb[j * c:(j + 1) * c, ks], state.astype(BF16),
                                 preferred_element_type=F32)
            state = state * decay_cols[:, j:j + 1] + upd[j]
        state_ref[h] = state
        outs.append(o_intra + jnp.concatenate(o_inter, axis=0))
    return jnp.concatenate(outs, axis=1)


def _gla_kernel(q_ref, k_ref, v_ref, lr_ref, r_ref, wgf_ref, bgf_ref, wgb_ref, bgb_ref, gain_ref,
                o_ref, state_ref, oacc_ref, *, nblk):
    i = pl.program_id(2)
    rows = q_ref.shape[0]
    heads, _, dv = state_ref.shape

    @pl.when((i == 0) | (i == nblk))
    def _():
        state_ref[...] = jnp.zeros_like(state_ref)

    @pl.when(i < nblk)
    def _():
        o = _gla_direction(q_ref, k_ref, v_ref, lr_ref, wgf_ref, bgf_ref, state_ref, False)
        oacc_ref[pl.ds(pl.multiple_of(i * rows, rows), rows), :] = o

    @pl.when(i >= nblk)
    def _():
        j = 2 * nblk - 1 - i
        o = _gla_direction(q_ref, k_ref, v_ref, lr_ref, wgb_ref, bgb_ref, state_ref, True)
        o = o + oacc_ref[pl.ds(pl.multiple_of(j * rows, rows), rows), :]
        gain = gain_ref[...]
        o = jnp.concatenate([_rms(o[:, h * dv:(h + 1) * dv], gain[:, h * dv:(h + 1) * dv])
                             for h in range(heads)], axis=1)
        o_ref[...] = (o * _silu(r_ref[...].astype(F32))).astype(o_ref.dtype)


def _gla(q, k, v, lr, r, wgf, bgf, wgb, bgb, gain, batch, seq):
    n = q.shape[0]
    heads = GLA_HEADS
    hps = GLA_HEADS_PER_STEP
    dk = q.shape[1] // heads
    dv = v.shape[1] // heads
    rows = GLA_BLOCK
    nblk = seq // rows
    assert seq % rows == 0 and rows % GLA_GROUP == 0 and GLA_GROUP % GLA_CHUNK == 0 and heads % hps == 0

    def blk(i):
        return jnp.where(i < nblk, i, 2 * nblk - 1 - i)

    def late_blk(i):
        return jnp.where(i < nblk, nblk - 1, 2 * nblk - 1 - i)

    row_map = lambda b, h, i: (b * nblk + blk(i), h)
    return pl.pallas_call(
        functools.partial(_gla_kernel, nblk=nblk),
        grid=(batch, heads // hps, 2 * nblk),
        in_specs=[
            pl.BlockSpec((rows, hps * dk), row_map),
            pl.BlockSpec((rows, hps * dk), row_map),
            pl.BlockSpec((rows, hps * dv), row_map),
            pl.BlockSpec((rows, lr.shape[1]), lambda b, h, i: (b * nblk + blk(i), 0)),
            pl.BlockSpec((rows, hps * dv), lambda b, h, i: (b * nblk + late_blk(i), h)),
            pl.BlockSpec((wgf.shape[0], hps * dk), lambda b, h, i: (0, h)),
            pl.BlockSpec((1, hps * dk), lambda b, h, i: (0, h)),
            pl.BlockSpec((wgb.shape[0], hps * dk), lambda b, h, i: (0, h)),
            pl.BlockSpec((1, hps * dk), lambda b, h, i: (0, h)),
            pl.BlockSpec((1, hps * dv), lambda b, h, i: (0, h)),
        ],
        out_specs=pl.BlockSpec((rows, hps * dv), lambda b, h, i: (b * nblk + late_blk(i), h)),
        out_shape=jax.ShapeDtypeStruct((n, heads * dv), BF16),
        scratch_shapes=[pltpu.VMEM((hps, dk, dv), F32), pltpu.VMEM((seq, hps * dv), F32)],
        compiler_params=_params("arbitrary", "arbitrary", "arbitrary"),
        name="gla",
    )(q, k, v, lr, r, wgf, bgf, wgb, bgb, gain)


def _proj_residual_router_kernel(a_ref, w_ref, b_ref, h_ref, g_ref, rhl_ref,
                                 h_out_ref, hn_out_ref, meta_ref, count_ref, tilecnt_ref, *, n_experts):
    i = pl.program_id(0)
    h = h_ref[...] + (jnp.dot(a_ref[...], w_ref[...], preferred_element_type=F32) + b_ref[...])
    h_out_ref[...] = h
    hn = _rms(h, g_ref[...])
    hn_out_ref[...] = hn.astype(hn_out_ref.dtype)

    hn_hi = hn.astype(BF16)
    hn_lo = (hn - hn_hi.astype(F32)).astype(BF16)
    hh = jnp.dot(hn_hi, rhl_ref[...], preferred_element_type=F32)
    logits = (hh[:, :LANES] + hh[:, LANES:]
              + jnp.dot(hn_lo, rhl_ref[:, :LANES], preferred_element_type=F32))
    tm = logits.shape[0]
    lane = lax.broadcasted_iota(jnp.int32, logits.shape, 1).astype(F32)
    neg = jnp.float32(-jnp.inf)
    logits = jnp.where(lane < n_experts, logits, neg)
    m1 = jnp.max(logits, axis=-1, keepdims=True)
    i1 = jnp.min(jnp.where(logits == m1, lane, float(LANES)), axis=-1, keepdims=True)
    rest = jnp.where(lane == i1, neg, logits)
    m2 = jnp.max(rest, axis=-1, keepdims=True)
    i2 = jnp.min(jnp.where(rest == m2, lane, float(LANES)), axis=-1, keepdims=True)
    e2 = jnp.exp(m2 - m1)
    w1 = 1.0 / (1.0 + e2)
    w2 = e2 / (1.0 + e2)

    @pl.when(i == 0)
    def _():
        count_ref[...] = jnp.zeros_like(count_ref)

    sel = (lane == i1) | (lane == i2)
    onehot = jnp.where(sel, 1.0, 0.0)
    ri = lax.broadcasted_iota(jnp.int32, (tm, tm), 0)
    ci = lax.broadcasted_iota(jnp.int32, (tm, tm), 1)
    strict_lower = jnp.where(ci < ri, 1.0, 0.0).astype(BF16)
    rank = jnp.dot(strict_lower, onehot.astype(BF16), preferred_element_type=F32) + count_ref[...]
    r1 = jnp.sum(jnp.where(lane == i1, rank, 0.0), axis=-1, keepdims=True)
    r2 = jnp.sum(jnp.where(lane == i2, rank, 0.0), axis=-1, keepdims=True)
    count_ref[...] = count_ref[...] + jnp.sum(onehot, axis=0, keepdims=True)
    sub = tm // tilecnt_ref.shape[0]
    tilecnt_ref[...] = jnp.concatenate(
        [jnp.sum(onehot[u * sub:(u + 1) * sub], axis=0, keepdims=True) for u in range(tilecnt_ref.shape[0])],
        axis=0)

    meta = jnp.where(lane == 0, i1, 0.0)
    meta = jnp.where(lane == 1, i2, meta)
    meta = jnp.where(lane == 2, w1, meta)
    meta = jnp.where(lane == 3, w2, meta)
    meta = jnp.where(lane == 4, r1, meta)
    meta = jnp.where(lane == 5, r2, meta)
    meta_ref[...] = meta[:, :meta_ref.shape[1]]


def _proj_residual_router(a, w, bias, h, gain, router, name):
    n, kdim = a.shape
    d = w.shape[1]
    tm = ROW_TILE
    assert n % tm == 0 and tm % COMBINE_TILE == 0
    sub = tm // COMBINE_TILE
    n_experts = router.shape[1]
    rpad = jnp.zeros((d, LANES), F32).at[:, :n_experts].set(router)
    rhi = rpad.astype(BF16)
    rlo = (rpad - rhi.astype(F32)).astype(BF16)
    row = lambda i: (i, 0)
    fixed = lambda i: (0, 0)
    return pl.pallas_call(
        functools.partial(_proj_residual_router_kernel, n_experts=n_experts),
        grid=(n // tm,),
        in_specs=[
            pl.BlockSpec((tm, kdim), row),
            pl.BlockSpec((kdim, d), fixed),
            pl.BlockSpec((1, d), fixed),
            pl.BlockSpec((tm, d), row),
            pl.BlockSpec((1, d), fixed),
            pl.BlockSpec((d, 2 * LANES), fixed),
        ],
        out_specs=[pl.BlockSpec((tm, d), row), pl.BlockSpec((tm, d), row),
                   pl.BlockSpec((tm, META_WIDTH), row), pl.BlockSpec((1, LANES), fixed),
                   pl.BlockSpec((None, sub, LANES), lambda i: (i, 0, 0))],
        out_shape=[jax.ShapeDtypeStruct((n, d), F32), jax.ShapeDtypeStruct((n, d), BF16),
                   jax.ShapeDtypeStruct((n, META_WIDTH), F32), jax.ShapeDtypeStruct((1, LANES), F32),
                   jax.ShapeDtypeStruct((n // tm, sub, LANES), F32)],
        compiler_params=_params("arbitrary"),
        name=name,
    )(a, w, bias.reshape(1, d), h, gain.reshape(1, d), jnp.concatenate([rhi, rlo], axis=1))


def _swiglu_step(x, wgu_ref, wd_ref, acc_ref):
    tf = wd_ref.shape[0]
    gu = jnp.dot(x, wgu_ref[...], preferred_element_type=F32)
    a = (_silu(gu[:, :tf]) * gu[:, tf:]).astype(BF16)
    acc_ref[...] += jnp.dot(a, wd_ref[...], preferred_element_type=F32)


def _moe_swiglu_kernel(te_ref, nu_ref, x_ref, wgu_ref, wd_ref, o_ref, acc_ref):
    i = pl.program_id(0)
    j = pl.program_id(1)
    used = i < nu_ref[0]

    @pl.when(used & (j == 0))
    def _():
        acc_ref[...] = jnp.zeros_like(acc_ref)

    @pl.when(used)
    def _():
        _swiglu_step(x_ref[...].astype(BF16), wgu_ref, wd_ref, acc_ref)

    @pl.when(j == pl.num_programs(1) - 1)
    def _():
        @pl.when(used)
        def _():
            o_ref[...] = acc_ref[...].astype(o_ref.dtype)

        @pl.when(jnp.logical_not(used))
        def _():
            o_ref[...] = jnp.zeros_like(o_ref)


def _swiglu_weights(wg, wu, wd):
    e, d, f = wg.shape
    tf = FFN_F_TILE
    assert f % tf == 0
    nf = f // tf
    tiles = []
    for j in range(nf):
        tiles += [wg[:, :, j * tf:(j + 1) * tf], wu[:, :, j * tf:(j + 1) * tf]]
    wgu = jnp.concatenate(tiles, axis=-1).astype(BF16)
    return wgu, wd.astype(BF16).reshape(e, nf, tf, d)


def _moe_swiglu(x, wgu, wd, tile_expert, n_used, tm, out_dtype):
    rows, d = x.shape
    nf, tf = wd.shape[1], wd.shape[2]
    assert rows % tm == 0

    def fcol(i, j, nu):
        return jnp.where(i < nu[0], j, nf - 1)

    def xmap(i, j, te, nu):
        return (jnp.where(i < nu[0], i, nu[0] - 1), 0)

    return pl.pallas_call(
        _moe_swiglu_kernel,
        grid_spec=pltpu.PrefetchScalarGridSpec(
            num_scalar_prefetch=2,
            grid=(rows // tm, nf),
            in_specs=[
                pl.BlockSpec((tm, d), xmap),
                pl.BlockSpec((None, d, 2 * tf), lambda i, j, te, nu: (te[i], 0, fcol(i, j, nu))),
                pl.BlockSpec((None, None, tf, d), lambda i, j, te, nu: (te[i], fcol(i, j, nu), 0, 0)),
            ],
            out_specs=pl.BlockSpec((tm, d), lambda i, j, te, nu: (i, 0)),
            scratch_shapes=[pltpu.VMEM((tm, d), F32)],
        ),
        out_shape=jax.ShapeDtypeStruct((rows, d), out_dtype),
        compiler_params=_params("arbitrary", "arbitrary"),
        name="moe_swiglu",
    )(tile_expert, n_used, x, wgu, wd)


def _proj_swiglu_kernel(a_ref, wo_ref, h_ref, g_ref, wgu_ref, wd_ref, o_ref, hres_ref, xn_ref, acc_ref):
    j = pl.program_id(1)

    @pl.when(j == 0)
    def _():
        h = h_ref[...] + jnp.dot(a_ref[...], wo_ref[...], preferred_element_type=F32)
        hres_ref[...] = h
        xn_ref[...] = _rms(h, g_ref[...]).astype(xn_ref.dtype)
        acc_ref[...] = jnp.zeros_like(acc_ref)

    _swiglu_step(xn_ref[...], wgu_ref, wd_ref, acc_ref)

    @pl.when(j == pl.num_programs(1) - 1)
    def _():
        o_ref[...] = hres_ref[...] + acc_ref[...]


def _proj_swiglu(a, wo, h, gain, wgu, wd):
    n, kdim = a.shape
    d = wo.shape[1]
    nf, tf = wd.shape[1], wd.shape[2]
    tm = FFN_TILE
    assert n % tm == 0
    return pl.pallas_call(
        _proj_swiglu_kernel,
        grid=(n // tm, nf),
        in_specs=[
            pl.BlockSpec((tm, kdim), lambda i, j: (i, 0)),
            pl.BlockSpec((kdim, d), lambda i, j: (0, 0)),
            pl.BlockSpec((tm, d), lambda i, j: (i, 0)),
            pl.BlockSpec((1, d), lambda i, j: (0, 0)),
            pl.BlockSpec((None, d, 2 * tf), lambda i, j: (0, 0, j)),
            pl.BlockSpec((None, None, tf, d), lambda i, j: (0, j, 0, 0)),
        ],
        out_specs=pl.BlockSpec((tm, d), lambda i, j: (i, 0)),
        out_shape=jax.ShapeDtypeStruct((n, d), F32),
        scratch_shapes=[pltpu.VMEM((tm, d), F32), pltpu.VMEM((tm, d), BF16), pltpu.VMEM((tm, d), F32)],
        compiler_params=_params("arbitrary", "arbitrary"),
        name="gla_out_proj_dense_swiglu",
    )(a, wo, h, gain.reshape(1, d), wgu, wd)


def _alibi_slope(head):
    return float(2.0 ** (-8.0 * (head + 1) / N_Q_HEADS))


def _swa_bias(t):
    qi = jnp.arange(t)[:, None]
    kj = jnp.arange(3 * t)[None, :]
    dist = jnp.abs(qi + t - kj)
    band = dist <= WINDOW
    valid = jnp.stack([band & (kj >= t), band, band & (kj < 2 * t)])
    slopes = jnp.asarray([_alibi_slope(h) for h in range(N_Q_HEADS)], F32)
    bias = -slopes[None, :, None, None] * dist.astype(F32)[None, None]
    bias = jnp.where(valid[:, None], bias, -jnp.inf)
    return bias.reshape(3, N_KV_HEADS, GROUP * t, 3 * t)


def _swa_kernel(sink_ref, q_ref, kp_ref, kc_ref, kn_ref, vp_ref, vc_ref, vn_ref, bias0_ref, bias1_ref,
                o_ref):
    t = ATT_BLOCK
    lane = lax.broadcasted_iota(jnp.int32, (t, LANES), 1)
    low = lane < HEAD_DIM
    row_head = lax.broadcasted_iota(jnp.int32, (GROUP * t, 1), 0) // t
    ones = jnp.ones((3 * t, LANES), BF16)

    def window(p_ref, c_ref, n_ref, kvh, sub):
        ks = slice(kvh * LANES, (kvh + 1) * LANES)
        if sub == 0:
            return jnp.concatenate([p_ref[t:, ks], c_ref[:, ks]], axis=0)
        return jnp.concatenate([c_ref[:, ks], n_ref[:t, ks]], axis=0)

    row_blocks = []
    for sub, bias_ref in enumerate((bias0_ref, bias1_ref)):
        rows = slice(sub * t, (sub + 1) * t)
        blocks = []
        for kvh in range(N_KV_HEADS):
            qs = []
            for g in range(GROUP):
                head = kvh * GROUP + g
                col = (head // 2) * LANES
                q2 = q_ref[rows, col:col + LANES]
                qs.append(jnp.where(low if head % 2 == 0 else jnp.logical_not(low), q2,
                                    jnp.zeros_like(q2)))
            qg = jnp.concatenate(qs, axis=0)
            kw = window(kp_ref, kc_ref, kn_ref, kvh, sub)
            s = lax.dot_general(qg, kw, (((1,), (1,)), ((), ())), preferred_element_type=F32)
            s = s + bias_ref[kvh]
            sink = jnp.full((GROUP * t, 1), sink_ref[kvh * GROUP], F32)
            for g in range(1, GROUP):
                sink = jnp.where(row_head == g, sink_ref[kvh * GROUP + g], sink)
            m = jnp.maximum(jnp.max(s, axis=-1, keepdims=True), sink)
            p = jnp.exp(s - m).astype(BF16)
            vw = jnp.concatenate([window(vp_ref, vc_ref, vn_ref, kvh, sub), ones], axis=1)
            res = jnp.dot(p, vw, preferred_element_type=F32)
            out = res[:, :LANES] * (1.0 / (res[:, LANES:] + jnp.exp(sink - m)))
            for pair in range(GROUP // 2):
                a = out[(2 * pair) * t:(2 * pair + 1) * t]
                b = out[(2 * pair + 1) * t:(2 * pair + 2) * t]
                blocks.append(jnp.where(low, a, b).astype(o_ref.dtype))
        row_blocks.append(jnp.concatenate(blocks, axis=1))
    o_ref[...] = jnp.concatenate(row_blocks, axis=0)


def _swa(q, kdup, vdup, sinks, batch, seq):
    n, qw = q.shape
    t = ATT_BLOCK
    rows = 2 * t
    nblk = seq // rows
    assert seq % rows == 0 and nblk >= 2
    kvw = kdup.shape[1]
    bias = _swa_bias(t)
    prev = lambda b, i, s: (b * nblk + jnp.maximum(i - 1, 0), 0)
    cur = lambda b, i, s: (b * nblk + i, 0)
    nxt = lambda b, i, s: (b * nblk + jnp.minimum(i + 1, nblk - 1), 0)
    first = lambda b, i, s: (jnp.where(i == 0, 0, 1), 0, 0, 0)
    second = lambda b, i, s: (jnp.where(i == nblk - 1, 2, 1), 0, 0, 0)
    return pl.pallas_call(
        _swa_kernel,
        grid_spec=pltpu.PrefetchScalarGridSpec(
            num_scalar_prefetch=1,
            grid=(batch, nblk),
            in_specs=[
                pl.BlockSpec((rows, qw), cur),
                pl.BlockSpec((rows, kvw), prev), pl.BlockSpec((rows, kvw), cur), pl.BlockSpec((rows, kvw), nxt),
                pl.BlockSpec((rows, kvw), prev), pl.BlockSpec((rows, kvw), cur), pl.BlockSpec((rows, kvw), nxt),
                pl.BlockSpec((None,) + bias.shape[1:], first),
                pl.BlockSpec((None,) + bias.shape[1:], second),
            ],
            out_specs=pl.BlockSpec((rows, qw), cur),
        ),
        out_shape=jax.ShapeDtypeStruct((n, qw), BF16),
        compiler_params=_params("arbitrary", "arbitrary"),
        name="swa",
    )(sinks, q, kdup, kdup, kdup, vdup, vdup, vdup, bias, bias)


SEG_ALIGN = 8
SEG_BIG = 64
RUN_FIELDS = 7


def _block_rows(tm, n_experts):
    worst = TOP_K * tm + 2 * (SEG_ALIGN - 1) * n_experts
    return -(-worst // LANES) * LANES


def _pair_block_rows(meta, base_ref, tile, n_experts):
    row1, row2 = meta[:, 4:5], meta[:, 5:6]
    for e in range(n_experts):
        base = base_ref[tile * n_experts + e]
        row1 = row1 + jnp.where(meta[:, 0:1] == float(e), base, 0.0)
        row2 = row2 + jnp.where(meta[:, 1:2] == float(e), base, 0.0)
    return row1, row2


def _for_pieces(n_rows, make_copy, act):
    n_big = lax.shift_right_logical(n_rows, 6)
    n_small = lax.shift_right_logical(n_rows - n_big * SEG_BIG, 3)

    def big(j, carry):
        act(make_copy(j * SEG_BIG, SEG_BIG))
        return carry

    def small(j, carry):
        act(make_copy(n_big * SEG_BIG + j * SEG_ALIGN, SEG_ALIGN))
        return carry

    lax.fori_loop(0, n_big, big, 0)
    lax.fori_loop(0, n_small, small, 0)


def _for_rows(first, count, make_copy, act):
    def one(j, carry):
        act(make_copy(first + j))
        return carry

    lax.fori_loop(0, count, one, 0)


def _start(copy):
    copy.start()


def _wait(copy):
    copy.wait()


def _scatter_kernel(runs_ref, base_ref, pads_ref, x_ref, meta_ref, xs_ref, blk_ref, zero_ref, sems,
                    zero_sem, *, n_experts):
    i = pl.program_id(0)
    steps = pl.num_programs(0)
    tm = x_ref.shape[0]
    rows = blk_ref.shape[1]
    n_runs = steps * n_experts

    def tile_copies(tile, act):
        slot = lax.rem(tile, 2)
        for e in range(n_experts):
            k = tile * n_experts + e
            head0, head_n, mid0, mid_len, tail0, tail_n, shift = [runs_ref[f * n_runs + k]
                                                                  for f in range(RUN_FIELDS)]

            def row_copy(r, shift=shift):
                return pltpu.make_async_copy(blk_ref.at[slot, pl.ds(r - shift, 1)],
                                             xs_ref.at[pl.ds(r, 1)], sems.at[slot])

            def piece(off, size, mid0=mid0, shift=shift):
                src = pl.multiple_of(mid0 - shift + off, SEG_ALIGN)
                dst = pl.multiple_of(mid0 + off, SEG_ALIGN)
                return pltpu.make_async_copy(blk_ref.at[slot, pl.ds(src, size)],
                                             xs_ref.at[pl.ds(dst, size)], sems.at[slot])

            _for_rows(head0, head_n, row_copy, act)
            _for_pieces(mid_len, piece, act)
            _for_rows(tail0, tail_n, row_copy, act)

    @pl.when(i == 0)
    def _():
        zero_ref[...] = jnp.zeros_like(zero_ref)
        for act in (_start, _wait):
            for e in range(n_experts):
                pad0, head_n, mid_len = [pads_ref[f * n_experts + e] for f in range(3)]

                def zero_row(r):
                    return pltpu.make_async_copy(zero_ref.at[pl.ds(0, 1)], xs_ref.at[pl.ds(r, 1)], zero_sem)

                def zero_piece(off, size, pad0=pad0, head_n=head_n):
                    dst = pl.multiple_of(pad0 + head_n + off, SEG_ALIGN)
                    return pltpu.make_async_copy(zero_ref.at[pl.ds(0, size)], xs_ref.at[pl.ds(dst, size)],
                                                 zero_sem)

                _for_rows(pad0, head_n, zero_row, act)
                _for_pieces(mid_len, zero_piece, act)

    @pl.when(i >= 2)
    def _():
        tile_copies(i - 2, _wait)

    row1, row2 = _pair_block_rows(meta_ref[...], base_ref, i, n_experts)
    col = lax.broadcasted_iota(jnp.int32, (tm, rows), 1).astype(F32)
    pick = jnp.where((col == row1) | (col == row2), 1.0, 0.0).astype(BF16)
    blk_ref[lax.rem(i, 2)] = lax.dot_general(pick, x_ref[...], (((0,), (0,)), ((), ())),
                                             preferred_element_type=F32)
    tile_copies(i, _start)

    @pl.when(i == steps - 1)
    def _():
        @pl.when(i >= 1)
        def _():
            tile_copies(i - 1, _wait)

        tile_copies(i, _wait)


def _scatter_rows(x, meta, runs, row_base, pads, total_rows, n_experts):
    n, d = x.shape
    tm = COMBINE_TILE
    assert n % tm == 0
    rows = _block_rows(tm, n_experts)
    smem = pl.BlockSpec(memory_space=pltpu.SMEM)
    return pl.pallas_call(
        functools.partial(_scatter_kernel, n_experts=n_experts),
        grid=(n // tm,),
        in_specs=[
            smem, smem, smem,
            pl.BlockSpec((tm, d), lambda i: (i, 0)),
            pl.BlockSpec((tm, META_WIDTH), lambda i: (i, 0)),
        ],
        out_specs=pl.BlockSpec(memory_space=pl.ANY),
        out_shape=jax.ShapeDtypeStruct((total_rows, d), F32),
        scratch_shapes=[pltpu.VMEM((2, rows, d), F32), pltpu.VMEM((SEG_BIG, d), F32),
                        pltpu.SemaphoreType.DMA((2,)), pltpu.SemaphoreType.DMA],
        compiler_params=_params("arbitrary"),
        name="moe_scatter",
    )(runs, row_base, pads, x, meta)


def _combine_kernel(src_ref, len_ref, dst_ref, base_ref, h_ref, meta_ref, g_ref, ys_ref, o_ref, yblk_ref,
                    sems, *, n_experts):
    i = pl.program_id(0)
    steps = pl.num_programs(0)
    tm = h_ref.shape[0]
    rows = yblk_ref.shape[1]

    def run_copies(tile, act):
        slot = lax.rem(tile, 2)
        for e in range(n_experts):
            k = tile * n_experts + e

            def piece(off, size, src0=src_ref[k], dst0=dst_ref[k]):
                src = pl.multiple_of(src0 + off, SEG_ALIGN)
                dst = pl.multiple_of(dst0 + off, SEG_ALIGN)
                return pltpu.make_async_copy(ys_ref.at[pl.ds(src, size)],
                                             yblk_ref.at[slot, pl.ds(dst, size)], sems.at[slot])

            _for_pieces(len_ref[k], piece, act)

    @pl.when(i == 0)
    def _():
        yblk_ref[...] = jnp.zeros_like(yblk_ref)
        run_copies(i, _start)

    @pl.when(i + 1 < steps)
    def _():
        run_copies(i + 1, _start)

    run_copies(i, _wait)

    meta = meta_ref[...]
    w1, w2 = meta[:, 2:3], meta[:, 3:4]
    row1, row2 = _pair_block_rows(meta, base_ref, i, n_experts)
    col = lax.broadcasted_iota(jnp.int32, (tm, rows), 1).astype(F32)
    pick = jnp.concatenate([jnp.where(col == row1, 1.0, 0.0),
                            jnp.where(col == row2, 1.0, 0.0)], axis=0).astype(BF16)
    yblk = yblk_ref[lax.rem(i, 2)].astype(BF16)
    y = jnp.dot(pick, yblk, preferred_element_type=F32)
    o_ref[...] = _rms(h_ref[...] + (w1 * y[:tm] + w2 * y[tm:]), g_ref[...])


def _combine(h, meta, gain, ys, seg_src, seg_len, seg_dst, row_base, n_experts):
    n, d = h.shape
    tm = COMBINE_TILE
    assert n % tm == 0
    rows = _block_rows(tm, n_experts)
    smem = pl.BlockSpec(memory_space=pltpu.SMEM)
    return pl.pallas_call(
        functools.partial(_combine_kernel, n_experts=n_experts),
        grid=(n // tm,),
        in_specs=[
            smem, smem, smem, smem,
            pl.BlockSpec((tm, d), lambda i: (i, 0)),
            pl.BlockSpec((tm, META_WIDTH), lambda i: (i, 0)),
            pl.BlockSpec((1, d), lambda i: (0, 0)),
            pl.BlockSpec(memory_space=pl.ANY),
        ],
        out_specs=pl.BlockSpec((tm, d), lambda i: (i, 0)),
        out_shape=jax.ShapeDtypeStruct((n, d), F32),
        scratch_shapes=[pltpu.VMEM((2, rows, d), ys.dtype), pltpu.SemaphoreType.DMA((2,))],
        compiler_params=_params("arbitrary"),
        name="moe_combine",
    )(seg_src, seg_len, seg_dst, row_base, h, meta, gain.reshape(1, d), ys)


def kernel(x, mix_norm, ffn_norm, gla_in_proj, gla_gate_w_fwd, gla_gate_b_fwd, gla_gate_w_bwd, gla_gate_b_bwd, gla_head_norm, gla_out_proj, swa_qkv_proj, swa_qkv_bias, swa_sinks, swa_out_proj, swa_out_bias, dense_w_gate, dense_w_up, dense_w_down, moe_router, moe_w_gate, moe_w_up, moe_w_down, final_norm):
    batch, seq, d = x.shape
    n = batch * seq
    h0 = x.reshape(n, d)

    key_w = gla_gate_w_fwd.shape[2]
    val_w = gla_head_norm.shape[1]
    rank = GLA_GATE_RANK
    in_w = gla_in_proj.shape[2]
    splits = ((0, key_w), (key_w, key_w), (2 * key_w, val_w), (2 * key_w + val_w, val_w),
              (2 * key_w + 2 * val_w, 2 * rank))
    q, k, v, r, lr = _norm_proj(h0, mix_norm[0], gla_in_proj[0].astype(BF16), jnp.zeros((in_w,), F32),
                                splits, (BF16, BF16, BF16, BF16, F32), "gla_in_proj")
    zero_gate = jnp.zeros((rank, key_w), F32)
    wgf = jnp.concatenate([gla_gate_w_fwd[0], zero_gate], axis=0).astype(BF16)
    wgb = jnp.concatenate([zero_gate, gla_gate_w_bwd[0]], axis=0).astype(BF16)
    og = _gla(q, k, v, lr, r, wgf, gla_gate_b_fwd[0].reshape(1, key_w), wgb,
              gla_gate_b_bwd[0].reshape(1, key_w), gla_head_norm[0].reshape(1, val_w), batch, seq)

    dense_wgu, dense_wd = _swiglu_weights(dense_w_gate, dense_w_up, dense_w_down)
    h2 = _proj_swiglu(og, gla_out_proj[0].astype(BF16), h0, ffn_norm[0], dense_wgu, dense_wd)

    qw = N_Q_HEADS * HEAD_DIM
    kvw = N_KV_HEADS * HEAD_DIM
    scale = HEAD_DIM ** -0.5
    wqkv, bqkv = swa_qkv_proj[0], swa_qkv_bias[0]

    def dup(m):
        lead = m.shape[:-1]
        m = m.reshape(lead + (N_KV_HEADS, 1, HEAD_DIM))
        return jnp.broadcast_to(m, lead + (N_KV_HEADS, 2, HEAD_DIM)).reshape(lead + (2 * kvw,))

    w_aug = jnp.concatenate([wqkv[:, :qw] * scale, dup(wqkv[:, qw:qw + kvw]), dup(wqkv[:, qw + kvw:])], axis=1)
    b_aug = jnp.concatenate([bqkv[:qw] * scale, dup(bqkv[qw:qw + kvw]), dup(bqkv[qw + kvw:])], axis=0)
    splits = ((0, qw), (qw, 2 * kvw), (qw + 2 * kvw, 2 * kvw))
    aq, ak, av = _norm_proj(h2, mix_norm[1], w_aug.astype(BF16), b_aug, splits, (BF16, BF16, BF16),
                            "swa_qkv_proj")
    oa = _swa(aq, ak, av, swa_sinks[0], batch, seq)

    n_experts = moe_router.shape[2]
    h3, hn3, meta, counts, tile_counts = _proj_residual_router(
        oa, swa_out_proj[0].astype(BF16), swa_out_bias[0], h2, ffn_norm[1], moe_router[0], "swa_out_proj")
    tm = MOE_TILE
    counts = counts[0, :n_experts].astype(jnp.int32)
    tiles_per_expert = (counts + tm - 1) // tm
    tile_end = jnp.cumsum(tiles_per_expert)
    offsets = (tile_end - tiles_per_expert) * tm
    n_tiles = (TOP_K * n) // tm + n_experts
    tile_ids = jnp.arange(n_tiles, dtype=jnp.int32)
    tile_expert = jnp.sum((tile_ids[:, None] >= tile_end[None, :]).astype(jnp.int32), axis=1)
    tile_expert = jnp.minimum(tile_expert, n_experts - 1)
    n_used = tile_end[-1:].astype(jnp.int32)

    align = SEG_ALIGN
    tile_cnt = tile_counts.reshape(-1, LANES)[:, :n_experts].astype(jnp.int32)
    run_start = offsets[None, :] + jnp.cumsum(tile_cnt, axis=0) - tile_cnt
    run_end = run_start + tile_cnt
    seg_src = run_start - run_start % align
    seg_len = jnp.where(tile_cnt > 0, (run_end + align - 1) // align * align - seg_src, 0)
    seg_dst = jnp.cumsum(seg_len, axis=1) - seg_len
    row_base = (offsets[None, :] + seg_dst - seg_src).astype(F32).reshape(-1)
    mid_start = (run_start + align - 1) // align * align
    mid_end = run_end - run_end % align
    has_mid = mid_end >= mid_start
    runs = jnp.stack([run_start, jnp.where(has_mid, mid_start - run_start, tile_cnt),
                      mid_start, jnp.where(has_mid, mid_end - mid_start, 0),
                      mid_end, jnp.where(has_mid, run_end - mid_end, 0),
                      seg_src - seg_dst]).reshape(-1)
    pad_start = offsets + counts
    pad_count = tiles_per_expert * tm - counts
    pad_head = jnp.minimum(pad_count, (align - pad_start % align) % align)
    pads = jnp.stack([pad_start, pad_head, pad_count - pad_head]).reshape(-1)

    xs = _scatter_rows(hn3, meta, runs, row_base, pads, n_tiles * tm, n_experts)
    moe_wgu, moe_wd = _swiglu_weights(moe_w_gate[0], moe_w_up[0], moe_w_down[0])
    ys = _moe_swiglu(xs, moe_wgu, moe_wd, tile_expert, n_used, tm, F32)
    out = _combine(h3, meta, final_norm, ys, seg_src.reshape(-1), seg_len.reshape(-1),
                   seg_dst.reshape(-1), row_base, n_experts)
    return out.reshape(batch, seq, d)
```

```python
import functools

import jax
import jax.numpy as jnp
from jax import lax
from jax.experimental import pallas as pl
from jax.experimental.pallas import tpu as pltpu

F32 = jnp.float32
BF16 = jnp.bfloat16

NORM_EPS = 1e-5

GLA_HEADS = 4
GLA_GATE_RANK = 16
GLA_GATE_TAU = 16.0
GLA_CHUNK = 64
N_Q_HEADS = 16
N_KV_HEADS = 4
HEAD_DIM = 64
GROUP = N_Q_HEADS // N_KV_HEADS
WINDOW = 128
ATT_BLOCK = 128
TOP_K = 2

LANES = 128
VMEM_LIMIT_BYTES = 56 * 2**20

ROW_TILE = 512
GLA_BLOCK = 512
GLA_GROUP = 256
GLA_HEADS_PER_STEP = 4
FFN_TILE = 1024
FFN_F_TILE = 256
MOE_TILE = 1024
COMBINE_TILE = 256
META_WIDTH = 8


def _params(*sem):
    return pltpu.CompilerParams(dimension_semantics=sem, vmem_limit_bytes=VMEM_LIMIT_BYTES)


def _rms(x, gain):
    y = x * lax.rsqrt(jnp.mean(x * x, axis=-1, keepdims=True) + NORM_EPS)
    return y * gain


def _silu(x):
    return x * (1.0 / (1.0 + jnp.exp(-x)))


def _norm_proj_kernel(x_ref, g_ref, w_ref, b_ref, *o_refs, splits):
    y = _rms(x_ref[...], g_ref[...]).astype(BF16)
    for (start, width), o_ref in zip(splits, o_refs):
        acc = jnp.dot(y, w_ref[:, start:start + width], preferred_element_type=F32)
        acc = acc + b_ref[:, start:start + width]
        o_ref[...] = acc.astype(o_ref.dtype)


def _norm_proj(x, gain, w, bias, splits, dtypes, name):
    n, d = x.shape
    nout = w.shape[1]
    tm = ROW_TILE
    assert n % tm == 0
    return pl.pallas_call(
        functools.partial(_norm_proj_kernel, splits=splits),
        grid=(n // tm,),
        in_specs=[
            pl.BlockSpec((tm, d), lambda i: (i, 0)),
            pl.BlockSpec((1, d), lambda i: (0, 0)),
            pl.BlockSpec((d, nout), lambda i: (0, 0)),
            pl.BlockSpec((1, nout), lambda i: (0, 0)),
        ],
        out_specs=[pl.BlockSpec((tm, wd), lambda i: (i, 0)) for (_, wd) in splits],
        out_shape=[jax.ShapeDtypeStruct((n, wd), dt) for (_, wd), dt in zip(splits, dtypes)],
        compiler_params=_params("arbitrary"),
        name=name,
    )(x, gain.reshape(1, d), w, bias.reshape(1, nout))


def _gla_direction(q_ref, k_ref, v_ref, lr_ref, wg_ref, bg_ref, state_ref, reverse):
    heads, dk, dv = state_ref.shape
    rows, width = q_ref.shape
    c = GLA_CHUNK
    nc = rows // c
    q = q_ref[...].astype(F32) * (dk ** -0.5)
    k = k_ref[...].astype(F32)
    v = v_ref[...]
    z = jnp.dot(lr_ref[...].astype(BF16), wg_ref[...], preferred_element_type=F32) + bg_ref[...]
    la = (jnp.minimum(z, 0.0) - jnp.log(1.0 + jnp.exp(-jnp.abs(z)))) * (1.0 / GLA_GATE_TAU)
    la_hi = la.astype(BF16)
    la_lo = (la - la_hi.astype(F32)).astype(BF16)

    tn = (((0,), (0,)), ((), ()))
    nt = (((1,), (1,)), ((), ()))
    grp = GLA_GROUP
    ng = rows // grp
    ri = lax.broadcasted_iota(jnp.int32, (grp, grp), 0)
    ci = lax.broadcasted_iota(jnp.int32, (grp, grp), 1)
    same_chunk = (ri // c) == (ci // c)
    if reverse:
        cum_mask = same_chunk & (ci >= ri)
        att_mask = same_chunk & (ci > ri)
        ref_row, last_row = c // 2 - 1, 0
    else:
        cum_mask = same_chunk & (ci <= ri)
        att_mask = cum_mask
        ref_row, last_row = c // 2, c - 1
    cum = jnp.where(cum_mask, 1.0, 0.0).astype(BF16)

    la_hl = jnp.concatenate([la_hi, la_lo], axis=1)
    b = jnp.concatenate(
        [jnp.dot(cum, la_hl[g * grp:(g + 1) * grp], preferred_element_type=F32) for g in range(ng)],
        axis=0)
    b = b[:, :width] + b[:, width:]
    b3 = b.reshape(nc, c, width)
    b_ref = b3[:, ref_row:ref_row + 1, :]
    b_last = b3[:, last_row:last_row + 1, :]
    q3 = q.reshape(nc, c, width)
    k3 = k.reshape(nc, c, width)
    qe = (q3 * jnp.exp(b3 - b_ref)).astype(BF16).reshape(rows, width)
    ke = (k3 * jnp.exp(b_ref - b3)).astype(BF16).reshape(rows, width)
    kd = (k3 * jnp.exp(b_last - b3)).astype(BF16).reshape(rows, width)
    qb = (q3 * jnp.exp(b3)).astype(BF16).reshape(rows, width)
    decay_rows = jnp.exp(b_last.reshape(nc, width))

    outs = []
    for h in range(heads):
        ks = slice(h * dk, (h + 1) * dk)
        vh = v[:, h * dv:(h + 1) * dv]
        o_intra = []
        for g in range(ng):
            sl = slice(g * grp, (g + 1) * grp)
            s = lax.dot_general(qe[sl, ks], ke[sl, ks], nt, preferred_element_type=F32)
            s = jnp.where(att_mask, s, 0.0).astype(BF16)
            o_intra.append(jnp.dot(s, vh[sl], preferred_element_type=F32))
        o_intra = jnp.concatenate(o_intra, axis=0)

        upd = [lax.dot_general(kd[j * c:(j + 1) * c, ks], vh[j * c:(j + 1) * c], tn,
                               preferred_element_type=F32) for j in range(nc)]
        decay_cols = jnp.concatenate([decay_rows[:, ks], jnp.zeros((dk - nc, dk), F32)], axis=0).T

        state = state_ref[h]
        o_inter = [None] * nc
        for j in (range(nc - 1, -1, -1) if reverse else range(nc)):
            o_inter[j] = jnp.dot(qb[j * c:(j + 1) * c, ks], state.astype(BF16),
                                 preferred_element_type=F32)
            state = state * decay_cols[:, j:j + 1] + upd[j]
        state_ref[h] = state
        outs.append(o_intra + jnp.concatenate(o_inter, axis=0))
    return jnp.concatenate(outs, axis=1)


def _gla_kernel(q_ref, k_ref, v_ref, lr_ref, r_ref, wgf_ref, bgf_ref, wgb_ref, bgb_ref, gain_ref,
                o_ref, state_ref, oacc_ref, *, nblk):
    i = pl.program_id(2)
    rows = q_ref.shape[0]
    heads, _, dv = state_ref.shape

    @pl.when((i == 0) | (i == nblk))
    def _():
        state_ref[...] = jnp.zeros_like(state_ref)

    @pl.when(i < nblk)
    def _():
        o = _gla_direction(q_ref, k_ref, v_ref, lr_ref, wgf_ref, bgf_ref, state_ref, False)
        oacc_ref[pl.ds(pl.multiple_of(i * rows, rows), rows), :] = o

    @pl.when(i >= nblk)
    def _():
        j = 2 * nblk - 1 - i
        o = _gla_direction(q_ref, k_ref, v_ref, lr_ref, wgb_ref, bgb_ref, state_ref, True)
        o = o + oacc_ref[pl.ds(pl.multiple_of(j * rows, rows), rows), :]
        gain = gain_ref[...]
        o = jnp.concatenate([_rms(o[:, h * dv:(h + 1) * dv], gain[:, h * dv:(h + 1) * dv])
                             for h in range(heads)], axis=1)
        o_ref[...] = (o * _silu(r_ref[...].astype(F32))).astype(o_ref.dtype)


def _gla(q, k, v, lr, r, wgf, bgf, wgb, bgb, gain, batch, seq):
    n = q.shape[0]
    heads = GLA_HEADS
    hps = GLA_HEADS_PER_STEP
    dk = q.shape[1] // heads
    dv = v.shape[1] // heads
    rows = GLA_BLOCK
    nblk = seq // rows
    assert seq % rows == 0 and rows % GLA_GROUP == 0 and GLA_GROUP % GLA_CHUNK == 0 and heads % hps == 0

    def blk(i):
        return jnp.where(i < nblk, i, 2 * nblk - 1 - i)

    def late_blk(i):
        return jnp.where(i < nblk, nblk - 1, 2 * nblk - 1 - i)

    row_map = lambda b, h, i: (b * nblk + blk(i), h)
    return pl.pallas_call(
        functools.partial(_gla_kernel, nblk=nblk),
        grid=(batch, heads // hps, 2 * nblk),
        in_specs=[
            pl.BlockSpec((rows, hps * dk), row_map),
            pl.BlockSpec((rows, hps * dk), row_map),
            pl.BlockSpec((rows, hps * dv), row_map),
            pl.BlockSpec((rows, lr.shape[1]), lambda b, h, i: (b * nblk + blk(i), 0)),
            pl.BlockSpec((rows, hps * dv), lambda b, h, i: (b * nblk + late_blk(i), h)),
            pl.BlockSpec((wgf.shape[0], hps * dk), lambda b, h, i: (0, h)),
            pl.BlockSpec((1, hps * dk), lambda b, h, i: (0, h)),
            pl.BlockSpec((wgb.shape[0], hps * dk), lambda b, h, i: (0, h)),
            pl.BlockSpec((1, hps * dk), lambda b, h, i: (0, h)),
            pl.BlockSpec((1, hps * dv), lambda b, h, i: (0, h)),
        ],
        out_specs=pl.BlockSpec((rows, hps * dv), lambda b, h, i: (b * nblk + late_blk(i), h)),
        out_shape=jax.ShapeDtypeStruct((n, heads * dv), BF16),
        scratch_shapes=[pltpu.VMEM((hps, dk, dv), F32), pltpu.VMEM((seq, hps * dv), F32)],
        compiler_params=_params("arbitrary", "arbitrary", "arbitrary"),
        name="gla",
    )(q, k, v, lr, r, wgf, bgf, wgb, bgb, gain)


def _proj_residual_router_kernel(a_ref, w_ref, b_ref, h_ref, g_ref, rhl_ref,
                                 h_out_ref, hn_out_ref, meta_ref, count_ref, tilecnt_ref, *, n_experts):
    i = pl.program_id(0)
    h = h_ref[...] + (jnp.dot(a_ref[...], w_ref[...], preferred_element_type=F32) + b_ref[...])
    h_out_ref[...] = h
    hn = _rms(h, g_ref[...])
    hn_out_ref[...] = hn.astype(hn_out_ref.dtype)

    hn_hi = hn.astype(BF16)
    hn_lo = (hn - hn_hi.astype(F32)).astype(BF16)
    hh = jnp.dot(hn_hi, rhl_ref[...], preferred_element_type=F32)
    logits = (hh[:, :LANES] + hh[:, LANES:]
              + jnp.dot(hn_lo, rhl_ref[:, :LANES], preferred_element_type=F32))
    tm = logits.shape[0]
    lane = lax.broadcasted_iota(jnp.int32, logits.shape, 1).astype(F32)
    neg = jnp.float32(-jnp.inf)
    logits = jnp.where(lane < n_experts, logits, neg)
    m1 = jnp.max(logits, axis=-1, keepdims=True)
    i1 = jnp.min(jnp.where(logits == m1, lane, float(LANES)), axis=-1, keepdims=True)
    rest = jnp.where(lane == i1, neg, logits)
    m2 = jnp.max(rest, axis=-1, keepdims=True)
    i2 = jnp.min(jnp.where(rest == m2, lane, float(LANES)), axis=-1, keepdims=True)
    e2 = jnp.exp(m2 - m1)
    w1 = 1.0 / (1.0 + e2)
    w2 = e2 / (1.0 + e2)

    @pl.when(i == 0)
    def _():
        count_ref[...] = jnp.zeros_like(count_ref)

    sel = (lane == i1) | (lane == i2)
    onehot = jnp.where(sel, 1.0, 0.0)
    ri = lax.broadcasted_iota(jnp.int32, (tm, tm), 0)
    ci = lax.broadcasted_iota(jnp.int32, (tm, tm), 1)
    strict_lower = jnp.where(ci < ri, 1.0, 0.0).astype(BF16)
    rank = jnp.dot(strict_lower, onehot.astype(BF16), preferred_element_type=F32) + count_ref[...]
    r1 = jnp.sum(jnp.where(lane == i1, rank, 0.0), axis=-1, keepdims=True)
    r2 = jnp.sum(jnp.where(lane == i2, rank, 0.0), axis=-1, keepdims=True)
    count_ref[...] = count_ref[...] + jnp.sum(onehot, axis=0, keepdims=True)
    sub = tm // tilecnt_ref.shape[0]
    tilecnt_ref[...] = jnp.concatenate(
        [jnp.sum(onehot[u * sub:(u + 1) * sub], axis=0, keepdims=True) for u in range(tilecnt_ref.shape[0])],
        axis=0)

    meta = jnp.where(lane == 0, i1, 0.0)
    meta = jnp.where(lane == 1, i2, meta)
    meta = jnp.where(lane == 2, w1, meta)
    meta = jnp.where(lane == 3, w2, meta)
    meta = jnp.where(lane == 4, r1, meta)
    meta = jnp.where(lane == 5, r2, meta)
    meta_ref[...] = meta[:, :meta_ref.shape[1]]


def _proj_residual_router(a, w, bias, h, gain, router, name):
    n, kdim = a.shape
    d = w.shape[1]
    tm = ROW_TILE
    assert n % tm == 0 and tm % COMBINE_TILE == 0
    sub = tm // COMBINE_TILE
    n_experts = router.shape[1]
    rpad = jnp.zeros((d, LANES), F32).at[:, :n_experts].set(router)
    rhi = rpad.astype(BF16)
    rlo = (rpad - rhi.astype(F32)).astype(BF16)
    row = lambda i: (i, 0)
    fixed = lambda i: (0, 0)
    return pl.pallas_call(
        functools.partial(_proj_residual_router_kernel, n_experts=n_experts),
        grid=(n // tm,),
        in_specs=[
            pl.BlockSpec((tm, kdim), row),
            pl.BlockSpec((kdim, d), fixed),
            pl.BlockSpec((1, d), fixed),
            pl.BlockSpec((tm, d), row),
            pl.BlockSpec((1, d), fixed),
            pl.BlockSpec((d, 2 * LANES), fixed),
        ],
        out_specs=[pl.BlockSpec((tm, d), row), pl.BlockSpec((tm, d), row),
                   pl.BlockSpec((tm, META_WIDTH), row), pl.BlockSpec((1, LANES), fixed),
                   pl.BlockSpec((None, sub, LANES), lambda i: (i, 0, 0))],
        out_shape=[jax.ShapeDtypeStruct((n, d), F32), jax.ShapeDtypeStruct((n, d), BF16),
                   jax.ShapeDtypeStruct((n, META_WIDTH), F32), jax.ShapeDtypeStruct((1, LANES), F32),
                   jax.ShapeDtypeStruct((n // tm, sub, LANES), F32)],
        compiler_params=_params("arbitrary"),
        name=name,
    )(a, w, bias.reshape(1, d), h, gain.reshape(1, d), jnp.concatenate([rhi, rlo], axis=1))


def _swiglu_step(x, wgu_ref, wd_ref, acc_ref):
    tf = wd_ref.shape[0]
    gu = jnp.dot(x, wgu_ref[...], preferred_element_type=F32)
    a = (_silu(gu[:, :tf]) * gu[:, tf:]).astype(BF16)
    acc_ref[...] += jnp.dot(a, wd_ref[...], preferred_element_type=F32)


def _moe_swiglu_kernel(te_ref, nu_ref, x_ref, wgu_ref, wd_ref, o_ref, acc_ref):
    i = pl.program_id(0)
    j = pl.program_id(1)
    used = i < nu_ref[0]

    @pl.when(used & (j == 0))
    def _():
        acc_ref[...] = jnp.zeros_like(acc_ref)

    @pl.when(used)
    def _():
        _swiglu_step(x_ref[...].astype(BF16), wgu_ref, wd_ref, acc_ref)

    @pl.when(j == pl.num_programs(1) - 1)
    def _():
        @pl.when(used)
        def _():
            o_ref[...] = acc_ref[...].astype(o_ref.dtype)

        @pl.when(jnp.logical_not(used))
        def _():
            o_ref[...] = jnp.zeros_like(o_ref)


def _swiglu_weights(wg, wu, wd):
    e, d, f = wg.shape
    tf = FFN_F_TILE
    assert f % tf == 0
    nf = f // tf
    tiles = []
    for j in range(nf):
        tiles += [wg[:, :, j * tf:(j + 1) * tf], wu[:, :, j * tf:(j + 1) * tf]]
    wgu = jnp.concatenate(tiles, axis=-1).astype(BF16)
    return wgu, wd.astype(BF16).reshape(e, nf, tf, d)


def _moe_swiglu(x, wgu, wd, tile_expert, n_used, tm, out_dtype):
    rows, d = x.shape
    nf, tf = wd.shape[1], wd.shape[2]
    assert rows % tm == 0

    def fcol(i, j, nu):
        return jnp.where(i < nu[0], j, nf - 1)

    def xmap(i, j, te, nu):
        return (jnp.where(i < nu[0], i, nu[0] - 1), 0)

    return pl.pallas_call(
        _moe_swiglu_kernel,
        grid_spec=pltpu.PrefetchScalarGridSpec(
            num_scalar_prefetch=2,
            grid=(rows // tm, nf),
            in_specs=[
                pl.BlockSpec((tm, d), xmap),
                pl.BlockSpec((None, d, 2 * tf), lambda i, j, te, nu: (te[i], 0, fcol(i, j, nu))),
                pl.BlockSpec((None, None, tf, d), lambda i, j, te, nu: (te[i], fcol(i, j, nu), 0, 0)),
            ],
            out_specs=pl.BlockSpec((tm, d), lambda i, j, te, nu: (i, 0)),
            scratch_shapes=[pltpu.VMEM((tm, d), F32)],
        ),
        out_shape=jax.ShapeDtypeStruct((rows, d), out_dtype),
        compiler_params=_params("arbitrary", "arbitrary"),
        name="moe_swiglu",
    )(tile_expert, n_used, x, wgu, wd)


def _proj_swiglu_kernel(a_ref, wo_ref, h_ref, g_ref, wgu_ref, wd_ref, o_ref, hres_ref, xn_ref, acc_ref):
    j = pl.program_id(1)

    @pl.when(j == 0)
    def _():
        h = h_ref[...] + jnp.dot(a_ref[...], wo_ref[...], preferred_element_type=F32)
        hres_ref[...] = h
        xn_ref[...] = _rms(h, g_ref[...]).astype(xn_ref.dtype)
        acc_ref[...] = jnp.zeros_like(acc_ref)

    _swiglu_step(xn_ref[...], wgu_ref, wd_ref, acc_ref)

    @pl.when(j == pl.num_programs(1) - 1)
    def _():
        o_ref[...] = hres_ref[...] + acc_ref[...]


def _proj_swiglu(a, wo, h, gain, wgu, wd):
    n, kdim = a.shape
    d = wo.shape[1]
    nf, tf = wd.shape[1], wd.shape[2]
    tm = FFN_TILE
    assert n % tm == 0
    return pl.pallas_call(
        _proj_swiglu_kernel,
        grid=(n // tm, nf),
        in_specs=[
            pl.BlockSpec((tm, kdim), lambda i, j: (i, 0)),
            pl.BlockSpec((kdim, d), lambda i, j: (0, 0)),
            pl.BlockSpec((tm, d), lambda i, j: (i, 0)),
            pl.BlockSpec((1, d), lambda i, j: (0, 0)),
            pl.BlockSpec((None, d, 2 * tf), lambda i, j: (0, 0, j)),
            pl.BlockSpec((None, None, tf, d), lambda i, j: (0, j, 0, 0)),
        ],
        out_specs=pl.BlockSpec((tm, d), lambda i, j: (i, 0)),
        out_shape=jax.ShapeDtypeStruct((n, d), F32),
        scratch_shapes=[pltpu.VMEM((tm, d), F32), pltpu.VMEM((tm, d), BF16), pltpu.VMEM((tm, d), F32)],
        compiler_params=_params("arbitrary", "arbitrary"),
        name="gla_out_proj_dense_swiglu",
    )(a, wo, h, gain.reshape(1, d), wgu, wd)


def _alibi_slope(head):
    return float(2.0 ** (-8.0 * (head + 1) / N_Q_HEADS))


def _swa_bias(t):
    qi = jnp.arange(t)[:, None]
    kj = jnp.arange(3 * t)[None, :]
    dist = jnp.abs(qi + t - kj)
    band = dist <= WINDOW
    valid = jnp.stack([band & (kj >= t), band, band & (kj < 2 * t)])
    slopes = jnp.asarray([_alibi_slope(h) for h in range(N_Q_HEADS)], F32)
    bias = -slopes[None, :, None, None] * dist.astype(F32)[None, None]
    bias = jnp.where(valid[:, None], bias, -jnp.inf)
    return bias.reshape(3, N_KV_HEADS, GROUP * t, 3 * t)


def _swa_kernel(sink_ref, q_ref, kp_ref, kc_ref, kn_ref, vp_ref, vc_ref, vn_ref, bias0_ref, bias1_ref,
                o_ref):
    t = ATT_BLOCK
    lane = lax.broadcasted_iota(jnp.int32, (t, LANES), 1)
    low = lane < HEAD_DIM
    row_head = lax.broadcasted_iota(jnp.int32, (GROUP * t, 1), 0) // t
    ones = jnp.ones((3 * t, LANES), BF16)

    def window(p_ref, c_ref, n_ref, kvh, sub):
        ks = slice(kvh * LANES, (kvh + 1) * LANES)
        if sub == 0:
            return jnp.concatenate([p_ref[t:, ks], c_ref[:, ks]], axis=0)
        return jnp.concatenate([c_ref[:, ks], n_ref[:t, ks]], axis=0)

    row_blocks = []
    for sub, bias_ref in enumerate((bias0_ref, bias1_ref)):
        rows = slice(sub * t, (sub + 1) * t)
        blocks = []
        for kvh in range(N_KV_HEADS):
            qs = []
            for g in range(GROUP):
                head = kvh * GROUP + g
                col = (head // 2) * LANES
                q2 = q_ref[rows, col:col + LANES]
                qs.append(jnp.where(low if head % 2 == 0 else jnp.logical_not(low), q2,
                                    jnp.zeros_like(q2)))
            qg = jnp.concatenate(qs, axis=0)
            kw = window(kp_ref, kc_ref, kn_ref, kvh, sub)
            s = lax.dot_general(qg, kw, (((1,), (1,)), ((), ())), preferred_element_type=F32)
            s = s + bias_ref[kvh]
            sink = jnp.full((GROUP * t, 1), sink_ref[kvh * GROUP], F32)
            for g in range(1, GROUP):
                sink = jnp.where(row_head == g, sink_ref[kvh * GROUP + g], sink)
            m = jnp.maximum(jnp.max(s, axis=-1, keepdims=True), sink)
            p = jnp.exp(s - m).astype(BF16)
            vw = jnp.concatenate([window(vp_ref, vc_ref, vn_ref, kvh, sub), ones], axis=1)
            res = jnp.dot(p, vw, preferred_element_type=F32)
            out = res[:, :LANES] * (1.0 / (res[:, LANES:] + jnp.exp(sink - m)))
            for pair in range(GROUP // 2):
                a = out[(2 * pair) * t:(2 * pair + 1) * t]
                b = out[(2 * pair + 1) * t:(2 * pair + 2) * t]
                blocks.append(jnp.where(low, a, b).astype(o_ref.dtype))
        row_blocks.append(jnp.concatenate(blocks, axis=1))
    o_ref[...] = jnp.concatenate(row_blocks, axis=0)


def _swa(q, kdup, vdup, sinks, batch, seq):
    n, qw = q.shape
    t = ATT_BLOCK
    rows = 2 * t
    nblk = seq // rows
    assert seq % rows == 0 and nblk >= 2
    kvw = kdup.shape[1]
    bias = _swa_bias(t)
    prev = lambda b, i, s: (b * nblk + jnp.maximum(i - 1, 0), 0)
    cur = lambda b, i, s: (b * nblk + i, 0)
    nxt = lambda b, i, s: (b * nblk + jnp.minimum(i + 1, nblk - 1), 0)
    first = lambda b, i, s: (jnp.where(i == 0, 0, 1), 0, 0, 0)
    second = lambda b, i, s: (jnp.where(i == nblk - 1, 2, 1), 0, 0, 0)
    return pl.pallas_call(
        _swa_kernel,
        grid_spec=pltpu.PrefetchScalarGridSpec(
            num_scalar_prefetch=1,
            grid=(batch, nblk),
            in_specs=[
                pl.BlockSpec((rows, qw), cur),
                pl.BlockSpec((rows, kvw), prev), pl.BlockSpec((rows, kvw), cur), pl.BlockSpec((rows, kvw), nxt),
                pl.BlockSpec((rows, kvw), prev), pl.BlockSpec((rows, kvw), cur), pl.BlockSpec((rows, kvw), nxt),
                pl.BlockSpec((None,) + bias.shape[1:], first),
                pl.BlockSpec((None,) + bias.shape[1:], second),
            ],
            out_specs=pl.BlockSpec((rows, qw), cur),
        ),
        out_shape=jax.ShapeDtypeStruct((n, qw), BF16),
        compiler_params=_params("arbitrary", "arbitrary"),
        name="swa",
    )(sinks, q, kdup, kdup, kdup, vdup, vdup, vdup, bias, bias)


SEG_ALIGN = 8
SEG_BIG = 64
RUN_FIELDS = 7


def _block_rows(tm, n_experts):
    worst = TOP_K * tm + 2 * (SEG_ALIGN - 1) * n_experts
    return -(-worst // LANES) * LANES


def _pair_block_rows(meta, base_ref, tile, n_experts):
    row1, row2 = meta[:, 4:5], meta[:, 5:6]
    for e in range(n_experts):
        base = base_ref[tile * n_experts + e]
        row1 = row1 + jnp.where(meta[:, 0:1] == float(e), base, 0.0)
        row2 = row2 + jnp.where(meta[:, 1:2] == float(e), base, 0.0)
    return row1, row2


def _for_pieces(n_rows, make_copy, act):
    n_big = lax.shift_right_logical(n_rows, 6)
    n_small = lax.shift_right_logical(n_rows - n_big * SEG_BIG, 3)

    def big(j, carry):
        act(make_copy(j * SEG_BIG, SEG_BIG))
        return carry

    def small(j, carry):
        act(make_copy(n_big * SEG_BIG + j * SEG_ALIGN, SEG_ALIGN))
        return carry

    lax.fori_loop(0, n_big, big, 0)
    lax.fori_loop(0, n_small, small, 0)


def _for_rows(first, count, make_copy, act):
    def one(j, carry):
        act(make_copy(first + j))
        return carry

    lax.fori_loop(0, count, one, 0)


def _start(copy):
    copy.start()


def _wait(copy):
    copy.wait()


def _scatter_kernel(runs_ref, totals_ref, base_ref, pads_ref, x_ref, meta_ref, xs_ref, blk_ref, zero_ref,
                    sems, zero_sem, *, n_experts):
    i = pl.program_id(0)
    steps = pl.num_programs(0)
    tm = x_ref.shape[0]
    rows = blk_ref.shape[1]
    n_runs = steps * n_experts

    def start_tile(tile):
        slot = lax.rem(tile, 2)
        for e in range(n_experts):
            k = tile * n_experts + e
            head0, head_n, mid0, mid_len, tail0, tail_n, shift = [runs_ref[f * n_runs + k]
                                                                  for f in range(RUN_FIELDS)]

            def row_copy(j, head0=head0, head_n=head_n, tail0=tail0, shift=shift):
                r = jnp.where(j < head_n, head0 + j, tail0 + (j - head_n))
                return pltpu.make_async_copy(blk_ref.at[slot, pl.ds(r - shift, 1)],
                                             xs_ref.at[pl.ds(r, 1)], sems.at[slot])

            def piece(off, size, mid0=mid0, shift=shift):
                src = pl.multiple_of(mid0 - shift + off, SEG_ALIGN)
                dst = pl.multiple_of(mid0 + off, SEG_ALIGN)
                return pltpu.make_async_copy(blk_ref.at[slot, pl.ds(src, size)],
                                             xs_ref.at[pl.ds(dst, size)], sems.at[slot])

            _for_rows(0, head_n + tail_n, row_copy, _start)
            _for_pieces(mid_len, piece, _start)

    def wait_tile(tile):
        slot = lax.rem(tile, 2)
        for f, size in enumerate((1, SEG_BIG, SEG_ALIGN)):
            def same_size_copy(j, size=size):
                return pltpu.make_async_copy(blk_ref.at[slot, pl.ds(0, size)], xs_ref.at[pl.ds(0, size)],
                                             sems.at[slot])

            _for_rows(0, totals_ref[f * steps + tile], same_size_copy, _wait)

    @pl.when(i == 0)
    def _():
        zero_ref[...] = jnp.zeros_like(zero_ref)
        for act in (_start, _wait):
            for e in range(n_experts):
                pad0, head_n, mid_len = [pads_ref[f * n_experts + e] for f in range(3)]

                def zero_row(r):
                    return pltpu.make_async_copy(zero_ref.at[pl.ds(0, 1)], xs_ref.at[pl.ds(r, 1)], zero_sem)

                def zero_piece(off, size, pad0=pad0, head_n=head_n):
                    dst = pl.multiple_of(pad0 + head_n + off, SEG_ALIGN)
                    return pltpu.make_async_copy(zero_ref.at[pl.ds(0, size)], xs_ref.at[pl.ds(dst, size)],
                                                 zero_sem)

                _for_rows(pad0, head_n, zero_row, act)
                _for_pieces(mid_len, zero_piece, act)

    @pl.when(i >= 2)
    def _():
        wait_tile(i - 2)

    row1, row2 = _pair_block_rows(meta_ref[...], base_ref, i, n_experts)
    col = lax.broadcasted_iota(jnp.int32, (tm, rows), 1).astype(F32)
    pick = jnp.where((col == row1) | (col == row2), 1.0, 0.0).astype(BF16)
    blk_ref[lax.rem(i, 2)] = lax.dot_general(pick, x_ref[...], (((0,), (0,)), ((), ())),
                                             preferred_element_type=F32)
    start_tile(i)

    @pl.when(i == steps - 1)
    def _():
        @pl.when(i >= 1)
        def _():
            wait_tile(i - 1)

        wait_tile(i)


def _scatter_rows(x, meta, runs, totals, row_base, pads, total_rows, n_experts):
    n, d = x.shape
    tm = COMBINE_TILE
    assert n % tm == 0
    rows = _block_rows(tm, n_experts)
    smem = pl.BlockSpec(memory_space=pltpu.SMEM)
    return pl.pallas_call(
        functools.partial(_scatter_kernel, n_experts=n_experts),
        grid=(n // tm,),
        in_specs=[
            smem, smem, smem, smem,
            pl.BlockSpec((tm, d), lambda i: (i, 0)),
            pl.BlockSpec((tm, META_WIDTH), lambda i: (i, 0)),
        ],
        out_specs=pl.BlockSpec(memory_space=pl.ANY),
        out_shape=jax.ShapeDtypeStruct((total_rows, d), F32),
        scratch_shapes=[pltpu.VMEM((2, rows, d), F32), pltpu.VMEM((SEG_BIG, d), F32),
                        pltpu.SemaphoreType.DMA((2,)), pltpu.SemaphoreType.DMA],
        compiler_params=_params("arbitrary"),
        name="moe_scatter",
    )(runs, totals, row_base, pads, x, meta)


def _combine_kernel(src_ref, len_ref, dst_ref, totals_ref, base_ref, h_ref, meta_ref, g_ref, ys_ref, o_ref,
                    yblk_ref, sems, *, n_experts):
    i = pl.program_id(0)
    steps = pl.num_programs(0)
    tm = h_ref.shape[0]
    rows = yblk_ref.shape[1]

    def start_tile(tile):
        slot = lax.rem(tile, 2)
        for e in range(n_experts):
            k = tile * n_experts + e

            def piece(off, size, src0=src_ref[k], dst0=dst_ref[k]):
                src = pl.multiple_of(src0 + off, SEG_ALIGN)
                dst = pl.multiple_of(dst0 + off, SEG_ALIGN)
                return pltpu.make_async_copy(ys_ref.at[pl.ds(src, size)],
                                             yblk_ref.at[slot, pl.ds(dst, size)], sems.at[slot])

            _for_pieces(len_ref[k], piece, _start)

    def wait_tile(tile):
        slot = lax.rem(tile, 2)
        for f, size in enumerate((SEG_BIG, SEG_ALIGN)):
            def same_size_copy(j, size=size):
                return pltpu.make_async_copy(ys_ref.at[pl.ds(0, size)], yblk_ref.at[slot, pl.ds(0, size)],
                                             sems.at[slot])

            _for_rows(0, totals_ref[f * steps + tile], same_size_copy, _wait)

    @pl.when(i == 0)
    def _():
        yblk_ref[...] = jnp.zeros_like(yblk_ref)
        start_tile(i)

    @pl.when(i + 1 < steps)
    def _():
        start_tile(i + 1)

    wait_tile(i)

    meta = meta_ref[...]
    w1, w2 = meta[:, 2:3], meta[:, 3:4]
    row1, row2 = _pair_block_rows(meta, base_ref, i, n_experts)
    col = lax.broadcasted_iota(jnp.int32, (tm, rows), 1).astype(F32)
    pick = jnp.concatenate([jnp.where(col == row1, 1.0, 0.0),
                            jnp.where(col == row2, 1.0, 0.0)], axis=0).astype(BF16)
    yblk = yblk_ref[lax.rem(i, 2)].astype(BF16)
    y = jnp.dot(pick, yblk, preferred_element_type=F32)
    o_ref[...] = _rms(h_ref[...] + (w1 * y[:tm] + w2 * y[tm:]), g_ref[...])


def _combine(h, meta, gain, ys, seg_src, seg_len, seg_dst, row_base, n_experts):
    n, d = h.shape
    tm = COMBINE_TILE
    assert n % tm == 0
    rows = _block_rows(tm, n_experts)
    smem = pl.BlockSpec(memory_space=pltpu.SMEM)
    lens = seg_len.reshape(n // tm, n_experts)
    totals = jnp.stack([jnp.sum(lens // SEG_BIG, axis=1),
                        jnp.sum(lens % SEG_BIG // SEG_ALIGN, axis=1)]).reshape(-1)
    return pl.pallas_call(
        functools.partial(_combine_kernel, n_experts=n_experts),
        grid=(n // tm,),
        in_specs=[
            smem, smem, smem, smem, smem,
            pl.BlockSpec((tm, d), lambda i: (i, 0)),
            pl.BlockSpec((tm, META_WIDTH), lambda i: (i, 0)),
            pl.BlockSpec((1, d), lambda i: (0, 0)),
            pl.BlockSpec(memory_space=pl.ANY),
        ],
        out_specs=pl.BlockSpec((tm, d), lambda i: (i, 0)),
        out_shape=jax.ShapeDtypeStruct((n, d), F32),
        scratch_shapes=[pltpu.VMEM((2, rows, d), ys.dtype), pltpu.SemaphoreType.DMA((2,))],
        compiler_params=_params("arbitrary"),
        name="moe_combine",
    )(seg_src, seg_len, seg_dst, totals, row_base, h, meta, gain.reshape(1, d), ys)


def kernel(x, mix_norm, ffn_norm, gla_in_proj, gla_gate_w_fwd, gla_gate_b_fwd, gla_gate_w_bwd, gla_gate_b_bwd, gla_head_norm, gla_out_proj, swa_qkv_proj, swa_qkv_bias, swa_sinks, swa_out_proj, swa_out_bias, dense_w_gate, dense_w_up, dense_w_down, moe_router, moe_w_gate, moe_w_up, moe_w_down, final_norm):
    batch, seq, d = x.shape
    n = batch * seq
    h0 = x.reshape(n, d)

    key_w = gla_gate_w_fwd.shape[2]
    val_w = gla_head_norm.shape[1]
    rank = GLA_GATE_RANK
    in_w = gla_in_proj.shape[2]
    splits = ((0, key_w), (key_w, key_w), (2 * key_w, val_w), (2 * key_w + val_w, val_w),
              (2 * key_w + 2 * val_w, 2 * rank))
    q, k, v, r, lr = _norm_proj(h0, mix_norm[0], gla_in_proj[0].astype(BF16), jnp.zeros((in_w,), F32),
                                splits, (BF16, BF16, BF16, BF16, F32), "gla_in_proj")
    zero_gate = jnp.zeros((rank, key_w), F32)
    wgf = jnp.concatenate([gla_gate_w_fwd[0], zero_gate], axis=0).astype(BF16)
    wgb = jnp.concatenate([zero_gate, gla_gate_w_bwd[0]], axis=0).astype(BF16)
    og = _gla(q, k, v, lr, r, wgf, gla_gate_b_fwd[0].reshape(1, key_w), wgb,
              gla_gate_b_bwd[0].reshape(1, key_w), gla_head_norm[0].reshape(1, val_w), batch, seq)

    dense_wgu, dense_wd = _swiglu_weights(dense_w_gate, dense_w_up, dense_w_down)
    h2 = _proj_swiglu(og, gla_out_proj[0].astype(BF16), h0, ffn_norm[0], dense_wgu, dense_wd)

    qw = N_Q_HEADS * HEAD_DIM
    kvw = N_KV_HEADS * HEAD_DIM
    scale = HEAD_DIM ** -0.5
    wqkv, bqkv = swa_qkv_proj[0], swa_qkv_bias[0]

    def dup(m):
        lead = m.shape[:-1]
        m = m.reshape(lead + (N_KV_HEADS, 1, HEAD_DIM))
        return jnp.broadcast_to(m, lead + (N_KV_HEADS, 2, HEAD_DIM)).reshape(lead + (2 * kvw,))

    w_aug = jnp.concatenate([wqkv[:, :qw] * scale, dup(wqkv[:, qw:qw + kvw]), dup(wqkv[:, qw + kvw:])], axis=1)
    b_aug = jnp.concatenate([bqkv[:qw] * scale, dup(bqkv[qw:qw + kvw]), dup(bqkv[qw + kvw:])], axis=0)
    splits = ((0, qw), (qw, 2 * kvw), (qw + 2 * kvw, 2 * kvw))
    aq, ak, av = _norm_proj(h2, mix_norm[1], w_aug.astype(BF16), b_aug, splits, (BF16, BF16, BF16),
                            "swa_qkv_proj")
    oa = _swa(aq, ak, av, swa_sinks[0], batch, seq)

    n_experts = moe_router.shape[2]
    h3, hn3, meta, counts, tile_counts = _proj_residual_router(
        oa, swa_out_proj[0].astype(BF16), swa_out_bias[0], h2, ffn_norm[1], moe_router[0], "swa_out_proj")
    tm = MOE_TILE
    counts = counts[0, :n_experts].astype(jnp.int32)
    tiles_per_expert = (counts + tm - 1) // tm
    tile_end = jnp.cumsum(tiles_per_expert)
    offsets = (tile_end - tiles_per_expert) * tm
    n_tiles = (TOP_K * n) // tm + n_experts
    tile_ids = jnp.arange(n_tiles, dtype=jnp.int32)
    tile_expert = jnp.sum((tile_ids[:, None] >= tile_end[None, :]).astype(jnp.int32), axis=1)
    tile_expert = jnp.minimum(tile_expert, n_experts - 1)
    n_used = tile_end[-1:].astype(jnp.int32)

    align = SEG_ALIGN
    tile_cnt = tile_counts.reshape(-1, LANES)[:, :n_experts].astype(jnp.int32)
    run_start = offsets[None, :] + jnp.cumsum(tile_cnt, axis=0) - tile_cnt
    run_end = run_start + tile_cnt
    seg_src = run_start - run_start % align
    seg_len = jnp.where(tile_cnt > 0, (run_end + align - 1) // align * align - seg_src, 0)
    seg_dst = jnp.cumsum(seg_len, axis=1) - seg_len
    row_base = (offsets[None, :] + seg_dst - seg_src).astype(F32).reshape(-1)
    mid_start = (run_start + align - 1) // align * align
    mid_end = run_end - run_end % align
    has_mid = mid_end >= mid_start
    head_n = jnp.where(has_mid, mid_start - run_start, tile_cnt)
    mid_len = jnp.where(has_mid, mid_end - mid_start, 0)
    tail_n = jnp.where(has_mid, run_end - mid_end, 0)
    runs = jnp.stack([run_start, head_n, mid_start, mid_len, mid_end, tail_n, seg_src - seg_dst]).reshape(-1)
    totals = jnp.stack([jnp.sum(head_n + tail_n, axis=1), jnp.sum(mid_len // SEG_BIG, axis=1),
                        jnp.sum(mid_len % SEG_BIG // align, axis=1)]).reshape(-1)
    pad_start = offsets + counts
    pad_count = tiles_per_expert * tm - counts
    pad_head = jnp.minimum(pad_count, (align - pad_start % align) % align)
    pads = jnp.stack([pad_start, pad_head, pad_count - pad_head]).reshape(-1)

    xs = _scatter_rows(hn3, meta, runs, totals, row_base, pads, n_tiles * tm, n_experts)
    moe_wgu, moe_wd = _swiglu_weights(moe_w_gate[0], moe_w_up[0], moe_w_down[0])
    ys = _moe_swiglu(xs, moe_wgu, moe_wd, tile_expert, n_used, tm, F32)
    out = _combine(h3, meta, final_norm, ys, seg_src.reshape(-1), seg_len.reshape(-1),
                   seg_dst.reshape(-1), row_base, n_experts)
    return out.reshape(batch, seq, d)
```

```python
import functools

import jax
import jax.numpy as jnp
from jax import lax
from jax.experimental import pallas as pl
from jax.experimental.pallas import tpu as pltpu

F32 = jnp.float32
BF16 = jnp.bfloat16

NORM_EPS = 1e-5

GLA_HEADS = 4
GLA_GATE_RANK = 16
GLA_GATE_TAU = 16.0
GLA_CHUNK = 64
N_Q_HEADS = 16
N_KV_HEADS = 4
HEAD_DIM = 64
GROUP = N_Q_HEADS // N_KV_HEADS
WINDOW = 128
ATT_BLOCK = 128
TOP_K = 2

LANES = 128
VMEM_LIMIT_BYTES = 56 * 2**20

ROW_TILE = 512
GLA_BLOCK = 512
GLA_GROUP = 256
GLA_HEADS_PER_STEP = 4
FFN_TILE = 1024
FFN_F_TILE = 256
MOE_TILE = 1024
COMBINE_TILE = 256
META_WIDTH = 8


def _params(*sem):
    return pltpu.CompilerParams(dimension_semantics=sem, vmem_limit_bytes=VMEM_LIMIT_BYTES)


def _rms(x, gain):
    y = x * lax.rsqrt(jnp.mean(x * x, axis=-1, keepdims=True) + NORM_EPS)
    return y * gain


def _silu(x):
    return x * (1.0 / (1.0 + jnp.exp(-x)))


def _norm_proj_kernel(x_ref, g_ref, w_ref, b_ref, *o_refs, splits):
    y = _rms(x_ref[...], g_ref[...]).astype(BF16)
    for (start, width), o_ref in zip(splits, o_refs):
        acc = jnp.dot(y, w_ref[:, start:start + width], preferred_element_type=F32)
        acc = acc + b_ref[:, start:start + width]
        o_ref[...] = acc.astype(o_ref.dtype)


def _norm_proj(x, gain, w, bias, splits, dtypes, name):
    n, d = x.shape
    nout = w.shape[1]
    tm = ROW_TILE
    assert n % tm == 0
    return pl.pallas_call(
        functools.partial(_norm_proj_kernel, splits=splits),
        grid=(n // tm,),
        in_specs=[
            pl.BlockSpec((tm, d), lambda i: (i, 0)),
            pl.BlockSpec((1, d), lambda i: (0, 0)),
            pl.BlockSpec((d, nout), lambda i: (0, 0)),
            pl.BlockSpec((1, nout), lambda i: (0, 0)),
        ],
        out_specs=[pl.BlockSpec((tm, wd), lambda i: (i, 0)) for (_, wd) in splits],
        out_shape=[jax.ShapeDtypeStruct((n, wd), dt) for (_, wd), dt in zip(splits, dtypes)],
        compiler_params=_params("arbitrary"),
        name=name,
    )(x, gain.reshape(1, d), w, bias.reshape(1, nout))


def _gla_direction(q_ref, k_ref, v_ref, lr_ref, wg_ref, bg_ref, state_ref, reverse):
    heads, dk, dv = state_ref.shape
    rows, width = q_ref.shape
    c = GLA_CHUNK
    nc = rows // c
    q = q_ref[...].astype(F32) * (dk ** -0.5)
    k = k_ref[...].astype(F32)
    v = v_ref[...]
    z = jnp.dot(lr_ref[...].astype(BF16), wg_ref[...], preferred_element_type=F32) + bg_ref[...]
    la = (jnp.minimum(z, 0.0) - jnp.log(1.0 + jnp.exp(-jnp.abs(z)))) * (1.0 / GLA_GATE_TAU)
    la_hi = la.astype(BF16)
    la_lo = (la - la_hi.astype(F32)).astype(BF16)

    tn = (((0,), (0,)), ((), ()))
    nt = (((1,), (1,)), ((), ()))
    grp = GLA_GROUP
    ng = rows // grp
    ri = lax.broadcasted_iota(jnp.int32, (grp, grp), 0)
    ci = lax.broadcasted_iota(jnp.int32, (grp, grp), 1)
    same_chunk = (ri // c) == (ci // c)
    if reverse:
        cum_mask = same_chunk & (ci >= ri)
        att_mask = same_chunk & (ci > ri)
        ref_row, last_row = c // 2 - 1, 0
    else:
        cum_mask = same_chunk & (ci <= ri)
        att_mask = cum_mask
        ref_row, last_row = c // 2, c - 1
    cum = jnp.where(cum_mask, 1.0, 0.0).astype(BF16)

    la_hl = jnp.concatenate([la_hi, la_lo], axis=1)
    b = jnp.concatenate(
        [jnp.dot(cum, la_hl[g * grp:(g + 1) * grp], preferred_element_type=F32) for g in range(ng)],
        axis=0)
    b = b[:, :width] + b[:, width:]
    b3 = b.reshape(nc, c, width)
    b_ref = b3[:, ref_row:ref_row + 1, :]
    b_last = b3[:, last_row:last_row + 1, :]
    q3 = q.reshape(nc, c, width)
    k3 = k.reshape(nc, c, width)
    qe = (q3 * jnp.exp(b3 - b_ref)).astype(BF16).reshape(rows, width)
    ke = (k3 * jnp.exp(b_ref - b3)).astype(BF16).reshape(rows, width)
    kd = (k3 * jnp.exp(b_last - b3)).astype(BF16).reshape(rows, width)
    qb = (q3 * jnp.exp(b3)).astype(BF16).reshape(rows, width)
    decay_rows = jnp.exp(b_last.reshape(nc, width))

    outs = []
    for h in range(heads):
        ks = slice(h * dk, (h + 1) * dk)
        vh = v[:, h * dv:(h + 1) * dv]
        o_intra = []
        for g in range(ng):
            sl = slice(g * grp, (g + 1) * grp)
            s = lax.dot_general(qe[sl, ks], ke[sl, ks], nt, preferred_element_type=F32)
            s = jnp.where(att_mask, s, 0.0).astype(BF16)
            o_intra.append(jnp.dot(s, vh[sl], preferred_element_type=F32))
        o_intra = jnp.concatenate(o_intra, axis=0)

        upd = [lax.dot_general(kd[j * c:(j + 1) * c, ks], vh[j * c:(j + 1) * c], tn,
                               preferred_element_type=F32) for j in range(nc)]
        decay_cols = jnp.concatenate([decay_rows[:, ks], jnp.zeros((dk - nc, dk), F32)], axis=0).T

        state = state_ref[h]
        o_inter = [None] * nc
        for j in (range(nc - 1, -1, -1) if reverse else range(nc)):
            o_inter[j] = jnp.dot(qb[j * c:(j + 1) * c, ks], state.astype(BF16),
                                 preferred_element_type=F32)
            state = state * decay_cols[:, j:j + 1] + upd[j]
        state_ref[h] = state
        outs.append(o_intra + jnp.concatenate(o_inter, axis=0))
    return jnp.concatenate(outs, axis=1)


def _gla_kernel(q_ref, k_ref, v_ref, lr_ref, r_ref, wgf_ref, bgf_ref, wgb_ref, bgb_ref, gain_ref,
                o_ref, state_ref, oacc_ref, *, nblk):
    i = pl.program_id(2)
    rows = q_ref.shape[0]
    heads, _, dv = state_ref.shape

    @pl.when((i == 0) | (i == nblk))
    def _():
        state_ref[...] = jnp.zeros_like(state_ref)

    @pl.when(i < nblk)
    def _():
        o = _gla_direction(q_ref, k_ref, v_ref, lr_ref, wgf_ref, bgf_ref, state_ref, False)
        oacc_ref[pl.ds(pl.multiple_of(i * rows, rows), rows), :] = o

    @pl.when(i >= nblk)
    def _():
        j = 2 * nblk - 1 - i
        o = _gla_direction(q_ref, k_ref, v_ref, lr_ref, wgb_ref, bgb_ref, state_ref, True)
        o = o + oacc_ref[pl.ds(pl.multiple_of(j * rows, rows), rows), :]
        gain = gain_ref[...]
        o = jnp.concatenate([_rms(o[:, h * dv:(h + 1) * dv], gain[:, h * dv:(h + 1) * dv])
                             for h in range(heads)], axis=1)
        o_ref[...] = (o * _silu(r_ref[...].astype(F32))).astype(o_ref.dtype)


def _gla(q, k, v, lr, r, wgf, bgf, wgb, bgb, gain, batch, seq):
    n = q.shape[0]
    heads = GLA_HEADS
    hps = GLA_HEADS_PER_STEP
    dk = q.shape[1] // heads
    dv = v.shape[1] // heads
    rows = GLA_BLOCK
    nblk = seq // rows
    assert seq % rows == 0 and rows % GLA_GROUP == 0 and GLA_GROUP % GLA_CHUNK == 0 and heads % hps == 0

    def blk(i):
        return jnp.where(i < nblk, i, 2 * nblk - 1 - i)

    def late_blk(i):
        return jnp.where(i < nblk, nblk - 1, 2 * nblk - 1 - i)

    row_map = lambda b, h, i: (b * nblk + blk(i), h)
    return pl.pallas_call(
        functools.partial(_gla_kernel, nblk=nblk),
        grid=(batch, heads // hps, 2 * nblk),
        in_specs=[
            pl.BlockSpec((rows, hps * dk), row_map),
            pl.BlockSpec((rows, hps * dk), row_map),
            pl.BlockSpec((rows, hps * dv), row_map),
            pl.BlockSpec((rows, lr.shape[1]), lambda b, h, i: (b * nblk + blk(i), 0)),
            pl.BlockSpec((rows, hps * dv), lambda b, h, i: (b * nblk + late_blk(i), h)),
            pl.BlockSpec((wgf.shape[0], hps * dk), lambda b, h, i: (0, h)),
            pl.BlockSpec((1, hps * dk), lambda b, h, i: (0, h)),
            pl.BlockSpec((wgb.shape[0], hps * dk), lambda b, h, i: (0, h)),
            pl.BlockSpec((1, hps * dk), lambda b, h, i: (0, h)),
            pl.BlockSpec((1, hps * dv), lambda b, h, i: (0, h)),
        ],
        out_specs=pl.BlockSpec((rows, hps * dv), lambda b, h, i: (b * nblk + late_blk(i), h)),
        out_shape=jax.ShapeDtypeStruct((n, heads * dv), BF16),
        scratch_shapes=[pltpu.VMEM((hps, dk, dv), F32), pltpu.VMEM((seq, hps * dv), F32)],
        compiler_params=_params("arbitrary", "arbitrary", "arbitrary"),
        name="gla",
    )(q, k, v, lr, r, wgf, bgf, wgb, bgb, gain)


def _proj_residual_router_kernel(a_ref, w_ref, b_ref, h_ref, g_ref, rhl_ref,
                                 h_out_ref, hn_out_ref, meta_ref, count_ref, tilecnt_ref, *, n_experts):
    i = pl.program_id(0)
    h = h_ref[...] + (jnp.dot(a_ref[...], w_ref[...], preferred_element_type=F32) + b_ref[...])
    h_out_ref[...] = h
    hn = _rms(h, g_ref[...])
    hn_out_ref[...] = hn.astype(hn_out_ref.dtype)

    hn_hi = hn.astype(BF16)
    hn_lo = (hn - hn_hi.astype(F32)).astype(BF16)
    hh = jnp.dot(hn_hi, rhl_ref[...], preferred_element_type=F32)
    logits = (hh[:, :LANES] + hh[:, LANES:]
              + jnp.dot(hn_lo, rhl_ref[:, :LANES], preferred_element_type=F32))
    tm = logits.shape[0]
    lane = lax.broadcasted_iota(jnp.int32, logits.shape, 1).astype(F32)
    neg = jnp.float32(-jnp.inf)
    logits = jnp.where(lane < n_experts, logits, neg)
    m1 = jnp.max(logits, axis=-1, keepdims=True)
    i1 = jnp.min(jnp.where(logits == m1, lane, float(LANES)), axis=-1, keepdims=True)
    rest = jnp.where(lane == i1, neg, logits)
    m2 = jnp.max(rest, axis=-1, keepdims=True)
    i2 = jnp.min(jnp.where(rest == m2, lane, float(LANES)), axis=-1, keepdims=True)
    e2 = jnp.exp(m2 - m1)
    w1 = 1.0 / (1.0 + e2)
    w2 = e2 / (1.0 + e2)

    @pl.when(i == 0)
    def _():
        count_ref[...] = jnp.zeros_like(count_ref)

    sel = (lane == i1) | (lane == i2)
    onehot = jnp.where(sel, 1.0, 0.0)
    ri = lax.broadcasted_iota(jnp.int32, (tm, tm), 0)
    ci = lax.broadcasted_iota(jnp.int32, (tm, tm), 1)
    strict_lower = jnp.where(ci < ri, 1.0, 0.0).astype(BF16)
    rank = jnp.dot(strict_lower, onehot.astype(BF16), preferred_element_type=F32) + count_ref[...]
    r1 = jnp.sum(jnp.where(lane == i1, rank, 0.0), axis=-1, keepdims=True)
    r2 = jnp.sum(jnp.where(lane == i2, rank, 0.0), axis=-1, keepdims=True)
    count_ref[...] = count_ref[...] + jnp.sum(onehot, axis=0, keepdims=True)
    sub = tm // tilecnt_ref.shape[0]
    tilecnt_ref[...] = jnp.concatenate(
        [jnp.sum(onehot[u * sub:(u + 1) * sub], axis=0, keepdims=True) for u in range(tilecnt_ref.shape[0])],
        axis=0)

    meta = jnp.where(lane == 0, i1, 0.0)
    meta = jnp.where(lane == 1, i2, meta)
    meta = jnp.where(lane == 2, w1, meta)
    meta = jnp.where(lane == 3, w2, meta)
    meta = jnp.where(lane == 4, r1, meta)
    meta = jnp.where(lane == 5, r2, meta)
    meta_ref[...] = meta[:, :meta_ref.shape[1]]


def _proj_residual_router(a, w, bias, h, gain, router, name):
    n, kdim = a.shape
    d = w.shape[1]
    tm = ROW_TILE
    assert n % tm == 0 and tm % COMBINE_TILE == 0
    sub = tm // COMBINE_TILE
    n_experts = router.shape[1]
    rpad = jnp.zeros((d, LANES), F32).at[:, :n_experts].set(router)
    rhi = rpad.astype(BF16)
    rlo = (rpad - rhi.astype(F32)).astype(BF16)
    row = lambda i: (i, 0)
    fixed = lambda i: (0, 0)
    return pl.pallas_call(
        functools.partial(_proj_residual_router_kernel, n_experts=n_experts),
        grid=(n // tm,),
        in_specs=[
            pl.BlockSpec((tm, kdim), row),
            pl.BlockSpec((kdim, d), fixed),
            pl.BlockSpec((1, d), fixed),
            pl.BlockSpec((tm, d), row),
            pl.BlockSpec((1, d), fixed),
            pl.BlockSpec((d, 2 * LANES), fixed),
        ],
        out_specs=[pl.BlockSpec((tm, d), row), pl.BlockSpec((tm, d), row),
                   pl.BlockSpec((tm, META_WIDTH), row), pl.BlockSpec((1, LANES), fixed),
                   pl.BlockSpec((None, sub, LANES), lambda i: (i, 0, 0))],
        out_shape=[jax.ShapeDtypeStruct((n, d), F32), jax.ShapeDtypeStruct((n, d), BF16),
                   jax.ShapeDtypeStruct((n, META_WIDTH), F32), jax.ShapeDtypeStruct((1, LANES), F32),
                   jax.ShapeDtypeStruct((n // tm, sub, LANES), F32)],
        compiler_params=_params("arbitrary"),
        name=name,
    )(a, w, bias.reshape(1, d), h, gain.reshape(1, d), jnp.concatenate([rhi, rlo], axis=1))


def _swiglu_step(x, wgu_ref, wd_ref, acc_ref):
    tf = wd_ref.shape[0]
    gu = jnp.dot(x, wgu_ref[...], preferred_element_type=F32)
    a = (_silu(gu[:, :tf]) * gu[:, tf:]).astype(BF16)
    acc_ref[...] += jnp.dot(a, wd_ref[...], preferred_element_type=F32)


def _moe_swiglu_kernel(te_ref, nu_ref, x_ref, wgu_ref, wd_ref, o_ref, xb_ref, acc_ref):
    i = pl.program_id(0)
    j = pl.program_id(1)
    used = i < nu_ref[0]

    @pl.when(used & (j == 0))
    def _():
        xb_ref[...] = x_ref[...].astype(xb_ref.dtype)
        acc_ref[...] = jnp.zeros_like(acc_ref)

    @pl.when(used)
    def _():
        _swiglu_step(xb_ref[...], wgu_ref, wd_ref, acc_ref)

    @pl.when(j == pl.num_programs(1) - 1)
    def _():
        @pl.when(used)
        def _():
            o_ref[...] = acc_ref[...].astype(o_ref.dtype)

        @pl.when(jnp.logical_not(used))
        def _():
            o_ref[...] = jnp.zeros_like(o_ref)


def _swiglu_weights(wg, wu, wd):
    e, d, f = wg.shape
    tf = FFN_F_TILE
    assert f % tf == 0
    nf = f // tf
    tiles = []
    for j in range(nf):
        tiles += [wg[:, :, j * tf:(j + 1) * tf], wu[:, :, j * tf:(j + 1) * tf]]
    wgu = jnp.concatenate(tiles, axis=-1).astype(BF16)
    return wgu, wd.astype(BF16).reshape(e, nf, tf, d)


def _moe_swiglu(x, wgu, wd, tile_expert, n_used, tm, out_dtype):
    rows, d = x.shape
    nf, tf = wd.shape[1], wd.shape[2]
    assert rows % tm == 0

    def fcol(i, j, nu):
        return jnp.where(i < nu[0], j, nf - 1)

    def xmap(i, j, te, nu):
        return (jnp.where(i < nu[0], i, nu[0] - 1), 0)

    return pl.pallas_call(
        _moe_swiglu_kernel,
        grid_spec=pltpu.PrefetchScalarGridSpec(
            num_scalar_prefetch=2,
            grid=(rows // tm, nf),
            in_specs=[
                pl.BlockSpec((tm, d), xmap),
                pl.BlockSpec((None, d, 2 * tf), lambda i, j, te, nu: (te[i], 0, fcol(i, j, nu))),
                pl.BlockSpec((None, None, tf, d), lambda i, j, te, nu: (te[i], fcol(i, j, nu), 0, 0)),
            ],
            out_specs=pl.BlockSpec((tm, d), lambda i, j, te, nu: (i, 0)),
            scratch_shapes=[pltpu.VMEM((tm, d), BF16), pltpu.VMEM((tm, d), F32)],
        ),
        out_shape=jax.ShapeDtypeStruct((rows, d), out_dtype),
        compiler_params=_params("arbitrary", "arbitrary"),
        name="moe_swiglu",
    )(tile_expert, n_used, x, wgu, wd)


def _proj_swiglu_kernel(a_ref, wo_ref, h_ref, g_ref, wgu_ref, wd_ref, o_ref, hres_ref, xn_ref, acc_ref):
    j = pl.program_id(1)

    @pl.when(j == 0)
    def _():
        h = h_ref[...] + jnp.dot(a_ref[...], wo_ref[...], preferred_element_type=F32)
        hres_ref[...] = h
        xn_ref[...] = _rms(h, g_ref[...]).astype(xn_ref.dtype)
        acc_ref[...] = jnp.zeros_like(acc_ref)

    _swiglu_step(xn_ref[...], wgu_ref, wd_ref, acc_ref)

    @pl.when(j == pl.num_programs(1) - 1)
    def _():
        o_ref[...] = hres_ref[...] + acc_ref[...]


def _proj_swiglu(a, wo, h, gain, wgu, wd):
    n, kdim = a.shape
    d = wo.shape[1]
    nf, tf = wd.shape[1], wd.shape[2]
    tm = FFN_TILE
    assert n % tm == 0
    return pl.pallas_call(
        _proj_swiglu_kernel,
        grid=(n // tm, nf),
        in_specs=[
            pl.BlockSpec((tm, kdim), lambda i, j: (i, 0)),
            pl.BlockSpec((kdim, d), lambda i, j: (0, 0)),
            pl.BlockSpec((tm, d), lambda i, j: (i, 0)),
            pl.BlockSpec((1, d), lambda i, j: (0, 0)),
            pl.BlockSpec((None, d, 2 * tf), lambda i, j: (0, 0, j)),
            pl.BlockSpec((None, None, tf, d), lambda i, j: (0, j, 0, 0)),
        ],
        out_specs=pl.BlockSpec((tm, d), lambda i, j: (i, 0)),
        out_shape=jax.ShapeDtypeStruct((n, d), F32),
        scratch_shapes=[pltpu.VMEM((tm, d), F32), pltpu.VMEM((tm, d), BF16), pltpu.VMEM((tm, d), F32)],
        compiler_params=_params("arbitrary", "arbitrary"),
        name="gla_out_proj_dense_swiglu",
    )(a, wo, h, gain.reshape(1, d), wgu, wd)


def _alibi_slope(head):
    return float(2.0 ** (-8.0 * (head + 1) / N_Q_HEADS))


def _swa_bias(t):
    qi = jnp.arange(t)[:, None]
    kj = jnp.arange(3 * t)[None, :]
    dist = jnp.abs(qi + t - kj)
    band = dist <= WINDOW
    valid = jnp.stack([band & (kj >= t), band, band & (kj < 2 * t)])
    slopes = jnp.asarray([_alibi_slope(h) for h in range(N_Q_HEADS)], F32)
    bias = -slopes[None, :, None, None] * dist.astype(F32)[None, None]
    bias = jnp.where(valid[:, None], bias, -jnp.inf)
    return bias.reshape(3, N_KV_HEADS, GROUP * t, 3 * t)


def _swa_kernel(sink_ref, q_ref, kp_ref, kc_ref, kn_ref, vp_ref, vc_ref, vn_ref, bias0_ref, bias1_ref,
                o_ref):
    t = ATT_BLOCK
    lane = lax.broadcasted_iota(jnp.int32, (t, LANES), 1)
    low = lane < HEAD_DIM
    row_head = lax.broadcasted_iota(jnp.int32, (GROUP * t, 1), 0) // t
    ones = jnp.ones((3 * t, LANES), BF16)

    def window(p_ref, c_ref, n_ref, kvh, sub):
        ks = slice(kvh * LANES, (kvh + 1) * LANES)
        if sub == 0:
            return jnp.concatenate([p_ref[t:, ks], c_ref[:, ks]], axis=0)
        return jnp.concatenate([c_ref[:, ks], n_ref[:t, ks]], axis=0)

    row_blocks = []
    for sub, bias_ref in enumerate((bias0_ref, bias1_ref)):
        rows = slice(sub * t, (sub + 1) * t)
        blocks = []
        for kvh in range(N_KV_HEADS):
            qs = []
            for g in range(GROUP):
                head = kvh * GROUP + g
                col = (head // 2) * LANES
                q2 = q_ref[rows, col:col + LANES]
                qs.append(jnp.where(low if head % 2 == 0 else jnp.logical_not(low), q2,
                                    jnp.zeros_like(q2)))
            qg = jnp.concatenate(qs, axis=0)
            kw = window(kp_ref, kc_ref, kn_ref, kvh, sub)
            s = lax.dot_general(qg, kw, (((1,), (1,)), ((), ())), preferred_element_type=F32)
            s = s + bias_ref[kvh]
            sink = jnp.full((GROUP * t, 1), sink_ref[kvh * GROUP], F32)
            for g in range(1, GROUP):
                sink = jnp.where(row_head == g, sink_ref[kvh * GROUP + g], sink)
            m = jnp.maximum(jnp.max(s, axis=-1, keepdims=True), sink)
            p = jnp.exp(s - m).astype(BF16)
            vw = jnp.concatenate([window(vp_ref, vc_ref, vn_ref, kvh, sub), ones], axis=1)
            res = jnp.dot(p, vw, preferred_element_type=F32)
            out = res[:, :LANES] * (1.0 / (res[:, LANES:] + jnp.exp(sink - m)))
            for pair in range(GROUP // 2):
                a = out[(2 * pair) * t:(2 * pair + 1) * t]
                b = out[(2 * pair + 1) * t:(2 * pair + 2) * t]
                blocks.append(jnp.where(low, a, b).astype(o_ref.dtype))
        row_blocks.append(jnp.concatenate(blocks, axis=1))
    o_ref[...] = jnp.concatenate(row_blocks, axis=0)


def _swa(q, kdup, vdup, sinks, batch, seq):
    n, qw = q.shape
    t = ATT_BLOCK
    rows = 2 * t
    nblk = seq // rows
    assert seq % rows == 0 and nblk >= 2
    kvw = kdup.shape[1]
    bias = _swa_bias(t)
    prev = lambda b, i, s: (b * nblk + jnp.maximum(i - 1, 0), 0)
    cur = lambda b, i, s: (b * nblk + i, 0)
    nxt = lambda b, i, s: (b * nblk + jnp.minimum(i + 1, nblk - 1), 0)
    first = lambda b, i, s: (jnp.where(i == 0, 0, 1), 0, 0, 0)
    second = lambda b, i, s: (jnp.where(i == nblk - 1, 2, 1), 0, 0, 0)
    return pl.pallas_call(
        _swa_kernel,
        grid_spec=pltpu.PrefetchScalarGridSpec(
            num_scalar_prefetch=1,
            grid=(batch, nblk),
            in_specs=[
                pl.BlockSpec((rows, qw), cur),
                pl.BlockSpec((rows, kvw), prev), pl.BlockSpec((rows, kvw), cur), pl.BlockSpec((rows, kvw), nxt),
                pl.BlockSpec((rows, kvw), prev), pl.BlockSpec((rows, kvw), cur), pl.BlockSpec((rows, kvw), nxt),
                pl.BlockSpec((None,) + bias.shape[1:], first),
                pl.BlockSpec((None,) + bias.shape[1:], second),
            ],
            out_specs=pl.BlockSpec((rows, qw), cur),
        ),
        out_shape=jax.ShapeDtypeStruct((n, qw), BF16),
        compiler_params=_params("arbitrary", "arbitrary"),
        name="swa",
    )(sinks, q, kdup, kdup, kdup, vdup, vdup, vdup, bias, bias)


SEG_ALIGN = 8
SEG_BIG = 64
RUN_FIELDS = 7


def _block_rows(tm, n_experts):
    worst = TOP_K * tm + 2 * (SEG_ALIGN - 1) * n_experts
    return -(-worst // LANES) * LANES


def _pair_block_rows(meta, base_ref, tile, n_experts):
    row1, row2 = meta[:, 4:5], meta[:, 5:6]
    for e in range(n_experts):
        base = base_ref[tile * n_experts + e]
        row1 = row1 + jnp.where(meta[:, 0:1] == float(e), base, 0.0)
        row2 = row2 + jnp.where(meta[:, 1:2] == float(e), base, 0.0)
    return row1, row2


def _for_pieces(n_rows, make_copy, act):
    n_big = lax.shift_right_logical(n_rows, 6)
    n_small = lax.shift_right_logical(n_rows - n_big * SEG_BIG, 3)

    def big(j, carry):
        act(make_copy(j * SEG_BIG, SEG_BIG))
        return carry

    def small(j, carry):
        act(make_copy(n_big * SEG_BIG + j * SEG_ALIGN, SEG_ALIGN))
        return carry

    lax.fori_loop(0, n_big, big, 0)
    lax.fori_loop(0, n_small, small, 0)


def _for_rows(first, count, make_copy, act):
    def one(j, carry):
        act(make_copy(first + j))
        return carry

    lax.fori_loop(0, count, one, 0)


def _start(copy):
    copy.start()


def _wait(copy):
    copy.wait()


def _scatter_kernel(runs_ref, totals_ref, base_ref, pads_ref, x_ref, meta_ref, xs_ref, blk_ref, zero_ref,
                    sems, zero_sem, *, n_experts):
    i = pl.program_id(0)
    steps = pl.num_programs(0)
    tm = x_ref.shape[0]
    rows = blk_ref.shape[1]
    n_runs = steps * n_experts

    def start_tile(tile):
        slot = lax.rem(tile, 2)
        for e in range(n_experts):
            k = tile * n_experts + e
            head0, head_n, mid0, mid_len, tail0, tail_n, shift = [runs_ref[f * n_runs + k]
                                                                  for f in range(RUN_FIELDS)]

            def row_copy(j, head0=head0, head_n=head_n, tail0=tail0, shift=shift):
                r = jnp.where(j < head_n, head0 + j, tail0 + (j - head_n))
                return pltpu.make_async_copy(blk_ref.at[slot, pl.ds(r - shift, 1)],
                                             xs_ref.at[pl.ds(r, 1)], sems.at[slot])

            def piece(off, size, mid0=mid0, shift=shift):
                src = pl.multiple_of(mid0 - shift + off, SEG_ALIGN)
                dst = pl.multiple_of(mid0 + off, SEG_ALIGN)
                return pltpu.make_async_copy(blk_ref.at[slot, pl.ds(src, size)],
                                             xs_ref.at[pl.ds(dst, size)], sems.at[slot])

            _for_rows(0, head_n + tail_n, row_copy, _start)
            _for_pieces(mid_len, piece, _start)

    def wait_tile(tile):
        slot = lax.rem(tile, 2)
        for f, size in enumerate((1, SEG_BIG, SEG_ALIGN)):
            def same_size_copy(j, size=size):
                return pltpu.make_async_copy(blk_ref.at[slot, pl.ds(0, size)], xs_ref.at[pl.ds(0, size)],
                                             sems.at[slot])

            _for_rows(0, totals_ref[f * steps + tile], same_size_copy, _wait)

    @pl.when(i == 0)
    def _():
        zero_ref[...] = jnp.zeros_like(zero_ref)
        for act in (_start, _wait):
            for e in range(n_experts):
                pad0, head_n, mid_len = [pads_ref[f * n_experts + e] for f in range(3)]

                def zero_row(r):
                    return pltpu.make_async_copy(zero_ref.at[pl.ds(0, 1)], xs_ref.at[pl.ds(r, 1)], zero_sem)

                def zero_piece(off, size, pad0=pad0, head_n=head_n):
                    dst = pl.multiple_of(pad0 + head_n + off, SEG_ALIGN)
                    return pltpu.make_async_copy(zero_ref.at[pl.ds(0, size)], xs_ref.at[pl.ds(dst, size)],
                                                 zero_sem)

                _for_rows(pad0, head_n, zero_row, act)
                _for_pieces(mid_len, zero_piece, act)

    @pl.when(i >= 2)
    def _():
        wait_tile(i - 2)

    row1, row2 = _pair_block_rows(meta_ref[...], base_ref, i, n_experts)
    col = lax.broadcasted_iota(jnp.int32, (tm, rows), 1).astype(F32)
    pick = jnp.where((col == row1) | (col == row2), 1.0, 0.0).astype(BF16)
    blk_ref[lax.rem(i, 2)] = lax.dot_general(pick, x_ref[...], (((0,), (0,)), ((), ())),
                                             preferred_element_type=F32)
    start_tile(i)

    @pl.when(i == steps - 1)
    def _():
        @pl.when(i >= 1)
        def _():
            wait_tile(i - 1)

        wait_tile(i)


def _scatter_rows(x, meta, runs, totals, row_base, pads, total_rows, n_experts):
    n, d = x.shape
    tm = COMBINE_TILE
    assert n % tm == 0
    rows = _block_rows(tm, n_experts)
    smem = pl.BlockSpec(memory_space=pltpu.SMEM)
    return pl.pallas_call(
        functools.partial(_scatter_kernel, n_experts=n_experts),
        grid=(n // tm,),
        in_specs=[
            smem, smem, smem, smem,
            pl.BlockSpec((tm, d), lambda i: (i, 0)),
            pl.BlockSpec((tm, META_WIDTH), lambda i: (i, 0)),
        ],
        out_specs=pl.BlockSpec(memory_space=pl.ANY),
        out_shape=jax.ShapeDtypeStruct((total_rows, d), F32),
        scratch_shapes=[pltpu.VMEM((2, rows, d), F32), pltpu.VMEM((SEG_BIG, d), F32),
                        pltpu.SemaphoreType.DMA((2,)), pltpu.SemaphoreType.DMA],
        compiler_params=_params("arbitrary"),
        name="moe_scatter",
    )(runs, totals, row_base, pads, x, meta)


def _combine_kernel(src_ref, len_ref, dst_ref, totals_ref, base_ref, h_ref, meta_ref, g_ref, ys_ref, o_ref,
                    yblk_ref, sems, *, n_experts):
    i = pl.program_id(0)
    steps = pl.num_programs(0)
    tm = h_ref.shape[0]
    rows = yblk_ref.shape[1]

    def start_tile(tile):
        slot = lax.rem(tile, 2)
        for e in range(n_experts):
            k = tile * n_experts + e

            def piece(off, size, src0=src_ref[k], dst0=dst_ref[k]):
                src = pl.multiple_of(src0 + off, SEG_ALIGN)
                dst = pl.multiple_of(dst0 + off, SEG_ALIGN)
                return pltpu.make_async_copy(ys_ref.at[pl.ds(src, size)],
                                             yblk_ref.at[slot, pl.ds(dst, size)], sems.at[slot])

            _for_pieces(len_ref[k], piece, _start)

    def wait_tile(tile):
        slot = lax.rem(tile, 2)
        for f, size in enumerate((SEG_BIG, SEG_ALIGN)):
            def same_size_copy(j, size=size):
                return pltpu.make_async_copy(ys_ref.at[pl.ds(0, size)], yblk_ref.at[slot, pl.ds(0, size)],
                                             sems.at[slot])

            _for_rows(0, totals_ref[f * steps + tile], same_size_copy, _wait)

    @pl.when(i == 0)
    def _():
        yblk_ref[...] = jnp.zeros_like(yblk_ref)
        start_tile(i)

    @pl.when(i + 1 < steps)
    def _():
        start_tile(i + 1)

    wait_tile(i)

    meta = meta_ref[...]
    w1, w2 = meta[:, 2:3], meta[:, 3:4]
    row1, row2 = _pair_block_rows(meta, base_ref, i, n_experts)
    col = lax.broadcasted_iota(jnp.int32, (tm, rows), 1).astype(F32)
    pick = jnp.concatenate([jnp.where(col == row1, 1.0, 0.0),
                            jnp.where(col == row2, 1.0, 0.0)], axis=0).astype(BF16)
    yblk = yblk_ref[lax.rem(i, 2)].astype(BF16)
    y = jnp.dot(pick, yblk, preferred_element_type=F32)
    o_ref[...] = _rms(h_ref[...] + (w1 * y[:tm] + w2 * y[tm:]), g_ref[...])


def _combine(h, meta, gain, ys, seg_src, seg_len, seg_dst, row_base, n_experts):
    n, d = h.shape
    tm = COMBINE_TILE
    assert n % tm == 0
    rows = _block_rows(tm, n_experts)
    smem = pl.BlockSpec(memory_space=pltpu.SMEM)
    lens = seg_len.reshape(n // tm, n_experts)
    totals = jnp.stack([jnp.sum(lens // SEG_BIG, axis=1),
                        jnp.sum(lens % SEG_BIG // SEG_ALIGN, axis=1)]).reshape(-1)
    return pl.pallas_call(
        functools.partial(_combine_kernel, n_experts=n_experts),
        grid=(n // tm,),
        in_specs=[
            smem, smem, smem, smem, smem,
            pl.BlockSpec((tm, d), lambda i: (i, 0)),
            pl.BlockSpec((tm, META_WIDTH), lambda i: (i, 0)),
            pl.BlockSpec((1, d), lambda i: (0, 0)),
            pl.BlockSpec(memory_space=pl.ANY),
        ],
        out_specs=pl.BlockSpec((tm, d), lambda i: (i, 0)),
        out_shape=jax.ShapeDtypeStruct((n, d), F32),
        scratch_shapes=[pltpu.VMEM((2, rows, d), ys.dtype), pltpu.SemaphoreType.DMA((2,))],
        compiler_params=_params("arbitrary"),
        name="moe_combine",
    )(seg_src, seg_len, seg_dst, totals, row_base, h, meta, gain.reshape(1, d), ys)


def kernel(x, mix_norm, ffn_norm, gla_in_proj, gla_gate_w_fwd, gla_gate_b_fwd, gla_gate_w_bwd, gla_gate_b_bwd, gla_head_norm, gla_out_proj, swa_qkv_proj, swa_qkv_bias, swa_sinks, swa_out_proj, swa_out_bias, dense_w_gate, dense_w_up, dense_w_down, moe_router, moe_w_gate, moe_w_up, moe_w_down, final_norm):
    batch, seq, d = x.shape
    n = batch * seq
    h0 = x.reshape(n, d)

    key_w = gla_gate_w_fwd.shape[2]
    val_w = gla_head_norm.shape[1]
    rank = GLA_GATE_RANK
    in_w = gla_in_proj.shape[2]
    splits = ((0, key_w), (key_w, key_w), (2 * key_w, val_w), (2 * key_w + val_w, val_w),
              (2 * key_w + 2 * val_w, 2 * rank))
    q, k, v, r, lr = _norm_proj(h0, mix_norm[0], gla_in_proj[0].astype(BF16), jnp.zeros((in_w,), F32),
                                splits, (BF16, BF16, BF16, BF16, F32), "gla_in_proj")
    zero_gate = jnp.zeros((rank, key_w), F32)
    wgf = jnp.concatenate([gla_gate_w_fwd[0], zero_gate], axis=0).astype(BF16)
    wgb = jnp.concatenate([zero_gate, gla_gate_w_bwd[0]], axis=0).astype(BF16)
    og = _gla(q, k, v, lr, r, wgf, gla_gate_b_fwd[0].reshape(1, key_w), wgb,
              gla_gate_b_bwd[0].reshape(1, key_w), gla_head_norm[0].reshape(1, val_w), batch, seq)

    dense_wgu, dense_wd = _swiglu_weights(dense_w_gate, dense_w_up, dense_w_down)
    h2 = _proj_swiglu(og, gla_out_proj[0].astype(BF16), h0, ffn_norm[0], dense_wgu, dense_wd)

    qw = N_Q_HEADS * HEAD_DIM
    kvw = N_KV_HEADS * HEAD_DIM
    scale = HEAD_DIM ** -0.5
    wqkv, bqkv = swa_qkv_proj[0], swa_qkv_bias[0]

    def dup(m):
        lead = m.shape[:-1]
        m = m.reshape(lead + (N_KV_HEADS, 1, HEAD_DIM))
        return jnp.broadcast_to(m, lead + (N_KV_HEADS, 2, HEAD_DIM)).reshape(lead + (2 * kvw,))

    w_aug = jnp.concatenate([wqkv[:, :qw] * scale, dup(wqkv[:, qw:qw + kvw]), dup(wqkv[:, qw + kvw:])], axis=1)
    b_aug = jnp.concatenate([bqkv[:qw] * scale, dup(bqkv[qw:qw + kvw]), dup(bqkv[qw + kvw:])], axis=0)
    splits = ((0, qw), (qw, 2 * kvw), (qw + 2 * kvw, 2 * kvw))
    aq, ak, av = _norm_proj(h2, mix_norm[1], w_aug.astype(BF16), b_aug, splits, (BF16, BF16, BF16),
                            "swa_qkv_proj")
    oa = _swa(aq, ak, av, swa_sinks[0], batch, seq)

    n_experts = moe_router.shape[2]
    h3, hn3, meta, counts, tile_counts = _proj_residual_router(
        oa, swa_out_proj[0].astype(BF16), swa_out_bias[0], h2, ffn_norm[1], moe_router[0], "swa_out_proj")
    tm = MOE_TILE
    counts = counts[0, :n_experts].astype(jnp.int32)
    tiles_per_expert = (counts + tm - 1) // tm
    tile_end = jnp.cumsum(tiles_per_expert)
    offsets = (tile_end - tiles_per_expert) * tm
    n_tiles = (TOP_K * n) // tm + n_experts
    tile_ids = jnp.arange(n_tiles, dtype=jnp.int32)
    tile_expert = jnp.sum((tile_ids[:, None] >= tile_end[None, :]).astype(jnp.int32), axis=1)
    tile_expert = jnp.minimum(tile_expert, n_experts - 1)
    n_used = tile_end[-1:].astype(jnp.int32)

    align = SEG_ALIGN
    tile_cnt = tile_counts.reshape(-1, LANES)[:, :n_experts].astype(jnp.int32)
    run_start = offsets[None, :] + jnp.cumsum(tile_cnt, axis=0) - tile_cnt
    run_end = run_start + tile_cnt
    seg_src = run_start - run_start % align
    seg_len = jnp.where(tile_cnt > 0, (run_end + align - 1) // align * align - seg_src, 0)
    seg_dst = jnp.cumsum(seg_len, axis=1) - seg_len
    row_base = (offsets[None, :] + seg_dst - seg_src).astype(F32).reshape(-1)
    mid_start = (run_start + align - 1) // align * align
    mid_end = run_end - run_end % align
    has_mid = mid_end >= mid_start
    head_n = jnp.where(has_mid, mid_start - run_start, tile_cnt)
    mid_len = jnp.where(has_mid, mid_end - mid_start, 0)
    tail_n = jnp.where(has_mid, run_end - mid_end, 0)
    runs = jnp.stack([run_start, head_n, mid_start, mid_len, mid_end, tail_n, seg_src - seg_dst]).reshape(-1)
    totals = jnp.stack([jnp.sum(head_n + tail_n, axis=1), jnp.sum(mid_len // SEG_BIG, axis=1),
                        jnp.sum(mid_len % SEG_BIG // align, axis=1)]).reshape(-1)
    pad_start = offsets + counts
    pad_count = tiles_per_expert * tm - counts
    pad_head = jnp.minimum(pad_count, (align - pad_start % align) % align)
    pads = jnp.stack([pad_start, pad_head, pad_count - pad_head]).reshape(-1)

    xs = _scatter_rows(hn3, meta, runs, totals, row_base, pads, n_tiles * tm, n_experts)
    moe_wgu, moe_wd = _swiglu_weights(moe_w_gate[0], moe_w_up[0], moe_w_down[0])
    ys = _moe_swiglu(xs, moe_wgu, moe_wd, tile_expert, n_used, tm, F32)
    out = _combine(h3, meta, final_norm, ys, seg_src.reshape(-1), seg_len.reshape(-1),
                   seg_dst.reshape(-1), row_base, n_experts)
    return out.reshape(batch, seq, d)
```

```python
import functools

import jax
import jax.numpy as jnp
from jax import lax
from jax.experimental import pallas as pl
from jax.experimental.pallas import tpu as pltpu

F32 = jnp.float32
BF16 = jnp.bfloat16

NORM_EPS = 1e-5

GLA_HEADS = 4
GLA_GATE_RANK = 16
GLA_GATE_TAU = 16.0
GLA_CHUNK = 64
N_Q_HEADS = 16
N_KV_HEADS = 4
HEAD_DIM = 64
GROUP = N_Q_HEADS // N_KV_HEADS
WINDOW = 128
ATT_BLOCK = 128
TOP_K = 2

LANES = 128
VMEM_LIMIT_BYTES = 56 * 2**20

ROW_TILE = 512
GLA_BLOCK = 512
GLA_GROUP = 256
GLA_HEADS_PER_STEP = 4
FFN_TILE = 1024
FFN_F_TILE = 256
MOE_TILE = 1024
COMBINE_TILE = 256
META_WIDTH = 8


def _params(*sem):
    return pltpu.CompilerParams(dimension_semantics=sem, vmem_limit_bytes=VMEM_LIMIT_BYTES)


def _rms(x, gain):
    y = x * lax.rsqrt(jnp.mean(x * x, axis=-1, keepdims=True) + NORM_EPS)
    return y * gain


def _silu(x):
    return x * (1.0 / (1.0 + jnp.exp(-x)))


def _norm_proj_kernel(x_ref, g_ref, w_ref, b_ref, *o_refs, splits):
    y = _rms(x_ref[...], g_ref[...]).astype(BF16)
    for (start, width), o_ref in zip(splits, o_refs):
        acc = jnp.dot(y, w_ref[:, start:start + width], preferred_element_type=F32)
        acc = acc + b_ref[:, start:start + width]
        o_ref[...] = acc.astype(o_ref.dtype)


def _norm_proj(x, gain, w, bias, splits, dtypes, name):
    n, d = x.shape
    nout = w.shape[1]
    tm = ROW_TILE
    assert n % tm == 0
    return pl.pallas_call(
        functools.partial(_norm_proj_kernel, splits=splits),
        grid=(n // tm,),
        in_specs=[
            pl.BlockSpec((tm, d), lambda i: (i, 0)),
            pl.BlockSpec((1, d), lambda i: (0, 0)),
            pl.BlockSpec((d, nout), lambda i: (0, 0)),
            pl.BlockSpec((1, nout), lambda i: (0, 0)),
        ],
        out_specs=[pl.BlockSpec((tm, wd), lambda i: (i, 0)) for (_, wd) in splits],
        out_shape=[jax.ShapeDtypeStruct((n, wd), dt) for (_, wd), dt in zip(splits, dtypes)],
        compiler_params=_params("arbitrary"),
        name=name,
    )(x, gain.reshape(1, d), w, bias.reshape(1, nout))


def _gla_direction(q_ref, k_ref, v_ref, lr_ref, wg_ref, bg_ref, state_ref, reverse):
    heads, dk, dv = state_ref.shape
    rows, width = q_ref.shape
    c = GLA_CHUNK
    nc = rows // c
    q = q_ref[...].astype(F32) * (dk ** -0.5)
    k = k_ref[...].astype(F32)
    v = v_ref[...]
    z = jnp.dot(lr_ref[...].astype(BF16), wg_ref[...], preferred_element_type=F32) + bg_ref[...]
    la = (jnp.minimum(z, 0.0) - jnp.log(1.0 + jnp.exp(-jnp.abs(z)))) * (1.0 / GLA_GATE_TAU)
    la_hi = la.astype(BF16)
    la_lo = (la - la_hi.astype(F32)).astype(BF16)

    tn = (((0,), (0,)), ((), ()))
    nt = (((1,), (1,)), ((), ()))
    grp = GLA_GROUP
    ng = rows // grp
    ri = lax.broadcasted_iota(jnp.int32, (grp, grp), 0)
    ci = lax.broadcasted_iota(jnp.int32, (grp, grp), 1)
    same_chunk = (ri // c) == (ci // c)
    if reverse:
        cum_mask = same_chunk & (ci >= ri)
        att_mask = same_chunk & (ci > ri)
        ref_row, last_row = c // 2 - 1, 0
    else:
        cum_mask = same_chunk & (ci <= ri)
        att_mask = cum_mask
        ref_row, last_row = c // 2, c - 1
    cum = jnp.where(cum_mask, 1.0, 0.0).astype(BF16)

    la_hl = jnp.concatenate([la_hi, la_lo], axis=1)
    b = jnp.concatenate(
        [jnp.dot(cum, la_hl[g * grp:(g + 1) * grp], preferred_element_type=F32) for g in range(ng)],
        axis=0)
    b = b[:, :width] + b[:, width:]
    b3 = b.reshape(nc, c, width)
    b_ref = b3[:, ref_row:ref_row + 1, :]
    b_last = b3[:, last_row:last_row + 1, :]
    q3 = q.reshape(nc, c, width)
    k3 = k.reshape(nc, c, width)
    qe = (q3 * jnp.exp(b3 - b_ref)).astype(BF16).reshape(rows, width)
    ke = (k3 * jnp.exp(b_ref - b3)).astype(BF16).reshape(rows, width)
    kd = (k3 * jnp.exp(b_last - b3)).astype(BF16).reshape(rows, width)
    qb = (q3 * jnp.exp(b3)).astype(BF16).reshape(rows, width)
    decay_rows = jnp.exp(b_last.reshape(nc, width))

    outs = []
    for h in range(heads):
        ks = slice(h * dk, (h + 1) * dk)
        vh = v[:, h * dv:(h + 1) * dv]
        o_intra = []
        for g in range(ng):
            sl = slice(g * grp, (g + 1) * grp)
            s = lax.dot_general(qe[sl, ks], ke[sl, ks], nt, preferred_element_type=F32)
            s = jnp.where(att_mask, s, 0.0).astype(BF16)
            o_intra.append(jnp.dot(s, vh[sl], preferred_element_type=F32))
        o_intra = jnp.concatenate(o_intra, axis=0)

        upd = [lax.dot_general(kd[j * c:(j + 1) * c, ks], vh[j * c:(j + 1) * c], tn,
                               preferred_element_type=F32) for j in range(nc)]
        decay_cols = jnp.concatenate([decay_rows[:, ks], jnp.zeros((dk - nc, dk), F32)], axis=0).T

        state = state_ref[h]
        o_inter = [None] * nc
        for j in (range(nc - 1, -1, -1) if reverse else range(nc)):
            o_inter[j] = jnp.dot(qb[j * c:(j + 1) * c, ks], state.astype(BF16),
                                 preferred_element_type=F32)
            state = state * decay_cols[:, j:j + 1] + upd[j]
        state_ref[h] = state
        outs.append(o_intra + jnp.concatenate(o_inter, axis=0))
    return jnp.concatenate(outs, axis=1)


def _gla_kernel(q_ref, k_ref, v_ref, lr_ref, r_ref, wgf_ref, bgf_ref, wgb_ref, bgb_ref, gain_ref,
                o_ref, state_ref, oacc_ref, *, nblk):
    i = pl.program_id(2)
    rows = q_ref.shape[0]
    heads, _, dv = state_ref.shape

    @pl.when((i == 0) | (i == nblk))
    def _():
        state_ref[...] = jnp.zeros_like(state_ref)

    @pl.when(i < nblk)
    def _():
        o = _gla_direction(q_ref, k_ref, v_ref, lr_ref, wgf_ref, bgf_ref, state_ref, False)
        oacc_ref[pl.ds(pl.multiple_of(i * rows, rows), rows), :] = o

    @pl.when(i >= nblk)
    def _():
        j = 2 * nblk - 1 - i
        o = _gla_direction(q_ref, k_ref, v_ref, lr_ref, wgb_ref, bgb_ref, state_ref, True)
        o = o + oacc_ref[pl.ds(pl.multiple_of(j * rows, rows), rows), :]
        gain = gain_ref[...]
        o = jnp.concatenate([_rms(o[:, h * dv:(h + 1) * dv], gain[:, h * dv:(h + 1) * dv])
                             for h in range(heads)], axis=1)
        o_ref[...] = (o * _silu(r_ref[...].astype(F32))).astype(o_ref.dtype)


def _gla(q, k, v, lr, r, wgf, bgf, wgb, bgb, gain, batch, seq):
    n = q.shape[0]
    heads = GLA_HEADS
    hps = GLA_HEADS_PER_STEP
    dk = q.shape[1] // heads
    dv = v.shape[1] // heads
    rows = GLA_BLOCK
    nblk = seq // rows
    assert seq % rows == 0 and rows % GLA_GROUP == 0 and GLA_GROUP % GLA_CHUNK == 0 and heads % hps == 0

    def blk(i):
        return jnp.where(i < nblk, i, 2 * nblk - 1 - i)

    def late_blk(i):
        return jnp.where(i < nblk, nblk - 1, 2 * nblk - 1 - i)

    row_map = lambda b, h, i: (b * nblk + blk(i), h)
    return pl.pallas_call(
        functools.partial(_gla_kernel, nblk=nblk),
        grid=(batch, heads // hps, 2 * nblk),
        in_specs=[
            pl.BlockSpec((rows, hps * dk), row_map),
            pl.BlockSpec((rows, hps * dk), row_map),
            pl.BlockSpec((rows, hps * dv), row_map),
            pl.BlockSpec((rows, lr.shape[1]), lambda b, h, i: (b * nblk + blk(i), 0)),
            pl.BlockSpec((rows, hps * dv), lambda b, h, i: (b * nblk + late_blk(i), h)),
            pl.BlockSpec((wgf.shape[0], hps * dk), lambda b, h, i: (0, h)),
            pl.BlockSpec((1, hps * dk), lambda b, h, i: (0, h)),
            pl.BlockSpec((wgb.shape[0], hps * dk), lambda b, h, i: (0, h)),
            pl.BlockSpec((1, hps * dk), lambda b, h, i: (0, h)),
            pl.BlockSpec((1, hps * dv), lambda b, h, i: (0, h)),
        ],
        out_specs=pl.BlockSpec((rows, hps * dv), lambda b, h, i: (b * nblk + late_blk(i), h)),
        out_shape=jax.ShapeDtypeStruct((n, heads * dv), BF16),
        scratch_shapes=[pltpu.VMEM((hps, dk, dv), F32), pltpu.VMEM((seq, hps * dv), F32)],
        compiler_params=_params("arbitrary", "arbitrary", "arbitrary"),
        name="gla",
    )(q, k, v, lr, r, wgf, bgf, wgb, bgb, gain)


def _proj_residual_router_kernel(a_ref, w_ref, b_ref, h_ref, g_ref, rhl_ref,
                                 h_out_ref, hn_out_ref, meta_ref, count_ref, tilecnt_ref, *, n_experts):
    i = pl.program_id(0)
    h = h_ref[...] + (jnp.dot(a_ref[...], w_ref[...], preferred_element_type=F32) + b_ref[...])
    h_out_ref[...] = h
    hn = _rms(h, g_ref[...])
    hn_out_ref[...] = hn.astype(hn_out_ref.dtype)

    hn_hi = hn.astype(BF16)
    hn_lo = (hn - hn_hi.astype(F32)).astype(BF16)
    hh = jnp.dot(hn_hi, rhl_ref[...], preferred_element_type=F32)
    logits = (hh[:, :LANES] + hh[:, LANES:]
              + jnp.dot(hn_lo, rhl_ref[:, :LANES], preferred_element_type=F32))
    tm = logits.shape[0]
    lane = lax.broadcasted_iota(jnp.int32, logits.shape, 1).astype(F32)
    neg = jnp.float32(-jnp.inf)
    logits = jnp.where(lane < n_experts, logits, neg)
    m1 = jnp.max(logits, axis=-1, keepdims=True)
    i1 = jnp.min(jnp.where(logits == m1, lane, float(LANES)), axis=-1, keepdims=True)
    rest = jnp.where(lane == i1, neg, logits)
    m2 = jnp.max(rest, axis=-1, keepdims=True)
    i2 = jnp.min(jnp.where(rest == m2, lane, float(LANES)), axis=-1, keepdims=True)
    e2 = jnp.exp(m2 - m1)
    w1 = 1.0 / (1.0 + e2)
    w2 = e2 / (1.0 + e2)

    @pl.when(i == 0)
    def _():
        count_ref[...] = jnp.zeros_like(count_ref)

    sel = (lane == i1) | (lane == i2)
    onehot = jnp.where(sel, 1.0, 0.0)
    ri = lax.broadcasted_iota(jnp.int32, (tm, tm), 0)
    ci = lax.broadcasted_iota(jnp.int32, (tm, tm), 1)
    strict_lower = jnp.where(ci < ri, 1.0, 0.0).astype(BF16)
    rank = jnp.dot(strict_lower, onehot.astype(BF16), preferred_element_type=F32) + count_ref[...]
    r1 = jnp.sum(jnp.where(lane == i1, rank, 0.0), axis=-1, keepdims=True)
    r2 = jnp.sum(jnp.where(lane == i2, rank, 0.0), axis=-1, keepdims=True)
    count_ref[...] = count_ref[...] + jnp.sum(onehot, axis=0, keepdims=True)
    sub = tm // tilecnt_ref.shape[0]
    tilecnt_ref[...] = jnp.concatenate(
        [jnp.sum(onehot[u * sub:(u + 1) * sub], axis=0, keepdims=True) for u in range(tilecnt_ref.shape[0])],
        axis=0)

    meta = jnp.where(lane == 0, i1, 0.0)
    meta = jnp.where(lane == 1, i2, meta)
    meta = jnp.where(lane == 2, w1, meta)
    meta = jnp.where(lane == 3, w2, meta)
    meta = jnp.where(lane == 4, r1, meta)
    meta = jnp.where(lane == 5, r2, meta)
    meta_ref[...] = meta[:, :meta_ref.shape[1]]


def _proj_residual_router(a, w, bias, h, gain, router, name):
    n, kdim = a.shape
    d = w.shape[1]
    tm = ROW_TILE
    assert n % tm == 0 and tm % COMBINE_TILE == 0
    sub = tm // COMBINE_TILE
    n_experts = router.shape[1]
    rpad = jnp.zeros((d, LANES), F32).at[:, :n_experts].set(router)
    rhi = rpad.astype(BF16)
    rlo = (rpad - rhi.astype(F32)).astype(BF16)
    row = lambda i: (i, 0)
    fixed = lambda i: (0, 0)
    return pl.pallas_call(
        functools.partial(_proj_residual_router_kernel, n_experts=n_experts),
        grid=(n // tm,),
        in_specs=[
            pl.BlockSpec((tm, kdim), row),
            pl.BlockSpec((kdim, d), fixed),
            pl.BlockSpec((1, d), fixed),
            pl.BlockSpec((tm, d), row),
            pl.BlockSpec((1, d), fixed),
            pl.BlockSpec((d, 2 * LANES), fixed),
        ],
        out_specs=[pl.BlockSpec((tm, d), row), pl.BlockSpec((tm, d), row),
                   pl.BlockSpec((tm, META_WIDTH), row), pl.BlockSpec((1, LANES), fixed),
                   pl.BlockSpec((None, sub, LANES), lambda i: (i, 0, 0))],
        out_shape=[jax.ShapeDtypeStruct((n, d), F32), jax.ShapeDtypeStruct((n, d), BF16),
                   jax.ShapeDtypeStruct((n, META_WIDTH), F32), jax.ShapeDtypeStruct((1, LANES), F32),
                   jax.ShapeDtypeStruct((n // tm, sub, LANES), F32)],
        compiler_params=_params("arbitrary"),
        name=name,
    )(a, w, bias.reshape(1, d), h, gain.reshape(1, d), jnp.concatenate([rhi, rlo], axis=1))


def _swiglu_step(x, wgu, wd, acc_ref):
    tf = wd.shape[0]
    gu = jnp.dot(x, wgu, preferred_element_type=F32)
    a = (_silu(gu[:, :tf]) * gu[:, tf:]).astype(BF16)
    acc_ref[...] += jnp.dot(a, wd, preferred_element_type=F32)


def _moe_swiglu_kernel(te_ref, nu_ref, x_ref, wg_ref, wu_ref, wd_ref, o_ref, acc_ref):
    i = pl.program_id(0)
    j = pl.program_id(1)
    used = i < nu_ref[0]

    @pl.when(used & (j == 0))
    def _():
        acc_ref[...] = jnp.zeros_like(acc_ref)

    @pl.when(used)
    def _():
        wgu = jnp.concatenate([wg_ref[...].astype(BF16), wu_ref[...].astype(BF16)], axis=1)
        _swiglu_step(x_ref[...].astype(BF16), wgu, wd_ref[...].astype(BF16), acc_ref)

    @pl.when(j == pl.num_programs(1) - 1)
    def _():
        @pl.when(used)
        def _():
            o_ref[...] = acc_ref[...].astype(o_ref.dtype)

        @pl.when(jnp.logical_not(used))
        def _():
            o_ref[...] = jnp.zeros_like(o_ref)


def _swiglu_weights(wg, wu, wd):
    e, d, f = wg.shape
    tf = FFN_F_TILE
    assert f % tf == 0
    nf = f // tf
    tiles = []
    for j in range(nf):
        tiles += [wg[:, :, j * tf:(j + 1) * tf], wu[:, :, j * tf:(j + 1) * tf]]
    wgu = jnp.concatenate(tiles, axis=-1).astype(BF16)
    return wgu, wd.astype(BF16).reshape(e, nf, tf, d)


def _moe_swiglu(x, wg, wu, wd, tile_expert, n_used, tm, out_dtype):
    rows, d = x.shape
    tf = FFN_F_TILE
    f = wg.shape[2]
    assert rows % tm == 0 and f % tf == 0
    nf = f // tf

    def fcol(i, j, nu):
        return jnp.where(i < nu[0], j, nf - 1)

    def xmap(i, j, te, nu):
        return (jnp.where(i < nu[0], i, nu[0] - 1), 0)

    return pl.pallas_call(
        _moe_swiglu_kernel,
        grid_spec=pltpu.PrefetchScalarGridSpec(
            num_scalar_prefetch=2,
            grid=(rows // tm, nf),
            in_specs=[
                pl.BlockSpec((tm, d), xmap),
                pl.BlockSpec((None, d, tf), lambda i, j, te, nu: (te[i], 0, fcol(i, j, nu))),
                pl.BlockSpec((None, d, tf), lambda i, j, te, nu: (te[i], 0, fcol(i, j, nu))),
                pl.BlockSpec((None, tf, d), lambda i, j, te, nu: (te[i], fcol(i, j, nu), 0)),
            ],
            out_specs=pl.BlockSpec((tm, d), lambda i, j, te, nu: (i, 0)),
            scratch_shapes=[pltpu.VMEM((tm, d), F32)],
        ),
        out_shape=jax.ShapeDtypeStruct((rows, d), out_dtype),
        compiler_params=_params("arbitrary", "arbitrary"),
        name="moe_swiglu",
    )(tile_expert, n_used, x, wg, wu, wd)


def _proj_swiglu_kernel(a_ref, wo_ref, h_ref, g_ref, wgu_ref, wd_ref, o_ref, hres_ref, xn_ref, acc_ref):
    j = pl.program_id(1)

    @pl.when(j == 0)
    def _():
        h = h_ref[...] + jnp.dot(a_ref[...], wo_ref[...], preferred_element_type=F32)
        hres_ref[...] = h
        xn_ref[...] = _rms(h, g_ref[...]).astype(xn_ref.dtype)
        acc_ref[...] = jnp.zeros_like(acc_ref)

    _swiglu_step(xn_ref[...], wgu_ref[...], wd_ref[...], acc_ref)

    @pl.when(j == pl.num_programs(1) - 1)
    def _():
        o_ref[...] = hres_ref[...] + acc_ref[...]


def _proj_swiglu(a, wo, h, gain, wgu, wd):
    n, kdim = a.shape
    d = wo.shape[1]
    nf, tf = wd.shape[1], wd.shape[2]
    tm = FFN_TILE
    assert n % tm == 0
    return pl.pallas_call(
        _proj_swiglu_kernel,
        grid=(n // tm, nf),
        in_specs=[
            pl.BlockSpec((tm, kdim), lambda i, j: (i, 0)),
            pl.BlockSpec((kdim, d), lambda i, j: (0, 0)),
            pl.BlockSpec((tm, d), lambda i, j: (i, 0)),
            pl.BlockSpec((1, d), lambda i, j: (0, 0)),
            pl.BlockSpec((None, d, 2 * tf), lambda i, j: (0, 0, j)),
            pl.BlockSpec((None, None, tf, d), lambda i, j: (0, j, 0, 0)),
        ],
        out_specs=pl.BlockSpec((tm, d), lambda i, j: (i, 0)),
        out_shape=jax.ShapeDtypeStruct((n, d), F32),
        scratch_shapes=[pltpu.VMEM((tm, d), F32), pltpu.VMEM((tm, d), BF16), pltpu.VMEM((tm, d), F32)],
        compiler_params=_params("arbitrary", "arbitrary"),
        name="gla_out_proj_dense_swiglu",
    )(a, wo, h, gain.reshape(1, d), wgu, wd)


def _alibi_slope(head):
    return float(2.0 ** (-8.0 * (head + 1) / N_Q_HEADS))


def _swa_bias(t):
    qi = jnp.arange(t)[:, None]
    kj = jnp.arange(3 * t)[None, :]
    dist = jnp.abs(qi + t - kj)
    band = dist <= WINDOW
    valid = jnp.stack([band & (kj >= t), band, band & (kj < 2 * t)])
    slopes = jnp.asarray([_alibi_slope(h) for h in range(N_Q_HEADS)], F32)
    bias = -slopes[None, :, None, None] * dist.astype(F32)[None, None]
    bias = jnp.where(valid[:, None], bias, -jnp.inf)
    return bias.reshape(3, N_KV_HEADS, GROUP * t, 3 * t)


def _swa_kernel(sink_ref, q_ref, kp_ref, kc_ref, kn_ref, vp_ref, vc_ref, vn_ref, bias0_ref, bias1_ref,
                o_ref):
    t = ATT_BLOCK
    lane = lax.broadcasted_iota(jnp.int32, (t, LANES), 1)
    low = lane < HEAD_DIM
    row_head = lax.broadcasted_iota(jnp.int32, (GROUP * t, 1), 0) // t
    ones = jnp.ones((3 * t, LANES), BF16)

    def window(p_ref, c_ref, n_ref, kvh, sub):
        ks = slice(kvh * LANES, (kvh + 1) * LANES)
        if sub == 0:
            return jnp.concatenate([p_ref[t:, ks], c_ref[:, ks]], axis=0)
        return jnp.concatenate([c_ref[:, ks], n_ref[:t, ks]], axis=0)

    row_blocks = []
    for sub, bias_ref in enumerate((bias0_ref, bias1_ref)):
        rows = slice(sub * t, (sub + 1) * t)
        blocks = []
        for kvh in range(N_KV_HEADS):
            qs = []
            for g in range(GROUP):
                head = kvh * GROUP + g
                col = (head // 2) * LANES
                q2 = q_ref[rows, col:col + LANES]
                qs.append(jnp.where(low if head % 2 == 0 else jnp.logical_not(low), q2,
                                    jnp.zeros_like(q2)))
            qg = jnp.concatenate(qs, axis=0)
            kw = window(kp_ref, kc_ref, kn_ref, kvh, sub)
            s = lax.dot_general(qg, kw, (((1,), (1,)), ((), ())), preferred_element_type=F32)
            s = s + bias_ref[kvh]
            sink = jnp.full((GROUP * t, 1), sink_ref[kvh * GROUP], F32)
            for g in range(1, GROUP):
                sink = jnp.where(row_head == g, sink_ref[kvh * GROUP + g], sink)
            m = jnp.maximum(jnp.max(s, axis=-1, keepdims=True), sink)
            p = jnp.exp(s - m).astype(BF16)
            vw = jnp.concatenate([window(vp_ref, vc_ref, vn_ref, kvh, sub), ones], axis=1)
            res = jnp.dot(p, vw, preferred_element_type=F32)
            out = res[:, :LANES] * (1.0 / (res[:, LANES:] + jnp.exp(sink - m)))
            for pair in range(GROUP // 2):
                a = out[(2 * pair) * t:(2 * pair + 1) * t]
                b = out[(2 * pair + 1) * t:(2 * pair + 2) * t]
                blocks.append(jnp.where(low, a, b).astype(o_ref.dtype))
        row_blocks.append(jnp.concatenate(blocks, axis=1))
    o_ref[...] = jnp.concatenate(row_blocks, axis=0)


def _swa(q, kdup, vdup, sinks, batch, seq):
    n, qw = q.shape
    t = ATT_BLOCK
    rows = 2 * t
    nblk = seq // rows
    assert seq % rows == 0 and nblk >= 2
    kvw = kdup.shape[1]
    bias = _swa_bias(t)
    prev = lambda b, i, s: (b * nblk + jnp.maximum(i - 1, 0), 0)
    cur = lambda b, i, s: (b * nblk + i, 0)
    nxt = lambda b, i, s: (b * nblk + jnp.minimum(i + 1, nblk - 1), 0)
    first = lambda b, i, s: (jnp.where(i == 0, 0, 1), 0, 0, 0)
    second = lambda b, i, s: (jnp.where(i == nblk - 1, 2, 1), 0, 0, 0)
    return pl.pallas_call(
        _swa_kernel,
        grid_spec=pltpu.PrefetchScalarGridSpec(
            num_scalar_prefetch=1,
            grid=(batch, nblk),
            in_specs=[
                pl.BlockSpec((rows, qw), cur),
                pl.BlockSpec((rows, kvw), prev), pl.BlockSpec((rows, kvw), cur), pl.BlockSpec((rows, kvw), nxt),
                pl.BlockSpec((rows, kvw), prev), pl.BlockSpec((rows, kvw), cur), pl.BlockSpec((rows, kvw), nxt),
                pl.BlockSpec((None,) + bias.shape[1:], first),
                pl.BlockSpec((None,) + bias.shape[1:], second),
            ],
            out_specs=pl.BlockSpec((rows, qw), cur),
        ),
        out_shape=jax.ShapeDtypeStruct((n, qw), BF16),
        compiler_params=_params("arbitrary", "arbitrary"),
        name="swa",
    )(sinks, q, kdup, kdup, kdup, vdup, vdup, vdup, bias, bias)


SEG_ALIGN = 8
SEG_BIG = 64
RUN_FIELDS = 7


def _block_rows(tm, n_experts):
    worst = TOP_K * tm + 2 * (SEG_ALIGN - 1) * n_experts
    return -(-worst // LANES) * LANES


def _pair_block_rows(meta, base_ref, tile, n_experts):
    row1, row2 = meta[:, 4:5], meta[:, 5:6]
    for e in range(n_experts):
        base = base_ref[tile * n_experts + e]
        row1 = row1 + jnp.where(meta[:, 0:1] == float(e), base, 0.0)
        row2 = row2 + jnp.where(meta[:, 1:2] == float(e), base, 0.0)
    return row1, row2


def _for_pieces(n_rows, make_copy, act):
    n_big = lax.shift_right_logical(n_rows, 6)
    n_small = lax.shift_right_logical(n_rows - n_big * SEG_BIG, 3)

    def big(j, carry):
        act(make_copy(j * SEG_BIG, SEG_BIG))
        return carry

    def small(j, carry):
        act(make_copy(n_big * SEG_BIG + j * SEG_ALIGN, SEG_ALIGN))
        return carry

    lax.fori_loop(0, n_big, big, 0)
    lax.fori_loop(0, n_small, small, 0)


def _for_rows(first, count, make_copy, act):
    def one(j, carry):
        act(make_copy(first + j))
        return carry

    lax.fori_loop(0, count, one, 0)


def _start(copy):
    copy.start()


def _wait(copy):
    copy.wait()


def _scatter_kernel(runs_ref, totals_ref, base_ref, pads_ref, x_ref, meta_ref, xs_ref, blk_ref, zero_ref,
                    sems, zero_sem, *, n_experts):
    i = pl.program_id(0)
    steps = pl.num_programs(0)
    tm = x_ref.shape[0]
    rows = blk_ref.shape[1]
    n_runs = steps * n_experts

    def start_tile(tile):
        slot = lax.rem(tile, 2)
        for e in range(n_experts):
            k = tile * n_experts + e
            head0, head_n, mid0, mid_len, tail0, tail_n, shift = [runs_ref[f * n_runs + k]
                                                                  for f in range(RUN_FIELDS)]

            def row_copy(j, head0=head0, head_n=head_n, tail0=tail0, shift=shift):
                r = jnp.where(j < head_n, head0 + j, tail0 + (j - head_n))
                return pltpu.make_async_copy(blk_ref.at[slot, pl.ds(r - shift, 1)],
                                             xs_ref.at[pl.ds(r, 1)], sems.at[slot])

            def piece(off, size, mid0=mid0, shift=shift):
                src = pl.multiple_of(mid0 - shift + off, SEG_ALIGN)
                dst = pl.multiple_of(mid0 + off, SEG_ALIGN)
                return pltpu.make_async_copy(blk_ref.at[slot, pl.ds(src, size)],
                                             xs_ref.at[pl.ds(dst, size)], sems.at[slot])

            _for_rows(0, head_n + tail_n, row_copy, _start)
            _for_pieces(mid_len, piece, _start)

    def wait_tile(tile):
        slot = lax.rem(tile, 2)
        for f, size in enumerate((1, SEG_BIG, SEG_ALIGN)):
            def same_size_copy(j, size=size):
                return pltpu.make_async_copy(blk_ref.at[slot, pl.ds(0, size)], xs_ref.at[pl.ds(0, size)],
                                             sems.at[slot])

            _for_rows(0, totals_ref[f * steps + tile], same_size_copy, _wait)

    @pl.when(i == 0)
    def _():
        zero_ref[...] = jnp.zeros_like(zero_ref)
        for act in (_start, _wait):
            for e in range(n_experts):
                pad0, head_n, mid_len = [pads_ref[f * n_experts + e] for f in range(3)]

                def zero_row(r):
                    return pltpu.make_async_copy(zero_ref.at[pl.ds(0, 1)], xs_ref.at[pl.ds(r, 1)], zero_sem)

                def zero_piece(off, size, pad0=pad0, head_n=head_n):
                    dst = pl.multiple_of(pad0 + head_n + off, SEG_ALIGN)
                    return pltpu.make_async_copy(zero_ref.at[pl.ds(0, size)], xs_ref.at[pl.ds(dst, size)],
                                                 zero_sem)

                _for_rows(pad0, head_n, zero_row, act)
                _for_pieces(mid_len, zero_piece, act)

    @pl.when(i >= 2)
    def _():
        wait_tile(i - 2)

    row1, row2 = _pair_block_rows(meta_ref[...], base_ref, i, n_experts)
    col = lax.broadcasted_iota(jnp.int32, (tm, rows), 1).astype(F32)
    pick = jnp.where((col == row1) | (col == row2), 1.0, 0.0).astype(BF16)
    blk_ref[lax.rem(i, 2)] = lax.dot_general(pick, x_ref[...], (((0,), (0,)), ((), ())),
                                             preferred_element_type=F32)
    start_tile(i)

    @pl.when(i == steps - 1)
    def _():
        @pl.when(i >= 1)
        def _():
            wait_tile(i - 1)

        wait_tile(i)


def _scatter_rows(x, meta, runs, totals, row_base, pads, total_rows, n_experts):
    n, d = x.shape
    tm = COMBINE_TILE
    assert n % tm == 0
    rows = _block_rows(tm, n_experts)
    smem = pl.BlockSpec(memory_space=pltpu.SMEM)
    return pl.pallas_call(
        functools.partial(_scatter_kernel, n_experts=n_experts),
        grid=(n // tm,),
        in_specs=[
            smem, smem, smem, smem,
            pl.BlockSpec((tm, d), lambda i: (i, 0)),
            pl.BlockSpec((tm, META_WIDTH), lambda i: (i, 0)),
        ],
        out_specs=pl.BlockSpec(memory_space=pl.ANY),
        out_shape=jax.ShapeDtypeStruct((total_rows, d), F32),
        scratch_shapes=[pltpu.VMEM((2, rows, d), F32), pltpu.VMEM((SEG_BIG, d), F32),
                        pltpu.SemaphoreType.DMA((2,)), pltpu.SemaphoreType.DMA],
        compiler_params=_params("arbitrary"),
        name="moe_scatter",
    )(runs, totals, row_base, pads, x, meta)


def _combine_kernel(src_ref, len_ref, dst_ref, totals_ref, base_ref, h_ref, meta_ref, g_ref, ys_ref, o_ref,
                    yblk_ref, sems, *, n_experts):
    i = pl.program_id(0)
    steps = pl.num_programs(0)
    tm = h_ref.shape[0]
    rows = yblk_ref.shape[1]

    def start_tile(tile):
        slot = lax.rem(tile, 2)
        for e in range(n_experts):
            k = tile * n_experts + e

            def piece(off, size, src0=src_ref[k], dst0=dst_ref[k]):
                src = pl.multiple_of(src0 + off, SEG_ALIGN)
                dst = pl.multiple_of(dst0 + off, SEG_ALIGN)
                return pltpu.make_async_copy(ys_ref.at[pl.ds(src, size)],
                                             yblk_ref.at[slot, pl.ds(dst, size)], sems.at[slot])

            _for_pieces(len_ref[k], piece, _start)

    def wait_tile(tile):
        slot = lax.rem(tile, 2)
        for f, size in enumerate((SEG_BIG, SEG_ALIGN)):
            def same_size_copy(j, size=size):
                return pltpu.make_async_copy(ys_ref.at[pl.ds(0, size)], yblk_ref.at[slot, pl.ds(0, size)],
                                             sems.at[slot])

            _for_rows(0, totals_ref[f * steps + tile], same_size_copy, _wait)

    @pl.when(i == 0)
    def _():
        yblk_ref[...] = jnp.zeros_like(yblk_ref)
        start_tile(i)

    @pl.when(i + 1 < steps)
    def _():
        start_tile(i + 1)

    wait_tile(i)

    meta = meta_ref[...]
    w1, w2 = meta[:, 2:3], meta[:, 3:4]
    row1, row2 = _pair_block_rows(meta, base_ref, i, n_experts)
    col = lax.broadcasted_iota(jnp.int32, (tm, rows), 1).astype(F32)
    pick = jnp.concatenate([jnp.where(col == row1, 1.0, 0.0),
                            jnp.where(col == row2, 1.0, 0.0)], axis=0).astype(BF16)
    yblk = yblk_ref[lax.rem(i, 2)].astype(BF16)
    y = jnp.dot(pick, yblk, preferred_element_type=F32)
    o_ref[...] = _rms(h_ref[...] + (w1 * y[:tm] + w2 * y[tm:]), g_ref[...])


def _combine(h, meta, gain, ys, seg_src, seg_len, seg_dst, row_base, n_experts):
    n, d = h.shape
    tm = COMBINE_TILE
    assert n % tm == 0
    rows = _block_rows(tm, n_experts)
    smem = pl.BlockSpec(memory_space=pltpu.SMEM)
    lens = seg_len.reshape(n // tm, n_experts)
    totals = jnp.stack([jnp.sum(lens // SEG_BIG, axis=1),
                        jnp.sum(lens % SEG_BIG // SEG_ALIGN, axis=1)]).reshape(-1)
    return pl.pallas_call(
        functools.partial(_combine_kernel, n_experts=n_experts),
        grid=(n // tm,),
        in_specs=[
            smem, smem, smem, smem, smem,
            pl.BlockSpec((tm, d), lambda i: (i, 0)),
            pl.BlockSpec((tm, META_WIDTH), lambda i: (i, 0)),
            pl.BlockSpec((1, d), lambda i: (0, 0)),
            pl.BlockSpec(memory_space=pl.ANY),
        ],
        out_specs=pl.BlockSpec((tm, d), lambda i: (i, 0)),
        out_shape=jax.ShapeDtypeStruct((n, d), F32),
        scratch_shapes=[pltpu.VMEM((2, rows, d), ys.dtype), pltpu.SemaphoreType.DMA((2,))],
        compiler_params=_params("arbitrary"),
        name="moe_combine",
    )(seg_src, seg_len, seg_dst, totals, row_base, h, meta, gain.reshape(1, d), ys)


def kernel(x, mix_norm, ffn_norm, gla_in_proj, gla_gate_w_fwd, gla_gate_b_fwd, gla_gate_w_bwd, gla_gate_b_bwd, gla_head_norm, gla_out_proj, swa_qkv_proj, swa_qkv_bias, swa_sinks, swa_out_proj, swa_out_bias, dense_w_gate, dense_w_up, dense_w_down, moe_router, moe_w_gate, moe_w_up, moe_w_down, final_norm):
    batch, seq, d = x.shape
    n = batch * seq
    h0 = x.reshape(n, d)

    key_w = gla_gate_w_fwd.shape[2]
    val_w = gla_head_norm.shape[1]
    rank = GLA_GATE_RANK
    in_w = gla_in_proj.shape[2]
    splits = ((0, key_w), (key_w, key_w), (2 * key_w, val_w), (2 * key_w + val_w, val_w),
              (2 * key_w + 2 * val_w, 2 * rank))
    q, k, v, r, lr = _norm_proj(h0, mix_norm[0], gla_in_proj[0].astype(BF16), jnp.zeros((in_w,), F32),
                                splits, (BF16, BF16, BF16, BF16, F32), "gla_in_proj")
    zero_gate = jnp.zeros((rank, key_w), F32)
    wgf = jnp.concatenate([gla_gate_w_fwd[0], zero_gate], axis=0).astype(BF16)
    wgb = jnp.concatenate([zero_gate, gla_gate_w_bwd[0]], axis=0).astype(BF16)
    og = _gla(q, k, v, lr, r, wgf, gla_gate_b_fwd[0].reshape(1, key_w), wgb,
              gla_gate_b_bwd[0].reshape(1, key_w), gla_head_norm[0].reshape(1, val_w), batch, seq)

    dense_wgu, dense_wd = _swiglu_weights(dense_w_gate, dense_w_up, dense_w_down)
    h2 = _proj_swiglu(og, gla_out_proj[0].astype(BF16), h0, ffn_norm[0], dense_wgu, dense_wd)

    qw = N_Q_HEADS * HEAD_DIM
    kvw = N_KV_HEADS * HEAD_DIM
    scale = HEAD_DIM ** -0.5
    wqkv, bqkv = swa_qkv_proj[0], swa_qkv_bias[0]

    def dup(m):
        lead = m.shape[:-1]
        m = m.reshape(lead + (N_KV_HEADS, 1, HEAD_DIM))
        return jnp.broadcast_to(m, lead + (N_KV_HEADS, 2, HEAD_DIM)).reshape(lead + (2 * kvw,))

    w_aug = jnp.concatenate([wqkv[:, :qw] * scale, dup(wqkv[:, qw:qw + kvw]), dup(wqkv[:, qw + kvw:])], axis=1)
    b_aug = jnp.concatenate([bqkv[:qw] * scale, dup(bqkv[qw:qw + kvw]), dup(bqkv[qw + kvw:])], axis=0)
    splits = ((0, qw), (qw, 2 * kvw), (qw + 2 * kvw, 2 * kvw))
    aq, ak, av = _norm_proj(h2, mix_norm[1], w_aug.astype(BF16), b_aug, splits, (BF16, BF16, BF16),
                            "swa_qkv_proj")
    oa = _swa(aq, ak, av, swa_sinks[0], batch, seq)

    n_experts = moe_router.shape[2]
    h3, hn3, meta, counts, tile_counts = _proj_residual_router(
        oa, swa_out_proj[0].astype(BF16), swa_out_bias[0], h2, ffn_norm[1], moe_router[0], "swa_out_proj")
    tm = MOE_TILE
    counts = counts[0, :n_experts].astype(jnp.int32)
    tiles_per_expert = (counts + tm - 1) // tm
    tile_end = jnp.cumsum(tiles_per_expert)
    offsets = (tile_end - tiles_per_expert) * tm
    n_tiles = (TOP_K * n) // tm + n_experts
    tile_ids = jnp.arange(n_tiles, dtype=jnp.int32)
    tile_expert = jnp.sum((tile_ids[:, None] >= tile_end[None, :]).astype(jnp.int32), axis=1)
    tile_expert = jnp.minimum(tile_expert, n_experts - 1)
    n_used = tile_end[-1:].astype(jnp.int32)

    align = SEG_ALIGN
    tile_cnt = tile_counts.reshape(-1, LANES)[:, :n_experts].astype(jnp.int32)
    run_start = offsets[None, :] + jnp.cumsum(tile_cnt, axis=0) - tile_cnt
    run_end = run_start + tile_cnt
    seg_src = run_start - run_start % align
    seg_len = jnp.where(tile_cnt > 0, (run_end + align - 1) // align * align - seg_src, 0)
    seg_dst = jnp.cumsum(seg_len, axis=1) - seg_len
    row_base = (offsets[None, :] + seg_dst - seg_src).astype(F32).reshape(-1)
    mid_start = (run_start + align - 1) // align * align
    mid_end = run_end - run_end % align
    has_mid = mid_end >= mid_start
    head_n = jnp.where(has_mid, mid_start - run_start, tile_cnt)
    mid_len = jnp.where(has_mid, mid_end - mid_start, 0)
    tail_n = jnp.where(has_mid, run_end - mid_end, 0)
    runs = jnp.stack([run_start, head_n, mid_start, mid_len, mid_end, tail_n, seg_src - seg_dst]).reshape(-1)
    totals = jnp.stack([jnp.sum(head_n + tail_n, axis=1), jnp.sum(mid_len // SEG_BIG, axis=1),
                        jnp.sum(mid_len % SEG_BIG // align, axis=1)]).reshape(-1)
    pad_start = offsets + counts
    pad_count = tiles_per_expert * tm - counts
    pad_head = jnp.minimum(pad_count, (align - pad_start % align) % align)
    pads = jnp.stack([pad_start, pad_head, pad_count - pad_head]).reshape(-1)

    xs = _scatter_rows(hn3, meta, runs, totals, row_base, pads, n_tiles * tm, n_experts)
    ys = _moe_swiglu(xs, moe_w_gate[0], moe_w_up[0], moe_w_down[0], tile_expert, n_used, tm, F32)
    out = _combine(h3, meta, final_norm, ys, seg_src.reshape(-1), seg_len.reshape(-1),
                   seg_dst.reshape(-1), row_base, n_experts)
    return out.reshape(batch, seq, d)
```

```python
import functools

import jax
import jax.numpy as jnp
from jax import lax
from jax.experimental import pallas as pl
from jax.experimental.pallas import tpu as pltpu

F32 = jnp.float32
BF16 = jnp.bfloat16

NORM_EPS = 1e-5

GLA_HEADS = 4
GLA_GATE_RANK = 16
GLA_GATE_TAU = 16.0
GLA_CHUNK = 64
N_Q_HEADS = 16
N_KV_HEADS = 4
HEAD_DIM = 64
GROUP = N_Q_HEADS // N_KV_HEADS
WINDOW = 128
ATT_BLOCK = 128
TOP_K = 2

LANES = 128
VMEM_LIMIT_BYTES = 56 * 2**20

ROW_TILE = 512
GLA_BLOCK = 512
GLA_GROUP = 256
GLA_HEADS_PER_STEP = 4
FFN_TILE = 1024
FFN_F_TILE = 256
MOE_TILE = 1024
COMBINE_TILE = 256
META_WIDTH = 8


def _params(*sem):
    return pltpu.CompilerParams(dimension_semantics=sem, vmem_limit_bytes=VMEM_LIMIT_BYTES)


def _rms(x, gain):
    y = x * lax.rsqrt(jnp.mean(x * x, axis=-1, keepdims=True) + NORM_EPS)
    return y * gain


def _silu(x):
    return x * (1.0 / (1.0 + jnp.exp(-x)))


def _norm_proj_kernel(x_ref, g_ref, w_ref, b_ref, *o_refs, splits):
    y = _rms(x_ref[...], g_ref[...]).astype(BF16)
    for (start, width), o_ref in zip(splits, o_refs):
        acc = jnp.dot(y, w_ref[:, start:start + width], preferred_element_type=F32)
        acc = acc + b_ref[:, start:start + width]
        o_ref[...] = acc.astype(o_ref.dtype)


def _norm_proj(x, gain, w, bias, splits, dtypes, name):
    n, d = x.shape
    nout = w.shape[1]
    tm = ROW_TILE
    assert n % tm == 0
    return pl.pallas_call(
        functools.partial(_norm_proj_kernel, splits=splits),
        grid=(n // tm,),
        in_specs=[
            pl.BlockSpec((tm, d), lambda i: (i, 0)),
            pl.BlockSpec((1, d), lambda i: (0, 0)),
            pl.BlockSpec((d, nout), lambda i: (0, 0)),
            pl.BlockSpec((1, nout), lambda i: (0, 0)),
        ],
        out_specs=[pl.BlockSpec((tm, wd), lambda i: (i, 0)) for (_, wd) in splits],
        out_shape=[jax.ShapeDtypeStruct((n, wd), dt) for (_, wd), dt in zip(splits, dtypes)],
        compiler_params=_params("arbitrary"),
        name=name,
    )(x, gain.reshape(1, d), w, bias.reshape(1, nout))


def _gla_direction(q_ref, k_ref, v_ref, lr_ref, wg_ref, bg_ref, state_ref, reverse):
    heads, dk, dv = state_ref.shape
    rows, width = q_ref.shape
    c = GLA_CHUNK
    nc = rows // c
    q = q_ref[...].astype(F32) * (dk ** -0.5)
    k = k_ref[...].astype(F32)
    v = v_ref[...]
    z = jnp.dot(lr_ref[...].astype(BF16), wg_ref[...], preferred_element_type=F32) + bg_ref[...]
    la = (jnp.minimum(z, 0.0) - jnp.log(1.0 + jnp.exp(-jnp.abs(z)))) * (1.0 / GLA_GATE_TAU)
    la_hi = la.astype(BF16)
    la_lo = (la - la_hi.astype(F32)).astype(BF16)

    tn = (((0,), (0,)), ((), ()))
    nt = (((1,), (1,)), ((), ()))
    grp = GLA_GROUP
    ng = rows // grp
    ri = lax.broadcasted_iota(jnp.int32, (grp, grp), 0)
    ci = lax.broadcasted_iota(jnp.int32, (grp, grp), 1)
    same_chunk = (ri // c) == (ci // c)
    if reverse:
        cum_mask = same_chunk & (ci >= ri)
        att_mask = same_chunk & (ci > ri)
        ref_row, last_row = c // 2 - 1, 0
    else:
        cum_mask = same_chunk & (ci <= ri)
        att_mask = cum_mask
        ref_row, last_row = c // 2, c - 1
    cum = jnp.where(cum_mask, 1.0, 0.0).astype(BF16)

    la_hl = jnp.concatenate([la_hi, la_lo], axis=1)
    b = jnp.concatenate(
        [jnp.dot(cum, la_hl[g * grp:(g + 1) * grp], preferred_element_type=F32) for g in range(ng)],
        axis=0)
    b = b[:, :width] + b[:, width:]
    b3 = b.reshape(nc, c, width)
    b_ref = b3[:, ref_row:ref_row + 1, :]
    b_last = b3[:, last_row:last_row + 1, :]
    q3 = q.reshape(nc, c, width)
    k3 = k.reshape(nc, c, width)
    qe = (q3 * jnp.exp(b3 - b_ref)).astype(BF16).reshape(rows, width)
    ke = (k3 * jnp.exp(b_ref - b3)).astype(BF16).reshape(rows, width)
    kd = (k3 * jnp.exp(b_last - b3)).astype(BF16).reshape(rows, width)
    qb = (q3 * jnp.exp(b3)).astype(BF16).reshape(rows, width)
    decay_rows = jnp.exp(b_last.reshape(nc, width))

    outs = []
    for h in range(heads):
        ks = slice(h * dk, (h + 1) * dk)
        vh = v[:, h * dv:(h + 1) * dv]
        o_intra = []
        for g in range(ng):
            sl = slice(g * grp, (g + 1) * grp)
            s = lax.dot_general(qe[sl, ks], ke[sl, ks], nt, preferred_element_type=F32)
            s = jnp.where(att_mask, s, 0.0).astype(BF16)
            o_intra.append(jnp.dot(s, vh[sl], preferred_element_type=F32))
        o_intra = jnp.concatenate(o_intra, axis=0)

        upd = [lax.dot_general(kd[j * c:(j + 1) * c, ks], vh[j * c:(j + 1) * c], tn,
                               preferred_element_type=F32) for j in range(nc)]
        decay_cols = jnp.concatenate([decay_rows[:, ks], jnp.zeros((dk - nc, dk), F32)], axis=0).T

        state = state_ref[h]
        o_inter = [None] * nc
        for j in (range(nc - 1, -1, -1) if reverse else range(nc)):
            o_inter[j] = jnp.dot(qb[j * c:(j + 1) * c, ks], state.astype(BF16),
                                 preferred_element_type=F32)
            state = state * decay_cols[:, j:j + 1] + upd[j]
        state_ref[h] = state
        outs.append(o_intra + jnp.concatenate(o_inter, axis=0))
    return jnp.concatenate(outs, axis=1)


def _gla_kernel(q_ref, k_ref, v_ref, lr_ref, r_ref, wgf_ref, bgf_ref, wgb_ref, bgb_ref, gain_ref,
                o_ref, state_ref, oacc_ref, *, nblk):
    i = pl.program_id(2)
    rows = q_ref.shape[0]
    heads, _, dv = state_ref.shape

    @pl.when((i == 0) | (i == nblk))
    def _():
        state_ref[...] = jnp.zeros_like(state_ref)

    @pl.when(i < nblk)
    def _():
        o = _gla_direction(q_ref, k_ref, v_ref, lr_ref, wgf_ref, bgf_ref, state_ref, False)
        oacc_ref[pl.ds(pl.multiple_of(i * rows, rows), rows), :] = o

    @pl.when(i >= nblk)
    def _():
        j = 2 * nblk - 1 - i
        o = _gla_direction(q_ref, k_ref, v_ref, lr_ref, wgb_ref, bgb_ref, state_ref, True)
        o = o + oacc_ref[pl.ds(pl.multiple_of(j * rows, rows), rows), :]
        gain = gain_ref[...]
        o = jnp.concatenate([_rms(o[:, h * dv:(h + 1) * dv], gain[:, h * dv:(h + 1) * dv])
                             for h in range(heads)], axis=1)
        o_ref[...] = (o * _silu(r_ref[...].astype(F32))).astype(o_ref.dtype)


def _gla(q, k, v, lr, r, wgf, bgf, wgb, bgb, gain, batch, seq):
    n = q.shape[0]
    heads = GLA_HEADS
    hps = GLA_HEADS_PER_STEP
    dk = q.shape[1] // heads
    dv = v.shape[1] // heads
    rows = GLA_BLOCK
    nblk = seq // rows
    assert seq % rows == 0 and rows % GLA_GROUP == 0 and GLA_GROUP % GLA_CHUNK == 0 and heads % hps == 0

    def blk(i):
        return jnp.where(i < nblk, i, 2 * nblk - 1 - i)

    def late_blk(i):
        return jnp.where(i < nblk, nblk - 1, 2 * nblk - 1 - i)

    row_map = lambda b, h, i: (b * nblk + blk(i), h)
    return pl.pallas_call(
        functools.partial(_gla_kernel, nblk=nblk),
        grid=(batch, heads // hps, 2 * nblk),
        in_specs=[
            pl.BlockSpec((rows, hps * dk), row_map),
            pl.BlockSpec((rows, hps * dk), row_map),
            pl.BlockSpec((rows, hps * dv), row_map),
            pl.BlockSpec((rows, lr.shape[1]), lambda b, h, i: (b * nblk + blk(i), 0)),
            pl.BlockSpec((rows, hps * dv), lambda b, h, i: (b * nblk + late_blk(i), h)),
            pl.BlockSpec((wgf.shape[0], hps * dk), lambda b, h, i: (0, h)),
            pl.BlockSpec((1, hps * dk), lambda b, h, i: (0, h)),
            pl.BlockSpec((wgb.shape[0], hps * dk), lambda b, h, i: (0, h)),
            pl.BlockSpec((1, hps * dk), lambda b, h, i: (0, h)),
            pl.BlockSpec((1, hps * dv), lambda b, h, i: (0, h)),
        ],
        out_specs=pl.BlockSpec((rows, hps * dv), lambda b, h, i: (b * nblk + late_blk(i), h)),
        out_shape=jax.ShapeDtypeStruct((n, heads * dv), BF16),
        scratch_shapes=[pltpu.VMEM((hps, dk, dv), F32), pltpu.VMEM((seq, hps * dv), F32)],
        compiler_params=_params("arbitrary", "arbitrary", "arbitrary"),
        name="gla",
    )(q, k, v, lr, r, wgf, bgf, wgb, bgb, gain)


def _proj_residual_router_kernel(a_ref, w_ref, b_ref, h_ref, g_ref, rhl_ref,
                                 h_out_ref, hn_out_ref, meta_ref, count_ref, tilecnt_ref, *, n_experts):
    i = pl.program_id(0)
    h = h_ref[...] + (jnp.dot(a_ref[...], w_ref[...], preferred_element_type=F32) + b_ref[...])
    h_out_ref[...] = h
    hn = _rms(h, g_ref[...])
    hn_out_ref[...] = hn.astype(hn_out_ref.dtype)

    hn_hi = hn.astype(BF16)
    hn_lo = (hn - hn_hi.astype(F32)).astype(BF16)
    hh = jnp.dot(hn_hi, rhl_ref[...], preferred_element_type=F32)
    logits = (hh[:, :LANES] + hh[:, LANES:]
              + jnp.dot(hn_lo, rhl_ref[:, :LANES], preferred_element_type=F32))
    tm = logits.shape[0]
    lane = lax.broadcasted_iota(jnp.int32, logits.shape, 1).astype(F32)
    neg = jnp.float32(-jnp.inf)
    logits = jnp.where(lane < n_experts, logits, neg)
    m1 = jnp.max(logits, axis=-1, keepdims=True)
    i1 = jnp.min(jnp.where(logits == m1, lane, float(LANES)), axis=-1, keepdims=True)
    rest = jnp.where(lane == i1, neg, logits)
    m2 = jnp.max(rest, axis=-1, keepdims=True)
    i2 = jnp.min(jnp.where(rest == m2, lane, float(LANES)), axis=-1, keepdims=True)
    e2 = jnp.exp(m2 - m1)
    w1 = 1.0 / (1.0 + e2)
    w2 = e2 / (1.0 + e2)

    @pl.when(i == 0)
    def _():
        count_ref[...] = jnp.zeros_like(count_ref)

    sel = (lane == i1) | (lane == i2)
    onehot = jnp.where(sel, 1.0, 0.0)
    ri = lax.broadcasted_iota(jnp.int32, (tm, tm), 0)
    ci = lax.broadcasted_iota(jnp.int32, (tm, tm), 1)
    strict_lower = jnp.where(ci < ri, 1.0, 0.0).astype(BF16)
    rank = jnp.dot(strict_lower, onehot.astype(BF16), preferred_element_type=F32) + count_ref[...]
    r1 = jnp.sum(jnp.where(lane == i1, rank, 0.0), axis=-1, keepdims=True)
    r2 = jnp.sum(jnp.where(lane == i2, rank, 0.0), axis=-1, keepdims=True)
    count_ref[...] = count_ref[...] + jnp.sum(onehot, axis=0, keepdims=True)
    sub = tm // tilecnt_ref.shape[0]
    tilecnt_ref[...] = jnp.concatenate(
        [jnp.sum(onehot[u * sub:(u + 1) * sub], axis=0, keepdims=True) for u in range(tilecnt_ref.shape[0])],
        axis=0)

    meta = jnp.where(lane == 0, i1, 0.0)
    meta = jnp.where(lane == 1, i2, meta)
    meta = jnp.where(lane == 2, w1, meta)
    meta = jnp.where(lane == 3, w2, meta)
    meta = jnp.where(lane == 4, r1, meta)
    meta = jnp.where(lane == 5, r2, meta)
    meta_ref[...] = meta[:, :meta_ref.shape[1]]


def _proj_residual_router(a, w, bias, h, gain, router, name):
    n, kdim = a.shape
    d = w.shape[1]
    tm = ROW_TILE
    assert n % tm == 0 and tm % COMBINE_TILE == 0
    sub = tm // COMBINE_TILE
    n_experts = router.shape[1]
    rpad = jnp.zeros((d, LANES), F32).at[:, :n_experts].set(router)
    rhi = rpad.astype(BF16)
    rlo = (rpad - rhi.astype(F32)).astype(BF16)
    row = lambda i: (i, 0)
    fixed = lambda i: (0, 0)
    return pl.pallas_call(
        functools.partial(_proj_residual_router_kernel, n_experts=n_experts),
        grid=(n // tm,),
        in_specs=[
            pl.BlockSpec((tm, kdim), row),
            pl.BlockSpec((kdim, d), fixed),
            pl.BlockSpec((1, d), fixed),
            pl.BlockSpec((tm, d), row),
            pl.BlockSpec((1, d), fixed),
            pl.BlockSpec((d, 2 * LANES), fixed),
        ],
        out_specs=[pl.BlockSpec((tm, d), row), pl.BlockSpec((tm, d), row),
                   pl.BlockSpec((tm, META_WIDTH), row), pl.BlockSpec((1, LANES), fixed),
                   pl.BlockSpec((None, sub, LANES), lambda i: (i, 0, 0))],
        out_shape=[jax.ShapeDtypeStruct((n, d), F32), jax.ShapeDtypeStruct((n, d), BF16),
                   jax.ShapeDtypeStruct((n, META_WIDTH), F32), jax.ShapeDtypeStruct((1, LANES), F32),
                   jax.ShapeDtypeStruct((n // tm, sub, LANES), F32)],
        compiler_params=_params("arbitrary"),
        name=name,
    )(a, w, bias.reshape(1, d), h, gain.reshape(1, d), jnp.concatenate([rhi, rlo], axis=1))


def _swiglu_step(x, wgu, wd, acc_ref):
    tf = wd.shape[0]
    gu = jnp.dot(x, wgu, preferred_element_type=F32)
    a = (_silu(gu[:, :tf]) * gu[:, tf:]).astype(BF16)
    acc_ref[...] += jnp.dot(a, wd, preferred_element_type=F32)


def _moe_swiglu_kernel(te_ref, nu_ref, x_ref, wg_ref, wu_ref, wd_ref, o_ref, acc_ref):
    i = pl.program_id(0)
    j = pl.program_id(1)
    used = i < nu_ref[0]

    @pl.when(used & (j == 0))
    def _():
        acc_ref[...] = jnp.zeros_like(acc_ref)

    @pl.when(used)
    def _():
        wgu = jnp.concatenate([wg_ref[...].astype(BF16), wu_ref[...].astype(BF16)], axis=1)
        _swiglu_step(x_ref[...].astype(BF16), wgu, wd_ref[...].astype(BF16), acc_ref)

    @pl.when(j == pl.num_programs(1) - 1)
    def _():
        @pl.when(used)
        def _():
            o_ref[...] = acc_ref[...].astype(o_ref.dtype)

        @pl.when(jnp.logical_not(used))
        def _():
            o_ref[...] = jnp.zeros_like(o_ref)


def _swiglu_weights(wg, wu, wd):
    e, d, f = wg.shape
    tf = FFN_F_TILE
    assert f % tf == 0
    nf = f // tf
    tiles = []
    for j in range(nf):
        tiles += [wg[:, :, j * tf:(j + 1) * tf], wu[:, :, j * tf:(j + 1) * tf]]
    wgu = jnp.concatenate(tiles, axis=-1).astype(BF16)
    return wgu, wd.astype(BF16).reshape(e, nf, tf, d)


def _moe_swiglu(x, wg, wu, wd, tile_expert, n_used, tm, out_dtype):
    rows, d = x.shape
    tf = FFN_F_TILE
    f = wg.shape[2]
    assert rows % tm == 0 and f % tf == 0
    nf = f // tf

    def fcol(i, j, nu):
        return jnp.where(i < nu[0], j, nf - 1)

    def xmap(i, j, te, nu):
        return (jnp.where(i < nu[0], i, nu[0] - 1), 0)

    return pl.pallas_call(
        _moe_swiglu_kernel,
        grid_spec=pltpu.PrefetchScalarGridSpec(
            num_scalar_prefetch=2,
            grid=(rows // tm, nf),
            in_specs=[
                pl.BlockSpec((tm, d), xmap),
                pl.BlockSpec((None, d, tf), lambda i, j, te, nu: (te[i], 0, fcol(i, j, nu))),
                pl.BlockSpec((None, d, tf), lambda i, j, te, nu: (te[i], 0, fcol(i, j, nu))),
                pl.BlockSpec((None, tf, d), lambda i, j, te, nu: (te[i], fcol(i, j, nu), 0)),
            ],
            out_specs=pl.BlockSpec((tm, d), lambda i, j, te, nu: (i, 0)),
            scratch_shapes=[pltpu.VMEM((tm, d), F32)],
        ),
        out_shape=jax.ShapeDtypeStruct((rows, d), out_dtype),
        compiler_params=_params("arbitrary", "arbitrary"),
        name="moe_swiglu",
    )(tile_expert, n_used, x, wg, wu, wd)


def _proj_swiglu_kernel(a_ref, wo_ref, h_ref, g_ref, wgu_ref, wd_ref, o_ref, hres_ref, xn_ref, acc_ref):
    j = pl.program_id(1)

    @pl.when(j == 0)
    def _():
        h = h_ref[...] + jnp.dot(a_ref[...], wo_ref[...], preferred_element_type=F32)
        hres_ref[...] = h
        xn_ref[...] = _rms(h, g_ref[...]).astype(xn_ref.dtype)
        acc_ref[...] = jnp.zeros_like(acc_ref)

    _swiglu_step(xn_ref[...], wgu_ref[...], wd_ref[...], acc_ref)

    @pl.when(j == pl.num_programs(1) - 1)
    def _():
        o_ref[...] = hres_ref[...] + acc_ref[...]


def _proj_swiglu(a, wo, h, gain, wgu, wd):
    n, kdim = a.shape
    d = wo.shape[1]
    nf, tf = wd.shape[1], wd.shape[2]
    tm = FFN_TILE
    assert n % tm == 0
    return pl.pallas_call(
        _proj_swiglu_kernel,
        grid=(n // tm, nf),
        in_specs=[
            pl.BlockSpec((tm, kdim), lambda i, j: (i, 0)),
            pl.BlockSpec((kdim, d), lambda i, j: (0, 0)),
            pl.BlockSpec((tm, d), lambda i, j: (i, 0)),
            pl.BlockSpec((1, d), lambda i, j: (0, 0)),
            pl.BlockSpec((None, d, 2 * tf), lambda i, j: (0, 0, j)),
            pl.BlockSpec((None, None, tf, d), lambda i, j: (0, j, 0, 0)),
        ],
        out_specs=pl.BlockSpec((tm, d), lambda i, j: (i, 0)),
        out_shape=jax.ShapeDtypeStruct((n, d), F32),
        scratch_shapes=[pltpu.VMEM((tm, d), F32), pltpu.VMEM((tm, d), BF16), pltpu.VMEM((tm, d), F32)],
        compiler_params=_params("arbitrary", "arbitrary"),
        name="gla_out_proj_dense_swiglu",
    )(a, wo, h, gain.reshape(1, d), wgu, wd)


def _alibi_slope(head):
    return float(2.0 ** (-8.0 * (head + 1) / N_Q_HEADS))


def _swa_bias(t):
    qi = jnp.arange(t)[:, None]
    kj = jnp.arange(3 * t)[None, :]
    dist = jnp.abs(qi + t - kj)
    band = dist <= WINDOW
    valid = jnp.stack([band & (kj >= t), band, band & (kj < 2 * t)])
    slopes = jnp.asarray([_alibi_slope(h) for h in range(N_Q_HEADS)], F32)
    bias = -slopes[None, :, None, None] * dist.astype(F32)[None, None]
    bias = jnp.where(valid[:, None], bias, -jnp.inf)
    return bias.reshape(3, N_KV_HEADS, GROUP * t, 3 * t)


def _swa_kernel(sink_ref, q_ref, kp_ref, kc_ref, kn_ref, vp_ref, vc_ref, vn_ref, bias0_ref, bias1_ref,
                o_ref):
    t = ATT_BLOCK
    lane = lax.broadcasted_iota(jnp.int32, (t, LANES), 1)
    low = lane < HEAD_DIM
    row_head = lax.broadcasted_iota(jnp.int32, (GROUP * t, 1), 0) // t
    ones = jnp.ones((3 * t, LANES), BF16)

    def window(p_ref, c_ref, n_ref, kvh, sub):
        ks = slice(kvh * LANES, (kvh + 1) * LANES)
        if sub == 0:
            return jnp.concatenate([p_ref[t:, ks], c_ref[:, ks]], axis=0)
        return jnp.concatenate([c_ref[:, ks], n_ref[:t, ks]], axis=0)

    row_blocks = []
    for sub, bias_ref in enumerate((bias0_ref, bias1_ref)):
        rows = slice(sub * t, (sub + 1) * t)
        blocks = []
        for kvh in range(N_KV_HEADS):
            qs = []
            for g in range(GROUP):
                head = kvh * GROUP + g
                col = (head // 2) * LANES
                q2 = q_ref[rows, col:col + LANES]
                qs.append(jnp.where(low if head % 2 == 0 else jnp.logical_not(low), q2,
                                    jnp.zeros_like(q2)))
            qg = jnp.concatenate(qs, axis=0)
            kw = window(kp_ref, kc_ref, kn_ref, kvh, sub)
            s = lax.dot_general(qg, kw, (((1,), (1,)), ((), ())), preferred_element_type=F32)
            s = s + bias_ref[kvh]
            sink = jnp.full((GROUP * t, 1), sink_ref[kvh * GROUP], F32)
            for g in range(1, GROUP):
                sink = jnp.where(row_head == g, sink_ref[kvh * GROUP + g], sink)
            m = jnp.maximum(jnp.max(s, axis=-1, keepdims=True), sink)
            p = jnp.exp(s - m).astype(BF16)
            vw = jnp.concatenate([window(vp_ref, vc_ref, vn_ref, kvh, sub), ones], axis=1)
            res = jnp.dot(p, vw, preferred_element_type=F32)
            out = res[:, :LANES] * (1.0 / (res[:, LANES:] + jnp.exp(sink - m)))
            for pair in range(GROUP // 2):
                a = out[(2 * pair) * t:(2 * pair + 1) * t]
                b = out[(2 * pair + 1) * t:(2 * pair + 2) * t]
                blocks.append(jnp.where(low, a, b).astype(o_ref.dtype))
        row_blocks.append(jnp.concatenate(blocks, axis=1))
    o_ref[...] = jnp.concatenate(row_blocks, axis=0)


def _swa(q, kdup, vdup, sinks, batch, seq):
    n, qw = q.shape
    t = ATT_BLOCK
    rows = 2 * t
    nblk = seq // rows
    assert seq % rows == 0 and nblk >= 2
    kvw = kdup.shape[1]
    bias = _swa_bias(t)
    prev = lambda b, i, s: (b * nblk + jnp.maximum(i - 1, 0), 0)
    cur = lambda b, i, s: (b * nblk + i, 0)
    nxt = lambda b, i, s: (b * nblk + jnp.minimum(i + 1, nblk - 1), 0)
    first = lambda b, i, s: (jnp.where(i == 0, 0, 1), 0, 0, 0)
    second = lambda b, i, s: (jnp.where(i == nblk - 1, 2, 1), 0, 0, 0)
    return pl.pallas_call(
        _swa_kernel,
        grid_spec=pltpu.PrefetchScalarGridSpec(
            num_scalar_prefetch=1,
            grid=(batch, nblk),
            in_specs=[
                pl.BlockSpec((rows, qw), cur),
                pl.BlockSpec((rows, kvw), prev), pl.BlockSpec((rows, kvw), cur), pl.BlockSpec((rows, kvw), nxt),
                pl.BlockSpec((rows, kvw), prev), pl.BlockSpec((rows, kvw), cur), pl.BlockSpec((rows, kvw), nxt),
                pl.BlockSpec((None,) + bias.shape[1:], first),
                pl.BlockSpec((None,) + bias.shape[1:], second),
            ],
            out_specs=pl.BlockSpec((rows, qw), cur),
        ),
        out_shape=jax.ShapeDtypeStruct((n, qw), BF16),
        compiler_params=_params("arbitrary", "arbitrary"),
        name="swa",
    )(sinks, q, kdup, kdup, kdup, vdup, vdup, vdup, bias, bias)


SEG_ALIGN = 8
SEG_BIG = 64
RUN_FIELDS = 7


def _block_rows(tm, n_experts):
    worst = TOP_K * tm + 2 * (SEG_ALIGN - 1) * n_experts
    return -(-worst // LANES) * LANES


def _pair_block_rows(meta, base_ref, tile, n_experts):
    row1, row2 = meta[:, 4:5], meta[:, 5:6]
    for e in range(n_experts):
        base = base_ref[tile * n_experts + e]
        row1 = row1 + jnp.where(meta[:, 0:1] == float(e), base, 0.0)
        row2 = row2 + jnp.where(meta[:, 1:2] == float(e), base, 0.0)
    return row1, row2


def _for_pieces(n_rows, make_copy, act):
    n_big = lax.shift_right_logical(n_rows, 6)
    n_small = lax.shift_right_logical(n_rows - n_big * SEG_BIG, 3)

    def big(j, carry):
        act(make_copy(j * SEG_BIG, SEG_BIG))
        return carry

    def small(j, carry):
        act(make_copy(n_big * SEG_BIG + j * SEG_ALIGN, SEG_ALIGN))
        return carry

    lax.fori_loop(0, n_big, big, 0)
    lax.fori_loop(0, n_small, small, 0)


def _for_rows(first, count, make_copy, act):
    def one(j, carry):
        act(make_copy(first + j))
        return carry

    lax.fori_loop(0, count, one, 0)


def _start(copy):
    copy.start()


def _wait(copy):
    copy.wait()


def _scatter_kernel(runs_ref, totals_ref, base_ref, pads_ref, x_ref, meta_ref, xs_ref, blk_ref, zero_ref,
                    sems, zero_sem, *, n_experts):
    i = pl.program_id(0)
    steps = pl.num_programs(0)
    tm = x_ref.shape[0]
    rows = blk_ref.shape[1]
    n_runs = steps * n_experts

    def start_tile(tile):
        slot = lax.rem(tile, 2)
        for e in range(n_experts):
            k = tile * n_experts + e
            head0, head_n, mid0, mid_len, tail0, tail_n, shift = [runs_ref[f * n_runs + k]
                                                                  for f in range(RUN_FIELDS)]

            def row_copy(j, head0=head0, head_n=head_n, tail0=tail0, shift=shift):
                r = jnp.where(j < head_n, head0 + j, tail0 + (j - head_n))
                return pltpu.make_async_copy(blk_ref.at[slot, pl.ds(r - shift, 1)],
                                             xs_ref.at[pl.ds(r, 1)], sems.at[slot])

            def piece(off, size, mid0=mid0, shift=shift):
                src = pl.multiple_of(mid0 - shift + off, SEG_ALIGN)
                dst = pl.multiple_of(mid0 + off, SEG_ALIGN)
                return pltpu.make_async_copy(blk_ref.at[slot, pl.ds(src, size)],
                                             xs_ref.at[pl.ds(dst, size)], sems.at[slot])

            _for_rows(0, head_n + tail_n, row_copy, _start)
            _for_pieces(mid_len, piece, _start)

    def wait_tile(tile):
        slot = lax.rem(tile, 2)
        for f, size in enumerate((1, SEG_BIG, SEG_ALIGN)):
            def same_size_copy(j, size=size):
                return pltpu.make_async_copy(blk_ref.at[slot, pl.ds(0, size)], xs_ref.at[pl.ds(0, size)],
                                             sems.at[slot])

            _for_rows(0, totals_ref[f * steps + tile], same_size_copy, _wait)

    @pl.when(i == 0)
    def _():
        zero_ref[...] = jnp.zeros_like(zero_ref)
        for act in (_start, _wait):
            for e in range(n_experts):
                pad0, head_n, mid_len = [pads_ref[f * n_experts + e] for f in range(3)]

                def zero_row(r):
                    return pltpu.make_async_copy(zero_ref.at[pl.ds(0, 1)], xs_ref.at[pl.ds(r, 1)], zero_sem)

                def zero_piece(off, size, pad0=pad0, head_n=head_n):
                    dst = pl.multiple_of(pad0 + head_n + off, SEG_ALIGN)
                    return pltpu.make_async_copy(zero_ref.at[pl.ds(0, size)], xs_ref.at[pl.ds(dst, size)],
                                                 zero_sem)

                _for_rows(pad0, head_n, zero_row, act)
                _for_pieces(mid_len, zero_piece, act)

    @pl.when(i >= 2)
    def _():
        wait_tile(i - 2)

    row1, row2 = _pair_block_rows(meta_ref[...], base_ref, i, n_experts)
    col = lax.broadcasted_iota(jnp.int32, (tm, rows), 1).astype(F32)
    pick = jnp.where((col == row1) | (col == row2), 1.0, 0.0).astype(BF16)
    blk_ref[lax.rem(i, 2)] = lax.dot_general(pick, x_ref[...], (((0,), (0,)), ((), ())),
                                             preferred_element_type=F32)
    start_tile(i)

    @pl.when(i == steps - 1)
    def _():
        @pl.when(i >= 1)
        def _():
            wait_tile(i - 1)

        wait_tile(i)


def _scatter_rows(x, meta, runs, totals, row_base, pads, total_rows, n_experts):
    n, d = x.shape
    tm = COMBINE_TILE
    assert n % tm == 0
    rows = _block_rows(tm, n_experts)
    smem = pl.BlockSpec(memory_space=pltpu.SMEM)
    return pl.pallas_call(
        functools.partial(_scatter_kernel, n_experts=n_experts),
        grid=(n // tm,),
        in_specs=[
            smem, smem, smem, smem,
            pl.BlockSpec((tm, d), lambda i: (i, 0)),
            pl.BlockSpec((tm, META_WIDTH), lambda i: (i, 0)),
        ],
        out_specs=pl.BlockSpec(memory_space=pl.ANY),
        out_shape=jax.ShapeDtypeStruct((total_rows, d), F32),
        scratch_shapes=[pltpu.VMEM((2, rows, d), F32), pltpu.VMEM((SEG_BIG, d), F32),
                        pltpu.SemaphoreType.DMA((2,)), pltpu.SemaphoreType.DMA],
        compiler_params=_params("arbitrary"),
        name="moe_scatter",
    )(runs, totals, row_base, pads, x, meta)


def _combine_kernel(src_ref, len_ref, dst_ref, totals_ref, base_ref, h_ref, meta_ref, g_ref, ys_ref, o_ref,
                    yblk_ref, sems, *, n_experts):
    i = pl.program_id(0)
    steps = pl.num_programs(0)
    tm = h_ref.shape[0]
    rows = yblk_ref.shape[1]

    def start_tile(tile):
        slot = lax.rem(tile, 2)
        for e in range(n_experts):
            k = tile * n_experts + e

            def piece(off, size, src0=src_ref[k], dst0=dst_ref[k]):
                src = pl.multiple_of(src0 + off, SEG_ALIGN)
                dst = pl.multiple_of(dst0 + off, SEG_ALIGN)
                return pltpu.make_async_copy(ys_ref.at[pl.ds(src, size)],
                                             yblk_ref.at[slot, pl.ds(dst, size)], sems.at[slot])

            _for_pieces(len_ref[k], piece, _start)

    def wait_tile(tile):
        slot = lax.rem(tile, 2)
        for f, size in enumerate((SEG_BIG, SEG_ALIGN)):
            def same_size_copy(j, size=size):
                return pltpu.make_async_copy(ys_ref.at[pl.ds(0, size)], yblk_ref.at[slot, pl.ds(0, size)],
                                             sems.at[slot])

            _for_rows(0, totals_ref[f * steps + tile], same_size_copy, _wait)

    @pl.when(i == 0)
    def _():
        yblk_ref[...] = jnp.zeros_like(yblk_ref)
        start_tile(i)

    @pl.when(i + 1 < steps)
    def _():
        start_tile(i + 1)

    wait_tile(i)

    meta = meta_ref[...]
    w1, w2 = meta[:, 2:3], meta[:, 3:4]
    row1, row2 = _pair_block_rows(meta, base_ref, i, n_experts)
    col = lax.broadcasted_iota(jnp.int32, (tm, rows), 1).astype(F32)
    pick = jnp.concatenate([jnp.where(col == row1, 1.0, 0.0),
                            jnp.where(col == row2, 1.0, 0.0)], axis=0).astype(BF16)
    yblk = yblk_ref[lax.rem(i, 2)].astype(BF16)
    y = jnp.dot(pick, yblk, preferred_element_type=F32)
    o_ref[...] = _rms(h_ref[...] + (w1 * y[:tm] + w2 * y[tm:]), g_ref[...])


def _combine(h, meta, gain, ys, seg_src, seg_len, seg_dst, row_base, n_experts):
    n, d = h.shape
    tm = COMBINE_TILE
    assert n % tm == 0
    rows = _block_rows(tm, n_experts)
    smem = pl.BlockSpec(memory_space=pltpu.SMEM)
    lens = seg_len.reshape(n // tm, n_experts)
    totals = jnp.stack([jnp.sum(lens // SEG_BIG, axis=1),
                        jnp.sum(lens % SEG_BIG // SEG_ALIGN, axis=1)]).reshape(-1)
    return pl.pallas_call(
        functools.partial(_combine_kernel, n_experts=n_experts),
        grid=(n // tm,),
        in_specs=[
            smem, smem, smem, smem, smem,
            pl.BlockSpec((tm, d), lambda i: (i, 0)),
            pl.BlockSpec((tm, META_WIDTH), lambda i: (i, 0)),
            pl.BlockSpec((1, d), lambda i: (0, 0)),
            pl.BlockSpec(memory_space=pl.ANY),
        ],
        out_specs=pl.BlockSpec((tm, d), lambda i: (i, 0)),
        out_shape=jax.ShapeDtypeStruct((n, d), F32),
        scratch_shapes=[pltpu.VMEM((2, rows, d), ys.dtype), pltpu.SemaphoreType.DMA((2,))],
        compiler_params=_params("arbitrary"),
        name="moe_combine",
    )(seg_src, seg_len, seg_dst, totals, row_base, h, meta, gain.reshape(1, d), ys)


def kernel(x, mix_norm, ffn_norm, gla_in_proj, gla_gate_w_fwd, gla_gate_b_fwd, gla_gate_w_bwd, gla_gate_b_bwd, gla_head_norm, gla_out_proj, swa_qkv_proj, swa_qkv_bias, swa_sinks, swa_out_proj, swa_out_bias, dense_w_gate, dense_w_up, dense_w_down, moe_router, moe_w_gate, moe_w_up, moe_w_down, final_norm):
    batch, seq, d = x.shape
    n = batch * seq
    h0 = x.reshape(n, d)

    key_w = gla_gate_w_fwd.shape[2]
    val_w = gla_head_norm.shape[1]
    rank = GLA_GATE_RANK
    in_w = gla_in_proj.shape[2]
    splits = ((0, key_w), (key_w, key_w), (2 * key_w, val_w), (2 * key_w + val_w, val_w),
              (2 * key_w + 2 * val_w, 2 * rank))
    q, k, v, r, lr = _norm_proj(h0, mix_norm[0], gla_in_proj[0].astype(BF16), jnp.zeros((in_w,), F32),
                                splits, (BF16, BF16, BF16, BF16, F32), "gla_in_proj")
    zero_gate = jnp.zeros((rank, key_w), F32)
    wgf = jnp.concatenate([gla_gate_w_fwd[0], zero_gate], axis=0).astype(BF16)
    wgb = jnp.concatenate([zero_gate, gla_gate_w_bwd[0]], axis=0).astype(BF16)
    og = _gla(q, k, v, lr, r, wgf, gla_gate_b_fwd[0].reshape(1, key_w), wgb,
              gla_gate_b_bwd[0].reshape(1, key_w), gla_head_norm[0].reshape(1, val_w), batch, seq)

    dense_wgu, dense_wd = _swiglu_weights(dense_w_gate, dense_w_up, dense_w_down)
    h2 = _proj_swiglu(og, gla_out_proj[0].astype(BF16), h0, ffn_norm[0], dense_wgu, dense_wd)

    qw = N_Q_HEADS * HEAD_DIM
    kvw = N_KV_HEADS * HEAD_DIM
    scale = HEAD_DIM ** -0.5
    wqkv, bqkv = swa_qkv_proj[0], swa_qkv_bias[0]

    def dup(m):
        lead = m.shape[:-1]
        m = m.reshape(lead + (N_KV_HEADS, 1, HEAD_DIM))
        return jnp.broadcast_to(m, lead + (N_KV_HEADS, 2, HEAD_DIM)).reshape(lead + (2 * kvw,))

    w_aug = jnp.concatenate([wqkv[:, :qw] * scale, dup(wqkv[:, qw:qw + kvw]), dup(wqkv[:, qw + kvw:])], axis=1)
    b_aug = jnp.concatenate([bqkv[:qw] * scale, dup(bqkv[qw:qw + kvw]), dup(bqkv[qw + kvw:])], axis=0)
    splits = ((0, qw), (qw, 2 * kvw), (qw + 2 * kvw, 2 * kvw))
    aq, ak, av = _norm_proj(h2, mix_norm[1], w_aug.astype(BF16), b_aug, splits, (BF16, BF16, BF16),
                            "swa_qkv_proj")
    oa = _swa(aq, ak, av, swa_sinks[0], batch, seq)

    n_experts = moe_router.shape[2]
    h3, hn3, meta, counts, tile_counts = _proj_residual_router(
        oa, swa_out_proj[0].astype(BF16), swa_out_bias[0], h2, ffn_norm[1], moe_router[0], "swa_out_proj")
    tm = MOE_TILE
    counts = counts[0, :n_experts].astype(jnp.int32)
    tiles_per_expert = (counts + tm - 1) // tm
    tile_end = jnp.cumsum(tiles_per_expert)
    offsets = (tile_end - tiles_per_expert) * tm
    n_tiles = (TOP_K * n) // tm + n_experts
    tile_ids = jnp.arange(n_tiles, dtype=jnp.int32)
    tile_expert = jnp.sum((tile_ids[:, None] >= tile_end[None, :]).astype(jnp.int32), axis=1)
    tile_expert = jnp.minimum(tile_expert, n_experts - 1)
    n_used = tile_end[-1:].astype(jnp.int32)

    align = SEG_ALIGN
    tile_cnt = tile_counts.reshape(-1, LANES)[:, :n_experts].astype(jnp.int32)
    run_start = offsets[None, :] + jnp.cumsum(tile_cnt, axis=0) - tile_cnt
    run_end = run_start + tile_cnt
    seg_src = run_start - run_start % align
    seg_len = jnp.where(tile_cnt > 0, (run_end + align - 1) // align * align - seg_src, 0)
    seg_dst = jnp.cumsum(seg_len, axis=1) - seg_len
    row_base = (offsets[None, :] + seg_dst - seg_src).astype(F32).reshape(-1)
    mid_start = (run_start + align - 1) // align * align
    mid_end = run_end - run_end % align
    has_mid = mid_end >= mid_start
    head_n = jnp.where(has_mid, mid_start - run_start, tile_cnt)
    mid_len = jnp.where(has_mid, mid_end - mid_start, 0)
    tail_n = jnp.where(has_mid, run_end - mid_end, 0)
    runs = jnp.stack([run_start, head_n, mid_start, mid_len, mid_end, tail_n, seg_src - seg_dst]).reshape(-1)
    totals = jnp.stack([jnp.sum(head_n + tail_n, axis=1), jnp.sum(mid_len // SEG_BIG, axis=1),
                        jnp.sum(mid_len % SEG_BIG // align, axis=1)]).reshape(-1)
    pad_start = offsets + counts
    pad_count = tiles_per_expert * tm - counts
    pad_count = pad_count.at[n_experts - 1].set(n_tiles * tm - pad_start[n_experts - 1])
    pad_head = jnp.minimum(pad_count, (align - pad_start % align) % align)
    pads = jnp.stack([pad_start, pad_head, pad_count - pad_head]).reshape(-1)

    xs = _scatter_rows(hn3, meta, runs, totals, row_base, pads, n_tiles * tm, n_experts)
    ys = _moe_swiglu(xs, moe_w_gate[0], moe_w_up[0], moe_w_down[0], tile_expert, n_used, tm, F32)
    out = _combine(h3, meta, final_norm, ys, seg_src.reshape(-1), seg_len.reshape(-1),
                   seg_dst.reshape(-1), row_base, n_experts)
    return out.reshape(batch, seq, d)
```

```python
import functools

import jax
import jax.numpy as jnp
from jax import lax
from jax.experimental import pallas as pl
from jax.experimental.pallas import tpu as pltpu

F32 = jnp.float32
BF16 = jnp.bfloat16

NORM_EPS = 1e-5

GLA_HEADS = 4
GLA_GATE_RANK = 16
GLA_GATE_TAU = 16.0
GLA_CHUNK = 64
N_Q_HEADS = 16
N_KV_HEADS = 4
HEAD_DIM = 64
GROUP = N_Q_HEADS // N_KV_HEADS
WINDOW = 128
ATT_BLOCK = 128
TOP_K = 2

LANES = 128
VMEM_LIMIT_BYTES = 56 * 2**20

ROW_TILE = 512
GLA_BLOCK = 512
GLA_GROUP = 256
GLA_HEADS_PER_STEP = 4
FFN_TILE = 1024
FFN_F_TILE = 256
MOE_TILE = 1024
SCATTER_TILE = 512
COMBINE_TILE = 256
META_WIDTH = 8


def _params(*sem):
    return pltpu.CompilerParams(dimension_semantics=sem, vmem_limit_bytes=VMEM_LIMIT_BYTES)


def _rms(x, gain):
    y = x * lax.rsqrt(jnp.mean(x * x, axis=-1, keepdims=True) + NORM_EPS)
    return y * gain


def _silu(x):
    return x * (1.0 / (1.0 + jnp.exp(-x)))


def _norm_proj_kernel(x_ref, g_ref, w_ref, b_ref, *o_refs, splits):
    y = _rms(x_ref[...], g_ref[...]).astype(BF16)
    for (start, width), o_ref in zip(splits, o_refs):
        acc = jnp.dot(y, w_ref[:, start:start + width], preferred_element_type=F32)
        acc = acc + b_ref[:, start:start + width]
        o_ref[...] = acc.astype(o_ref.dtype)


def _norm_proj(x, gain, w, bias, splits, dtypes, name):
    n, d = x.shape
    nout = w.shape[1]
    tm = ROW_TILE
    assert n % tm == 0
    return pl.pallas_call(
        functools.partial(_norm_proj_kernel, splits=splits),
        grid=(n // tm,),
        in_specs=[
            pl.BlockSpec((tm, d), lambda i: (i, 0)),
            pl.BlockSpec((1, d), lambda i: (0, 0)),
            pl.BlockSpec((d, nout), lambda i: (0, 0)),
            pl.BlockSpec((1, nout), lambda i: (0, 0)),
        ],
        out_specs=[pl.BlockSpec((tm, wd), lambda i: (i, 0)) for (_, wd) in splits],
        out_shape=[jax.ShapeDtypeStruct((n, wd), dt) for (_, wd), dt in zip(splits, dtypes)],
        compiler_params=_params("arbitrary"),
        name=name,
    )(x, gain.reshape(1, d), w, bias.reshape(1, nout))


def _gla_direction(q_ref, k_ref, v_ref, lr_ref, wg_ref, bg_ref, state_ref, reverse):
    heads, dk, dv = state_ref.shape
    rows, width = q_ref.shape
    c = GLA_CHUNK
    nc = rows // c
    q = q_ref[...].astype(F32) * (dk ** -0.5)
    k = k_ref[...].astype(F32)
    v = v_ref[...]
    z = jnp.dot(lr_ref[...].astype(BF16), wg_ref[...], preferred_element_type=F32) + bg_ref[...]
    la = (jnp.minimum(z, 0.0) - jnp.log(1.0 + jnp.exp(-jnp.abs(z)))) * (1.0 / GLA_GATE_TAU)
    la_hi = la.astype(BF16)
    la_lo = (la - la_hi.astype(F32)).astype(BF16)

    tn = (((0,), (0,)), ((), ()))
    nt = (((1,), (1,)), ((), ()))
    grp = GLA_GROUP
    ng = rows // grp
    ri = lax.broadcasted_iota(jnp.int32, (grp, grp), 0)
    ci = lax.broadcasted_iota(jnp.int32, (grp, grp), 1)
    same_chunk = (ri // c) == (ci // c)
    if reverse:
        cum_mask = same_chunk & (ci >= ri)
        att_mask = same_chunk & (ci > ri)
        ref_row, last_row = c // 2 - 1, 0
    else:
        cum_mask = same_chunk & (ci <= ri)
        att_mask = cum_mask
        ref_row, last_row = c // 2, c - 1
    cum = jnp.where(cum_mask, 1.0, 0.0).astype(BF16)

    la_hl = jnp.concatenate([la_hi, la_lo], axis=1)
    b = jnp.concatenate(
        [jnp.dot(cum, la_hl[g * grp:(g + 1) * grp], preferred_element_type=F32) for g in range(ng)],
        axis=0)
    b = b[:, :width] + b[:, width:]
    b3 = b.reshape(nc, c, width)
    b_ref = b3[:, ref_row:ref_row + 1, :]
    b_last = b3[:, last_row:last_row + 1, :]
    q3 = q.reshape(nc, c, width)
    k3 = k.reshape(nc, c, width)
    qe = (q3 * jnp.exp(b3 - b_ref)).astype(BF16).reshape(rows, width)
    ke = (k3 * jnp.exp(b_ref - b3)).astype(BF16).reshape(rows, width)
    kd = (k3 * jnp.exp(b_last - b3)).astype(BF16).reshape(rows, width)
    qb = (q3 * jnp.exp(b3)).astype(BF16).reshape(rows, width)
    decay_rows = jnp.exp(b_last.reshape(nc, width))

    outs = []
    for h in range(heads):
        ks = slice(h * dk, (h + 1) * dk)
        vh = v[:, h * dv:(h + 1) * dv]
        o_intra = []
        for g in range(ng):
            sl = slice(g * grp, (g + 1) * grp)
            s = lax.dot_general(qe[sl, ks], ke[sl, ks], nt, preferred_element_type=F32)
            s = jnp.where(att_mask, s, 0.0).astype(BF16)
            o_intra.append(jnp.dot(s, vh[sl], preferred_element_type=F32))
        o_intra = jnp.concatenate(o_intra, axis=0)

        upd = [lax.dot_general(kd[j * c:(j + 1) * c, ks], vh[j * c:(j + 1) * c], tn,
                               preferred_element_type=F32) for j in range(nc)]
        decay_cols = jnp.concatenate([decay_rows[:, ks], jnp.zeros((dk - nc, dk), F32)], axis=0).T

        state = state_ref[h]
        o_inter = [None] * nc
        for j in (range(nc - 1, -1, -1) if reverse else range(nc)):
            o_inter[j] = jnp.dot(qb[j * c:(j + 1) * c, ks], state.astype(BF16),
                                 preferred_element_type=F32)
            state = state * decay_cols[:, j:j + 1] + upd[j]
        state_ref[h] = state
        outs.append(o_intra + jnp.concatenate(o_inter, axis=0))
    return jnp.concatenate(outs, axis=1)


def _gla_kernel(q_ref, k_ref, v_ref, lr_ref, r_ref, wgf_ref, bgf_ref, wgb_ref, bgb_ref, gain_ref,
                o_ref, state_ref, oacc_ref, *, nblk):
    i = pl.program_id(2)
    rows = q_ref.shape[0]
    heads, _, dv = state_ref.shape

    @pl.when((i == 0) | (i == nblk))
    def _():
        state_ref[...] = jnp.zeros_like(state_ref)

    @pl.when(i < nblk)
    def _():
        o = _gla_direction(q_ref, k_ref, v_ref, lr_ref, wgf_ref, bgf_ref, state_ref, False)
        oacc_ref[pl.ds(pl.multiple_of(i * rows, rows), rows), :] = o

    @pl.when(i >= nblk)
    def _():
        j = 2 * nblk - 1 - i
        o = _gla_direction(q_ref, k_ref, v_ref, lr_ref, wgb_ref, bgb_ref, state_ref, True)
        o = o + oacc_ref[pl.ds(pl.multiple_of(j * rows, rows), rows), :]
        gain = gain_ref[...]
        o = jnp.concatenate([_rms(o[:, h * dv:(h + 1) * dv], gain[:, h * dv:(h + 1) * dv])
                             for h in range(heads)], axis=1)
        o_ref[...] = (o * _silu(r_ref[...].astype(F32))).astype(o_ref.dtype)


def _gla(q, k, v, lr, r, wgf, bgf, wgb, bgb, gain, batch, seq):
    n = q.shape[0]
    heads = GLA_HEADS
    hps = GLA_HEADS_PER_STEP
    dk = q.shape[1] // heads
    dv = v.shape[1] // heads
    rows = GLA_BLOCK
    nblk = seq // rows
    assert seq % rows == 0 and rows % GLA_GROUP == 0 and GLA_GROUP % GLA_CHUNK == 0 and heads % hps == 0

    def blk(i):
        return jnp.where(i < nblk, i, 2 * nblk - 1 - i)

    def late_blk(i):
        return jnp.where(i < nblk, nblk - 1, 2 * nblk - 1 - i)

    row_map = lambda b, h, i: (b * nblk + blk(i), h)
    return pl.pallas_call(
        functools.partial(_gla_kernel, nblk=nblk),
        grid=(batch, heads // hps, 2 * nblk),
        in_specs=[
            pl.BlockSpec((rows, hps * dk), row_map),
            pl.BlockSpec((rows, hps * dk), row_map),
            pl.BlockSpec((rows, hps * dv), row_map),
            pl.BlockSpec((rows, lr.shape[1]), lambda b, h, i: (b * nblk + blk(i), 0)),
            pl.BlockSpec((rows, hps * dv), lambda b, h, i: (b * nblk + late_blk(i), h)),
            pl.BlockSpec((wgf.shape[0], hps * dk), lambda b, h, i: (0, h)),
            pl.BlockSpec((1, hps * dk), lambda b, h, i: (0, h)),
            pl.BlockSpec((wgb.shape[0], hps * dk), lambda b, h, i: (0, h)),
            pl.BlockSpec((1, hps * dk), lambda b, h, i: (0, h)),
            pl.BlockSpec((1, hps * dv), lambda b, h, i: (0, h)),
        ],
        out_specs=pl.BlockSpec((rows, hps * dv), lambda b, h, i: (b * nblk + late_blk(i), h)),
        out_shape=jax.ShapeDtypeStruct((n, heads * dv), BF16),
        scratch_shapes=[pltpu.VMEM((hps, dk, dv), F32), pltpu.VMEM((seq, hps * dv), F32)],
        compiler_params=_params("arbitrary", "arbitrary", "arbitrary"),
        name="gla",
    )(q, k, v, lr, r, wgf, bgf, wgb, bgb, gain)


def _proj_residual_router_kernel(a_ref, w_ref, b_ref, h_ref, g_ref, rhl_ref,
                                 h_out_ref, hn_out_ref, meta_ref, count_ref, tilecnt_ref, *, n_experts):
    i = pl.program_id(0)
    h = h_ref[...] + (jnp.dot(a_ref[...], w_ref[...], preferred_element_type=F32) + b_ref[...])
    h_out_ref[...] = h
    hn = _rms(h, g_ref[...])
    hn_out_ref[...] = hn.astype(hn_out_ref.dtype)

    hn_hi = hn.astype(BF16)
    hn_lo = (hn - hn_hi.astype(F32)).astype(BF16)
    hh = jnp.dot(hn_hi, rhl_ref[...], preferred_element_type=F32)
    logits = (hh[:, :LANES] + hh[:, LANES:]
              + jnp.dot(hn_lo, rhl_ref[:, :LANES], preferred_element_type=F32))
    tm = logits.shape[0]
    lane = lax.broadcasted_iota(jnp.int32, logits.shape, 1).astype(F32)
    neg = jnp.float32(-jnp.inf)
    logits = jnp.where(lane < n_experts, logits, neg)
    m1 = jnp.max(logits, axis=-1, keepdims=True)
    i1 = jnp.min(jnp.where(logits == m1, lane, float(LANES)), axis=-1, keepdims=True)
    rest = jnp.where(lane == i1, neg, logits)
    m2 = jnp.max(rest, axis=-1, keepdims=True)
    i2 = jnp.min(jnp.where(rest == m2, lane, float(LANES)), axis=-1, keepdims=True)
    e2 = jnp.exp(m2 - m1)
    w1 = 1.0 / (1.0 + e2)
    w2 = e2 / (1.0 + e2)

    @pl.when(i == 0)
    def _():
        count_ref[...] = jnp.zeros_like(count_ref)

    sel = (lane == i1) | (lane == i2)
    onehot = jnp.where(sel, 1.0, 0.0)
    ri = lax.broadcasted_iota(jnp.int32, (tm, tm), 0)
    ci = lax.broadcasted_iota(jnp.int32, (tm, tm), 1)
    strict_lower = jnp.where(ci < ri, 1.0, 0.0).astype(BF16)
    rank = jnp.dot(strict_lower, onehot.astype(BF16), preferred_element_type=F32) + count_ref[...]
    r1 = jnp.sum(jnp.where(lane == i1, rank, 0.0), axis=-1, keepdims=True)
    r2 = jnp.sum(jnp.where(lane == i2, rank, 0.0), axis=-1, keepdims=True)
    count_ref[...] = count_ref[...] + jnp.sum(onehot, axis=0, keepdims=True)
    sub = tm // tilecnt_ref.shape[0]
    tilecnt_ref[...] = jnp.concatenate(
        [jnp.sum(onehot[u * sub:(u + 1) * sub], axis=0, keepdims=True) for u in range(tilecnt_ref.shape[0])],
        axis=0)

    meta = jnp.where(lane == 0, i1, 0.0)
    meta = jnp.where(lane == 1, i2, meta)
    meta = jnp.where(lane == 2, w1, meta)
    meta = jnp.where(lane == 3, w2, meta)
    meta = jnp.where(lane == 4, r1, meta)
    meta = jnp.where(lane == 5, r2, meta)
    meta_ref[...] = meta[:, :meta_ref.shape[1]]


def _proj_residual_router(a, w, bias, h, gain, router, name):
    n, kdim = a.shape
    d = w.shape[1]
    tm = ROW_TILE
    assert n % tm == 0 and tm % COMBINE_TILE == 0
    sub = tm // COMBINE_TILE
    n_experts = router.shape[1]
    rpad = jnp.zeros((d, LANES), F32).at[:, :n_experts].set(router)
    rhi = rpad.astype(BF16)
    rlo = (rpad - rhi.astype(F32)).astype(BF16)
    row = lambda i: (i, 0)
    fixed = lambda i: (0, 0)
    return pl.pallas_call(
        functools.partial(_proj_residual_router_kernel, n_experts=n_experts),
        grid=(n // tm,),
        in_specs=[
            pl.BlockSpec((tm, kdim), row),
            pl.BlockSpec((kdim, d), fixed),
            pl.BlockSpec((1, d), fixed),
            pl.BlockSpec((tm, d), row),
            pl.BlockSpec((1, d), fixed),
            pl.BlockSpec((d, 2 * LANES), fixed),
        ],
        out_specs=[pl.BlockSpec((tm, d), row), pl.BlockSpec((tm, d), row),
                   pl.BlockSpec((tm, META_WIDTH), row), pl.BlockSpec((1, LANES), fixed),
                   pl.BlockSpec((None, sub, LANES), lambda i: (i, 0, 0))],
        out_shape=[jax.ShapeDtypeStruct((n, d), F32), jax.ShapeDtypeStruct((n, d), BF16),
                   jax.ShapeDtypeStruct((n, META_WIDTH), F32), jax.ShapeDtypeStruct((1, LANES), F32),
                   jax.ShapeDtypeStruct((n // tm, sub, LANES), F32)],
        compiler_params=_params("arbitrary"),
        name=name,
    )(a, w, bias.reshape(1, d), h, gain.reshape(1, d), jnp.concatenate([rhi, rlo], axis=1))


def _swiglu_step(x, wgu, wd, acc_ref):
    tf = wd.shape[0]
    gu = jnp.dot(x, wgu, preferred_element_type=F32)
    a = (_silu(gu[:, :tf]) * gu[:, tf:]).astype(BF16)
    acc_ref[...] += jnp.dot(a, wd, preferred_element_type=F32)


def _moe_swiglu_kernel(te_ref, nu_ref, x_ref, wg_ref, wu_ref, wd_ref, o_ref, acc_ref):
    i = pl.program_id(0)
    j = pl.program_id(1)
    used = i < nu_ref[0]

    @pl.when(used & (j == 0))
    def _():
        acc_ref[...] = jnp.zeros_like(acc_ref)

    @pl.when(used)
    def _():
        wgu = jnp.concatenate([wg_ref[...].astype(BF16), wu_ref[...].astype(BF16)], axis=1)
        _swiglu_step(x_ref[...].astype(BF16), wgu, wd_ref[...].astype(BF16), acc_ref)

    @pl.when(j == pl.num_programs(1) - 1)
    def _():
        @pl.when(used)
        def _():
            o_ref[...] = acc_ref[...].astype(o_ref.dtype)

        @pl.when(jnp.logical_not(used))
        def _():
            o_ref[...] = jnp.zeros_like(o_ref)


def _swiglu_weights(wg, wu, wd):
    e, d, f = wg.shape
    tf = FFN_F_TILE
    assert f % tf == 0
    nf = f // tf
    tiles = []
    for j in range(nf):
        tiles += [wg[:, :, j * tf:(j + 1) * tf], wu[:, :, j * tf:(j + 1) * tf]]
    wgu = jnp.concatenate(tiles, axis=-1).astype(BF16)
    return wgu, wd.astype(BF16).reshape(e, nf, tf, d)


def _moe_swiglu(x, wg, wu, wd, tile_expert, n_used, tm, out_dtype):
    rows, d = x.shape
    tf = FFN_F_TILE
    f = wg.shape[2]
    assert rows % tm == 0 and f % tf == 0
    nf = f // tf

    def fcol(i, j, nu):
        return jnp.where(i < nu[0], j, nf - 1)

    def xmap(i, j, te, nu):
        return (jnp.where(i < nu[0], i, nu[0] - 1), 0)

    return pl.pallas_call(
        _moe_swiglu_kernel,
        grid_spec=pltpu.PrefetchScalarGridSpec(
            num_scalar_prefetch=2,
            grid=(rows // tm, nf),
            in_specs=[
                pl.BlockSpec((tm, d), xmap),
                pl.BlockSpec((None, d, tf), lambda i, j, te, nu: (te[i], 0, fcol(i, j, nu))),
                pl.BlockSpec((None, d, tf), lambda i, j, te, nu: (te[i], 0, fcol(i, j, nu))),
                pl.BlockSpec((None, tf, d), lambda i, j, te, nu: (te[i], fcol(i, j, nu), 0)),
            ],
            out_specs=pl.BlockSpec((tm, d), lambda i, j, te, nu: (i, 0)),
            scratch_shapes=[pltpu.VMEM((tm, d), F32)],
        ),
        out_shape=jax.ShapeDtypeStruct((rows, d), out_dtype),
        compiler_params=_params("arbitrary", "arbitrary"),
        name="moe_swiglu",
    )(tile_expert, n_used, x, wg, wu, wd)


def _proj_swiglu_kernel(a_ref, wo_ref, h_ref, g_ref, wgu_ref, wd_ref, o_ref, hres_ref, xn_ref, acc_ref):
    j = pl.program_id(1)

    @pl.when(j == 0)
    def _():
        h = h_ref[...] + jnp.dot(a_ref[...], wo_ref[...], preferred_element_type=F32)
        hres_ref[...] = h
        xn_ref[...] = _rms(h, g_ref[...]).astype(xn_ref.dtype)
        acc_ref[...] = jnp.zeros_like(acc_ref)

    _swiglu_step(xn_ref[...], wgu_ref[...], wd_ref[...], acc_ref)

    @pl.when(j == pl.num_programs(1) - 1)
    def _():
        o_ref[...] = hres_ref[...] + acc_ref[...]


def _proj_swiglu(a, wo, h, gain, wgu, wd):
    n, kdim = a.shape
    d = wo.shape[1]
    nf, tf = wd.shape[1], wd.shape[2]
    tm = FFN_TILE
    assert n % tm == 0
    return pl.pallas_call(
        _proj_swiglu_kernel,
        grid=(n // tm, nf),
        in_specs=[
            pl.BlockSpec((tm, kdim), lambda i, j: (i, 0)),
            pl.BlockSpec((kdim, d), lambda i, j: (0, 0)),
            pl.BlockSpec((tm, d), lambda i, j: (i, 0)),
            pl.BlockSpec((1, d), lambda i, j: (0, 0)),
            pl.BlockSpec((None, d, 2 * tf), lambda i, j: (0, 0, j)),
            pl.BlockSpec((None, None, tf, d), lambda i, j: (0, j, 0, 0)),
        ],
        out_specs=pl.BlockSpec((tm, d), lambda i, j: (i, 0)),
        out_shape=jax.ShapeDtypeStruct((n, d), F32),
        scratch_shapes=[pltpu.VMEM((tm, d), F32), pltpu.VMEM((tm, d), BF16), pltpu.VMEM((tm, d), F32)],
        compiler_params=_params("arbitrary", "arbitrary"),
        name="gla_out_proj_dense_swiglu",
    )(a, wo, h, gain.reshape(1, d), wgu, wd)


def _alibi_slope(head):
    return float(2.0 ** (-8.0 * (head + 1) / N_Q_HEADS))


def _swa_bias(t):
    qi = jnp.arange(t)[:, None]
    kj = jnp.arange(3 * t)[None, :]
    dist = jnp.abs(qi + t - kj)
    band = dist <= WINDOW
    valid = jnp.stack([band & (kj >= t), band, band & (kj < 2 * t)])
    slopes = jnp.asarray([_alibi_slope(h) for h in range(N_Q_HEADS)], F32)
    bias = -slopes[None, :, None, None] * dist.astype(F32)[None, None]
    bias = jnp.where(valid[:, None], bias, -jnp.inf)
    return bias.reshape(3, N_KV_HEADS, GROUP * t, 3 * t)


def _swa_kernel(sink_ref, q_ref, kp_ref, kc_ref, kn_ref, vp_ref, vc_ref, vn_ref, bias0_ref, bias1_ref,
                o_ref):
    t = ATT_BLOCK
    lane = lax.broadcasted_iota(jnp.int32, (t, LANES), 1)
    low = lane < HEAD_DIM
    row_head = lax.broadcasted_iota(jnp.int32, (GROUP * t, 1), 0) // t
    ones = jnp.ones((3 * t, LANES), BF16)

    def window(p_ref, c_ref, n_ref, kvh, sub):
        ks = slice(kvh * LANES, (kvh + 1) * LANES)
        if sub == 0:
            return jnp.concatenate([p_ref[t:, ks], c_ref[:, ks]], axis=0)
        return jnp.concatenate([c_ref[:, ks], n_ref[:t, ks]], axis=0)

    row_blocks = []
    for sub, bias_ref in enumerate((bias0_ref, bias1_ref)):
        rows = slice(sub * t, (sub + 1) * t)
        blocks = []
        for kvh in range(N_KV_HEADS):
            qs = []
            for g in range(GROUP):
                head = kvh * GROUP + g
                col = (head // 2) * LANES
                q2 = q_ref[rows, col:col + LANES]
                qs.append(jnp.where(low if head % 2 == 0 else jnp.logical_not(low), q2,
                                    jnp.zeros_like(q2)))
            qg = jnp.concatenate(qs, axis=0)
            kw = window(kp_ref, kc_ref, kn_ref, kvh, sub)
            s = lax.dot_general(qg, kw, (((1,), (1,)), ((), ())), preferred_element_type=F32)
            s = s + bias_ref[kvh]
            sink = jnp.full((GROUP * t, 1), sink_ref[kvh * GROUP], F32)
            for g in range(1, GROUP):
                sink = jnp.where(row_head == g, sink_ref[kvh * GROUP + g], sink)
            m = jnp.maximum(jnp.max(s, axis=-1, keepdims=True), sink)
            p = jnp.exp(s - m).astype(BF16)
            vw = jnp.concatenate([window(vp_ref, vc_ref, vn_ref, kvh, sub), ones], axis=1)
            res = jnp.dot(p, vw, preferred_element_type=F32)
            out = res[:, :LANES] * (1.0 / (res[:, LANES:] + jnp.exp(sink - m)))
            for pair in range(GROUP // 2):
                a = out[(2 * pair) * t:(2 * pair + 1) * t]
                b = out[(2 * pair + 1) * t:(2 * pair + 2) * t]
                blocks.append(jnp.where(low, a, b).astype(o_ref.dtype))
        row_blocks.append(jnp.concatenate(blocks, axis=1))
    o_ref[...] = jnp.concatenate(row_blocks, axis=0)


def _swa(q, kdup, vdup, sinks, batch, seq):
    n, qw = q.shape
    t = ATT_BLOCK
    rows = 2 * t
    nblk = seq // rows
    assert seq % rows == 0 and nblk >= 2
    kvw = kdup.shape[1]
    bias = _swa_bias(t)
    prev = lambda b, i, s: (b * nblk + jnp.maximum(i - 1, 0), 0)
    cur = lambda b, i, s: (b * nblk + i, 0)
    nxt = lambda b, i, s: (b * nblk + jnp.minimum(i + 1, nblk - 1), 0)
    first = lambda b, i, s: (jnp.where(i == 0, 0, 1), 0, 0, 0)
    second = lambda b, i, s: (jnp.where(i == nblk - 1, 2, 1), 0, 0, 0)
    return pl.pallas_call(
        _swa_kernel,
        grid_spec=pltpu.PrefetchScalarGridSpec(
            num_scalar_prefetch=1,
            grid=(batch, nblk),
            in_specs=[
                pl.BlockSpec((rows, qw), cur),
                pl.BlockSpec((rows, kvw), prev), pl.BlockSpec((rows, kvw), cur), pl.BlockSpec((rows, kvw), nxt),
                pl.BlockSpec((rows, kvw), prev), pl.BlockSpec((rows, kvw), cur), pl.BlockSpec((rows, kvw), nxt),
                pl.BlockSpec((None,) + bias.shape[1:], first),
                pl.BlockSpec((None,) + bias.shape[1:], second),
            ],
            out_specs=pl.BlockSpec((rows, qw), cur),
        ),
        out_shape=jax.ShapeDtypeStruct((n, qw), BF16),
        compiler_params=_params("arbitrary", "arbitrary"),
        name="swa",
    )(sinks, q, kdup, kdup, kdup, vdup, vdup, vdup, bias, bias)


SEG_ALIGN = 8
SEG_BIG = 64
RUN_FIELDS = 7


def _block_rows(tm, n_experts):
    worst = TOP_K * tm + 2 * (SEG_ALIGN - 1) * n_experts
    return -(-worst // LANES) * LANES


def _pair_block_rows(meta, base_ref, tile, n_experts):
    row1, row2 = meta[:, 4:5], meta[:, 5:6]
    for e in range(n_experts):
        base = base_ref[tile * n_experts + e]
        row1 = row1 + jnp.where(meta[:, 0:1] == float(e), base, 0.0)
        row2 = row2 + jnp.where(meta[:, 1:2] == float(e), base, 0.0)
    return row1, row2


def _for_pieces(n_rows, make_copy, act):
    n_big = lax.shift_right_logical(n_rows, 6)
    n_small = lax.shift_right_logical(n_rows - n_big * SEG_BIG, 3)

    def big(j, carry):
        act(make_copy(j * SEG_BIG, SEG_BIG))
        return carry

    def small(j, carry):
        act(make_copy(n_big * SEG_BIG + j * SEG_ALIGN, SEG_ALIGN))
        return carry

    lax.fori_loop(0, n_big, big, 0)
    lax.fori_loop(0, n_small, small, 0)


def _for_rows(first, count, make_copy, act):
    def one(j, carry):
        act(make_copy(first + j))
        return carry

    lax.fori_loop(0, count, one, 0)


def _start(copy):
    copy.start()


def _wait(copy):
    copy.wait()


def _scatter_kernel(runs_ref, totals_ref, base_ref, pads_ref, x_ref, meta_ref, xs_ref, blk_ref, zero_ref,
                    sems, zero_sem, *, n_experts):
    i = pl.program_id(0)
    steps = pl.num_programs(0)
    tm = x_ref.shape[0]
    rows = blk_ref.shape[1]
    n_runs = steps * n_experts

    def start_tile(tile):
        slot = lax.rem(tile, 2)
        for e in range(n_experts):
            k = tile * n_experts + e
            head0, head_n, mid0, mid_len, tail0, tail_n, shift = [runs_ref[f * n_runs + k]
                                                                  for f in range(RUN_FIELDS)]

            def row_copy(j, head0=head0, head_n=head_n, tail0=tail0, shift=shift):
                r = jnp.where(j < head_n, head0 + j, tail0 + (j - head_n))
                return pltpu.make_async_copy(blk_ref.at[slot, pl.ds(r - shift, 1)],
                                             xs_ref.at[pl.ds(r, 1)], sems.at[slot])

            def piece(off, size, mid0=mid0, shift=shift):
                src = pl.multiple_of(mid0 - shift + off, SEG_ALIGN)
                dst = pl.multiple_of(mid0 + off, SEG_ALIGN)
                return pltpu.make_async_copy(blk_ref.at[slot, pl.ds(src, size)],
                                             xs_ref.at[pl.ds(dst, size)], sems.at[slot])

            _for_rows(0, head_n + tail_n, row_copy, _start)
            _for_pieces(mid_len, piece, _start)

    def wait_tile(tile):
        slot = lax.rem(tile, 2)
        for f, size in enumerate((1, SEG_BIG, SEG_ALIGN)):
            def same_size_copy(j, size=size):
                return pltpu.make_async_copy(blk_ref.at[slot, pl.ds(0, size)], xs_ref.at[pl.ds(0, size)],
                                             sems.at[slot])

            _for_rows(0, totals_ref[f * steps + tile], same_size_copy, _wait)

    @pl.when(i == 0)
    def _():
        zero_ref[...] = jnp.zeros_like(zero_ref)
        for act in (_start, _wait):
            for e in range(n_experts):
                pad0, head_n, mid_len = [pads_ref[f * n_experts + e] for f in range(3)]

                def zero_row(r):
                    return pltpu.make_async_copy(zero_ref.at[pl.ds(0, 1)], xs_ref.at[pl.ds(r, 1)], zero_sem)

                def zero_piece(off, size, pad0=pad0, head_n=head_n):
                    dst = pl.multiple_of(pad0 + head_n + off, SEG_ALIGN)
                    return pltpu.make_async_copy(zero_ref.at[pl.ds(0, size)], xs_ref.at[pl.ds(dst, size)],
                                                 zero_sem)

                _for_rows(pad0, head_n, zero_row, act)
                _for_pieces(mid_len, zero_piece, act)

    @pl.when(i >= 2)
    def _():
        wait_tile(i - 2)

    row1, row2 = _pair_block_rows(meta_ref[...], base_ref, i, n_experts)
    col = lax.broadcasted_iota(jnp.int32, (tm, rows), 1).astype(F32)
    pick = jnp.where((col == row1) | (col == row2), 1.0, 0.0).astype(BF16)
    blk_ref[lax.rem(i, 2)] = lax.dot_general(pick, x_ref[...], (((0,), (0,)), ((), ())),
                                             preferred_element_type=F32)
    start_tile(i)

    @pl.when(i == steps - 1)
    def _():
        @pl.when(i >= 1)
        def _():
            wait_tile(i - 1)

        wait_tile(i)


def _scatter_rows(x, meta, runs, totals, row_base, pads, total_rows, n_experts):
    n, d = x.shape
    tm = SCATTER_TILE
    assert n % tm == 0
    rows = _block_rows(tm, n_experts)
    smem = pl.BlockSpec(memory_space=pltpu.SMEM)
    return pl.pallas_call(
        functools.partial(_scatter_kernel, n_experts=n_experts),
        grid=(n // tm,),
        in_specs=[
            smem, smem, smem, smem,
            pl.BlockSpec((tm, d), lambda i: (i, 0)),
            pl.BlockSpec((tm, META_WIDTH), lambda i: (i, 0)),
        ],
        out_specs=pl.BlockSpec(memory_space=pl.ANY),
        out_shape=jax.ShapeDtypeStruct((total_rows, d), F32),
        scratch_shapes=[pltpu.VMEM((2, rows, d), F32), pltpu.VMEM((SEG_BIG, d), F32),
                        pltpu.SemaphoreType.DMA((2,)), pltpu.SemaphoreType.DMA],
        compiler_params=_params("arbitrary"),
        name="moe_scatter",
    )(runs, totals, row_base, pads, x, meta)


def _combine_kernel(src_ref, len_ref, dst_ref, totals_ref, base_ref, h_ref, meta_ref, g_ref, ys_ref, o_ref,
                    yblk_ref, sems, *, n_experts):
    i = pl.program_id(0)
    steps = pl.num_programs(0)
    tm = h_ref.shape[0]
    rows = yblk_ref.shape[1]

    def start_tile(tile):
        slot = lax.rem(tile, 2)
        for e in range(n_experts):
            k = tile * n_experts + e

            def piece(off, size, src0=src_ref[k], dst0=dst_ref[k]):
                src = pl.multiple_of(src0 + off, SEG_ALIGN)
                dst = pl.multiple_of(dst0 + off, SEG_ALIGN)
                return pltpu.make_async_copy(ys_ref.at[pl.ds(src, size)],
                                             yblk_ref.at[slot, pl.ds(dst, size)], sems.at[slot])

            _for_pieces(len_ref[k], piece, _start)

    def wait_tile(tile):
        slot = lax.rem(tile, 2)
        for f, size in enumerate((SEG_BIG, SEG_ALIGN)):
            def same_size_copy(j, size=size):
                return pltpu.make_async_copy(ys_ref.at[pl.ds(0, size)], yblk_ref.at[slot, pl.ds(0, size)],
                                             sems.at[slot])

            _for_rows(0, totals_ref[f * steps + tile], same_size_copy, _wait)

    @pl.when(i == 0)
    def _():
        yblk_ref[...] = jnp.zeros_like(yblk_ref)
        start_tile(i)

    @pl.when(i + 1 < steps)
    def _():
        start_tile(i + 1)

    wait_tile(i)

    meta = meta_ref[...]
    w1, w2 = meta[:, 2:3], meta[:, 3:4]
    row1, row2 = _pair_block_rows(meta, base_ref, i, n_experts)
    col = lax.broadcasted_iota(jnp.int32, (tm, rows), 1).astype(F32)
    pick = jnp.concatenate([jnp.where(col == row1, 1.0, 0.0),
                            jnp.where(col == row2, 1.0, 0.0)], axis=0).astype(BF16)
    yblk = yblk_ref[lax.rem(i, 2)].astype(BF16)
    y = jnp.dot(pick, yblk, preferred_element_type=F32)
    o_ref[...] = _rms(h_ref[...] + (w1 * y[:tm] + w2 * y[tm:]), g_ref[...])


def _combine(h, meta, gain, ys, seg_src, seg_len, seg_dst, row_base, n_experts):
    n, d = h.shape
    tm = COMBINE_TILE
    assert n % tm == 0
    rows = _block_rows(tm, n_experts)
    smem = pl.BlockSpec(memory_space=pltpu.SMEM)
    lens = seg_len.reshape(n // tm, n_experts)
    totals = jnp.stack([jnp.sum(lens // SEG_BIG, axis=1),
                        jnp.sum(lens % SEG_BIG // SEG_ALIGN, axis=1)]).reshape(-1)
    return pl.pallas_call(
        functools.partial(_combine_kernel, n_experts=n_experts),
        grid=(n // tm,),
        in_specs=[
            smem, smem, smem, smem, smem,
            pl.BlockSpec((tm, d), lambda i: (i, 0)),
            pl.BlockSpec((tm, META_WIDTH), lambda i: (i, 0)),
            pl.BlockSpec((1, d), lambda i: (0, 0)),
            pl.BlockSpec(memory_space=pl.ANY),
        ],
        out_specs=pl.BlockSpec((tm, d), lambda i: (i, 0)),
        out_shape=jax.ShapeDtypeStruct((n, d), F32),
        scratch_shapes=[pltpu.VMEM((2, rows, d), ys.dtype), pltpu.SemaphoreType.DMA((2,))],
        compiler_params=_params("arbitrary"),
        name="moe_combine",
    )(seg_src, seg_len, seg_dst, totals, row_base, h, meta, gain.reshape(1, d), ys)


def kernel(x, mix_norm, ffn_norm, gla_in_proj, gla_gate_w_fwd, gla_gate_b_fwd, gla_gate_w_bwd, gla_gate_b_bwd, gla_head_norm, gla_out_proj, swa_qkv_proj, swa_qkv_bias, swa_sinks, swa_out_proj, swa_out_bias, dense_w_gate, dense_w_up, dense_w_down, moe_router, moe_w_gate, moe_w_up, moe_w_down, final_norm):
    batch, seq, d = x.shape
    n = batch * seq
    h0 = x.reshape(n, d)

    key_w = gla_gate_w_fwd.shape[2]
    val_w = gla_head_norm.shape[1]
    rank = GLA_GATE_RANK
    in_w = gla_in_proj.shape[2]
    splits = ((0, key_w), (key_w, key_w), (2 * key_w, val_w), (2 * key_w + val_w, val_w),
              (2 * key_w + 2 * val_w, 2 * rank))
    q, k, v, r, lr = _norm_proj(h0, mix_norm[0], gla_in_proj[0].astype(BF16), jnp.zeros((in_w,), F32),
                                splits, (BF16, BF16, BF16, BF16, F32), "gla_in_proj")
    zero_gate = jnp.zeros((rank, key_w), F32)
    wgf = jnp.concatenate([gla_gate_w_fwd[0], zero_gate], axis=0).astype(BF16)
    wgb = jnp.concatenate([zero_gate, gla_gate_w_bwd[0]], axis=0).astype(BF16)
    og = _gla(q, k, v, lr, r, wgf, gla_gate_b_fwd[0].reshape(1, key_w), wgb,
              gla_gate_b_bwd[0].reshape(1, key_w), gla_head_norm[0].reshape(1, val_w), batch, seq)

    dense_wgu, dense_wd = _swiglu_weights(dense_w_gate, dense_w_up, dense_w_down)
    h2 = _proj_swiglu(og, gla_out_proj[0].astype(BF16), h0, ffn_norm[0], dense_wgu, dense_wd)

    qw = N_Q_HEADS * HEAD_DIM
    kvw = N_KV_HEADS * HEAD_DIM
    scale = HEAD_DIM ** -0.5
    wqkv, bqkv = swa_qkv_proj[0], swa_qkv_bias[0]

    def dup(m):
        lead = m.shape[:-1]
        m = m.reshape(lead + (N_KV_HEADS, 1, HEAD_DIM))
        return jnp.broadcast_to(m, lead + (N_KV_HEADS, 2, HEAD_DIM)).reshape(lead + (2 * kvw,))

    w_aug = jnp.concatenate([wqkv[:, :qw] * scale, dup(wqkv[:, qw:qw + kvw]), dup(wqkv[:, qw + kvw:])], axis=1)
    b_aug = jnp.concatenate([bqkv[:qw] * scale, dup(bqkv[qw:qw + kvw]), dup(bqkv[qw + kvw:])], axis=0)
    splits = ((0, qw), (qw, 2 * kvw), (qw + 2 * kvw, 2 * kvw))
    aq, ak, av = _norm_proj(h2, mix_norm[1], w_aug.astype(BF16), b_aug, splits, (BF16, BF16, BF16),
                            "swa_qkv_proj")
    oa = _swa(aq, ak, av, swa_sinks[0], batch, seq)

    n_experts = moe_router.shape[2]
    h3, hn3, meta, counts, tile_counts = _proj_residual_router(
        oa, swa_out_proj[0].astype(BF16), swa_out_bias[0], h2, ffn_norm[1], moe_router[0], "swa_out_proj")
    tm = MOE_TILE
    counts = counts[0, :n_experts].astype(jnp.int32)
    tiles_per_expert = (counts + tm - 1) // tm
    tile_end = jnp.cumsum(tiles_per_expert)
    offsets = (tile_end - tiles_per_expert) * tm
    n_tiles = (TOP_K * n) // tm + n_experts
    tile_ids = jnp.arange(n_tiles, dtype=jnp.int32)
    tile_expert = jnp.sum((tile_ids[:, None] >= tile_end[None, :]).astype(jnp.int32), axis=1)
    tile_expert = jnp.minimum(tile_expert, n_experts - 1)
    n_used = tile_end[-1:].astype(jnp.int32)

    align = SEG_ALIGN
    fine_cnt = tile_counts.reshape(-1, LANES)[:, :n_experts].astype(jnp.int32)

    def run_tables(tile_cnt):
        run_start = offsets[None, :] + jnp.cumsum(tile_cnt, axis=0) - tile_cnt
        run_end = run_start + tile_cnt
        seg_src = run_start - run_start % align
        seg_len = jnp.where(tile_cnt > 0, (run_end + align - 1) // align * align - seg_src, 0)
        seg_dst = jnp.cumsum(seg_len, axis=1) - seg_len
        row_base = (offsets[None, :] + seg_dst - seg_src).astype(F32).reshape(-1)
        return run_start, run_end, seg_src, seg_len, seg_dst, row_base

    coarse_cnt = jnp.sum(fine_cnt.reshape(-1, SCATTER_TILE // COMBINE_TILE, n_experts), axis=1)
    run_start, run_end, seg_src, seg_len, seg_dst, scatter_base = run_tables(coarse_cnt)
    mid_start = (run_start + align - 1) // align * align
    mid_end = run_end - run_end % align
    has_mid = mid_end >= mid_start
    head_n = jnp.where(has_mid, mid_start - run_start, coarse_cnt)
    mid_len = jnp.where(has_mid, mid_end - mid_start, 0)
    tail_n = jnp.where(has_mid, run_end - mid_end, 0)
    runs = jnp.stack([run_start, head_n, mid_start, mid_len, mid_end, tail_n, seg_src - seg_dst]).reshape(-1)
    totals = jnp.stack([jnp.sum(head_n + tail_n, axis=1), jnp.sum(mid_len // SEG_BIG, axis=1),
                        jnp.sum(mid_len % SEG_BIG // align, axis=1)]).reshape(-1)
    pad_start = offsets + counts
    pad_count = tiles_per_expert * tm - counts
    pad_count = pad_count.at[n_experts - 1].set(n_tiles * tm - pad_start[n_experts - 1])
    pad_head = jnp.minimum(pad_count, (align - pad_start % align) % align)
    pads = jnp.stack([pad_start, pad_head, pad_count - pad_head]).reshape(-1)

    xs = _scatter_rows(hn3, meta, runs, totals, scatter_base, pads, n_tiles * tm, n_experts)
    ys = _moe_swiglu(xs, moe_w_gate[0], moe_w_up[0], moe_w_down[0], tile_expert, n_used, tm, F32)
    _, _, seg_src, seg_len, seg_dst, combine_base = run_tables(fine_cnt)
    out = _combine(h3, meta, final_norm, ys, seg_src.reshape(-1), seg_len.reshape(-1),
                   seg_dst.reshape(-1), combine_base, n_experts)
    return out.reshape(batch, seq, d)
```
